```python
import math
import jax
import jax.numpy as jnp
from jax import lax
import numpy as np

D_MODEL = 1024
BATCH = 32
SEQ = 256
DEPTH = 4
DEC_BATCH = 8
DEC_SEQ = 2048
PAST_LEN = 256

GRID_W = 64
N_MIXERS = 4
MIX_W = D_MODEL
GROUP_W = MIX_W // N_MIXERS
EPS = 1e-6
SSD_HEADDIM = 64
SSD_HEADS = GROUP_W // SSD_HEADDIM
SSD_GROUPS = 2
SSD_STATE = 64
SSD_CONV = 4
SSD_CHUNK = 128
SSD_BC = SSD_GROUPS * SSD_STATE
SSD_CONV_CH = GROUP_W + 2 * SSD_BC
POOL_WINDOWS = (2, 4, 8, 16)
POOL_GW = GROUP_W // len(POOL_WINDOWS)
DA_HEADS = 4
DA_VDIM = GROUP_W // DA_HEADS
DA_QKDIM = DA_VDIM // 2
ROPE_BASE = 10000.0
Q_BLOCK = 128
RG_BLOCKS = 4
RG_BW = GROUP_W // RG_BLOCKS
RG_CONV = 4
RG_C = 8.0
N_EGROUPS = 4
N_EPG = 4
N_EXPERTS = N_EGROUPS * N_EPG
EXPERT_FF = 256
TOP_K = 2
IN_SIZES = (GROUP_W, SSD_BC, SSD_BC, GROUP_W, 2 * SSD_HEADS, GROUP_W, GROUP_W, GROUP_W, GROUP_W, GROUP_W, GROUP_W)
IN_TOTAL = sum(IN_SIZES)
SPLIT_POINTS = tuple(int(s) for s in np.cumsum(IN_SIZES)[:-1])

kernel_name = 'hybrid_diffusion_prefix_step'


def rmsnorm(x, g):
    xf = x.astype(jnp.float32)
    var = jnp.mean(xf * xf, axis=-1, keepdims=True)
    return (xf * lax.rsqrt(var + EPS)).astype(x.dtype) * g


def flip(t):
    return jnp.flip(t, axis=1)


def dwconv(x, w, b):
    K = w.shape[0]
    left = (K - 1) // 2
    y = lax.conv_general_dilated(x, w[:, None, :].astype(x.dtype), window_strides=(1,),
                                 padding=[(left, K - 1 - left)],
                                 dimension_numbers=('NWC', 'WIO', 'NWC'),
                                 feature_group_count=x.shape[-1])
    return y + b


def ssd_scan(x, dt, A, Bm, Cm, h0):
    b, l, h, p = x.shape
    nc = l // SSD_CHUNK
    r = lambda t: t.reshape(b, nc, SSD_CHUNK, *t.shape[2:])
    xc, dtc, Bc, Cc = r(x), r(dt), r(Bm), r(Cm)
    acs = jnp.cumsum(dtc * A, axis=2)
    xdt = xc * dtc[..., None]
    diff = acs[:, :, :, None, :] - acs[:, :, None, :, :]
    lower = jnp.tril(jnp.ones((SSD_CHUNK, SSD_CHUNK), dtype=bool))[None, None, :, :, None]
    Lmat = jnp.exp(jnp.where(lower, diff, -jnp.inf))
    scores = jnp.einsum('bcihn,bcjhn->bcijh', Cc, Bc) * Lmat
    y_diag = jnp.einsum('bcijh,bcjhp->bcihp', scores, xdt)
    decay_to_end = jnp.exp(acs[:, :, -1:, :] - acs)
    chunk_states = jnp.einsum('bcjhn,bcjh,bcjhp->bchpn', Bc, decay_to_end, xdt)
    chunk_decay = jnp.exp(acs[:, :, -1, :])

    def step(s, inp):
        st, dec = inp
        return s * dec[:, :, None, None] + st, s

    final, prev = lax.scan(step, h0, (chunk_states.swapaxes(0, 1), chunk_decay.swapaxes(0, 1)))
    prev = prev.swapaxes(0, 1)
    y_off = jnp.einsum('bcihn,bchpn,bcih->bcihp', Cc, prev, jnp.exp(acs))
    return (y_diag + y_off).reshape(b, l, h, p), final


def ssd_mixer(xs, bm, cm, z, dt_raw, conv_w, conv_b, dt_bias, a_log, d_skip, norm_g, h0):
    f32 = jnp.float32
    b, l, _ = xs.shape
    xbc = jax.nn.silu(dwconv(jnp.concatenate([xs, bm, cm], axis=-1), conv_w, conv_b))
    xs_c, bm_c, cm_c = jnp.split(xbc, (GROUP_W, GROUP_W + SSD_BC), axis=-1)
    x = xs_c.reshape(b, l, SSD_HEADS, SSD_HEADDIM).astype(f32)
    head_group = jnp.arange(SSD_HEADS) // (SSD_HEADS // SSD_GROUPS)
    Bh = bm_c.reshape(b, l, SSD_GROUPS, SSD_STATE)[:, :, head_group].astype(f32)
    Ch = cm_c.reshape(b, l, SSD_GROUPS, SSD_STATE)[:, :, head_group].astype(f32)
    dt = jax.nn.softplus(dt_raw.reshape(b, l, 2, SSD_HEADS).astype(f32) + dt_bias.astype(f32))
    A = -jnp.exp(a_log.astype(f32))
    h0 = h0.astype(f32)
    y_f, s_f = ssd_scan(x, dt[:, :, 0], A[0], Bh, Ch, h0[:, 0])
    y_b, s_b = ssd_scan(flip(x), flip(dt[:, :, 1]), A[1], flip(Bh), flip(Ch), h0[:, 1])
    y = y_f + flip(y_b) + x * d_skip.astype(f32)[:, None]
    y = y.reshape(b, l, GROUP_W).astype(xs.dtype) * jax.nn.silu(z)
    return rmsnorm(y, norm_g), jnp.stack([s_f, s_b], axis=1).astype(xs.dtype)


def multiscale_pool(u, w_pool, pool_scale):
    b, l, _ = u.shape
    ug = u.reshape(b, l, len(POOL_WINDOWS), POOL_GW)
    cs = jnp.concatenate([jnp.zeros((b, 1, len(POOL_WINDOWS), POOL_GW), jnp.float32),
                          jnp.cumsum(ug.astype(jnp.float32), axis=1)], axis=1)
    t = jnp.arange(l)
    means = []
    for gi, w in enumerate(POOL_WINDOWS):
        lo = w // 2
        hi = w - 1 - lo
        start = jnp.clip(t - lo, 0, l)
        end = jnp.clip(t + hi + 1, 0, l)
        csg = cs[:, :, gi]
        cnt = (end - start).astype(jnp.float32)[None, :, None]
        means.append((csg[:, end] - csg[:, start]) / cnt)
    mean = jnp.stack(means, axis=2).astype(u.dtype)
    mixed = jnp.einsum('blgc,gcd->blgd', mean - ug, w_pool)
    return mixed.reshape(b, l, GROUP_W) * pool_scale


def axial_rope(x):
    n_tok = x.shape[1]
    rows = n_tok // GRID_W
    row = jnp.repeat(jnp.arange(rows, dtype=jnp.float32), GRID_W)
    col = jnp.tile(jnp.arange(GRID_W, dtype=jnp.float32), rows)
    n_freq = DA_QKDIM // 4
    inv_freq = ROPE_BASE ** (-jnp.arange(n_freq, dtype=jnp.float32) / n_freq)
    ang = jnp.concatenate([row[:, None] * inv_freq, col[:, None] * inv_freq], axis=-1)
    cos = jnp.cos(ang)[None, :, None, None, :]
    sin = jnp.sin(ang)[None, :, None, None, :]
    xf = x.astype(jnp.float32).reshape(*x.shape[:-1], DA_QKDIM // 2, 2)
    x1, x2 = xf[..., 0], xf[..., 1]
    out = jnp.stack([x1 * cos - x2 * sin, x1 * sin + x2 * cos], axis=-1)
    return out.reshape(x.shape).astype(x.dtype)


def diff_attention(q, k, v, lam):
    b, lq = q.shape[:2]
    nblk = lq // Q_BLOCK
    qb = q.reshape(b, nblk, Q_BLOCK, *q.shape[2:]).swapaxes(0, 1)
    scale = DA_QKDIM ** -0.5

    def one(qblk):
        s = jnp.einsum('bqhmd,bkhmd->bhmqk', qblk, k, preferred_element_type=jnp.float32) * scale
        p = jax.nn.softmax(s, axis=-1)
        w = p[:, :, 0] - lam * p[:, :, 1]
        return jnp.einsum('bhqk,bkhd->bqhd', w.astype(v.dtype), v)

    out = lax.map(one, qb)
    return out.swapaxes(0, 1).reshape(b, lq, *out.shape[3:])


def diffattn_mixer(q, k, v, lam_init, lq1, lk1, lq2, lk2, norm_g, ctx_k, ctx_v):
    f32 = jnp.float32
    b, l, _ = q.shape
    q = q.reshape(b, l, DA_HEADS, 2, DA_QKDIM)
    k = k.reshape(b, l, DA_HEADS, 2, DA_QKDIM)
    v = v.reshape(b, l, DA_HEADS, DA_VDIM)
    lam = (jnp.exp(jnp.sum(lq1.astype(f32) * lk1.astype(f32)))
           - jnp.exp(jnp.sum(lq2.astype(f32) * lk2.astype(f32))) + lam_init)
    if ctx_k is None:
        keys, vals, qr = k, v, q
    else:
        qr = axial_rope(q)
        keys = jnp.concatenate([ctx_k.reshape(b, -1, DA_HEADS, 2, DA_QKDIM).astype(k.dtype), axial_rope(k)], axis=1)
        vals = jnp.concatenate([ctx_v.astype(v.dtype), v], axis=1)
    o = diff_attention(qr, keys, vals, lam)
    o = rmsnorm(o, norm_g) * (1.0 - lam_init)
    return o.reshape(b, l, GROUP_W), k.reshape(b, l, DA_HEADS, 2 * DA_QKDIM), v


def rglru_dir(xc, w_a, b_a, w_x, b_x, lam, h0):
    f32 = jnp.float32
    b, l, _ = xc.shape
    xb = xc.reshape(b, l, RG_BLOCKS, RG_BW)
    rg = jax.nn.sigmoid(jnp.einsum('blkc,kcd->blkd', xb, w_a.astype(f32)).reshape(b, l, GROUP_W) + b_a.astype(f32))
    ig = jax.nn.sigmoid(jnp.einsum('blkc,kcd->blkd', xb, w_x.astype(f32)).reshape(b, l, GROUP_W) + b_x.astype(f32))
    log_a = -RG_C * rg * jax.nn.softplus(-lam.astype(f32))
    a = jnp.exp(log_a)
    u = jnp.sqrt(-jnp.expm1(2.0 * log_a)) * (ig * xc)

    def combine(e1, e2):
        a1, b1 = e1
        a2, b2 = e2
        return a1 * a2, a2 * b1 + b2

    a_cum, h_cum = lax.associative_scan(combine, (a, u), axis=1)
    h = h_cum + a_cum * h0[:, None, :]
    return h, h[:, -1]


def rglru_mixer(xr, gr, conv_w, conv_b, w_a, b_a, w_x, b_x, lam, h0):
    xc = dwconv(xr, conv_w, conv_b).astype(jnp.float32)
    h0 = h0.astype(jnp.float32)
    hf, sf = rglru_dir(xc, w_a[0], b_a[0], w_x[0], b_x[0], lam[0], h0[:, 0])
    hb, sb = rglru_dir(flip(xc), w_a[1], b_a[1], w_x[1], b_x[1], lam[1], h0[:, 1])
    y = (hf + flip(hb)).astype(xr.dtype) * jax.nn.gelu(gr)
    return y, jnp.stack([sf, sb], axis=1).astype(xr.dtype)


def hier_moe(h, w_group, b_group, w_expert, b_expert, w_gate, w_up, w_down):
    f32 = jnp.float32
    b, l, d = h.shape
    t = h.reshape(b * l, d)
    n_tok = t.shape[0]
    g_logits = (t @ w_group + b_group).astype(f32)
    g_prob = jax.nn.softmax(g_logits, axis=-1)
    g_sel = jnp.argmax(g_logits, axis=-1)
    g_w = jnp.take_along_axis(g_prob, g_sel[:, None], axis=-1)
    e_logits = (t @ w_expert + b_expert).astype(f32).reshape(n_tok, N_EGROUPS, N_EPG)
    sel = jnp.broadcast_to(g_sel[:, None, None], (n_tok, 1, N_EPG))
    e_logits = jnp.take_along_axis(e_logits, sel, axis=1)[:, 0]
    top_l, top_i = lax.top_k(e_logits, TOP_K)
    top_p = jax.nn.softmax(top_l, axis=-1)
    e_idx = g_sel[:, None] * N_EPG + top_i
    gate = jnp.sum(jax.nn.one_hot(e_idx, N_EXPERTS, dtype=f32) * (g_w * top_p)[..., None], axis=1)
    hid = jax.nn.silu(jnp.einsum('td,edf->tef', t, w_gate)) * jnp.einsum('td,edf->tef', t, w_up)
    out = jnp.einsum('tef,efd->td', hid * gate[..., None].astype(hid.dtype), w_down)
    return out.reshape(b, l, d)


def mixers(h, lp, layer, ctx):
    b, l, _ = h.shape
    u = jnp.einsum('bld,de->ble', h, lp['w_in'])
    xs, bm, cm, z, dt_raw, xpool, q, k, v, xr, gr = jnp.split(u, SPLIT_POINTS, axis=-1)
    if ctx is None:
        ctx_k, ctx_v = None, None
        h0_ssd = jnp.zeros((b, 2, SSD_HEADS, SSD_HEADDIM, SSD_STATE), jnp.float32)
        h0_rg = jnp.zeros((b, 2, GROUP_W), jnp.float32)
    else:
        ctx_k, ctx_v, h0_ssd, h0_rg = ctx
    y_ssd, s_ssd = ssd_mixer(xs, bm, cm, z, dt_raw, lp['ssd_conv_w'], lp['ssd_conv_b'], lp['ssd_dt_bias'],
                             lp['ssd_a_log'], lp['ssd_d'], lp['ssd_norm_g'], h0_ssd)
    y_pool = multiscale_pool(xpool, lp['pool_w'], lp['pool_scale'])
    lam_init = 0.8 - 0.6 * math.exp(-0.3 * layer)
    y_att, k_c, v_c = diffattn_mixer(q, k, v, lam_init, lp['da_lam_q1'], lp['da_lam_k1'], lp['da_lam_q2'],
                                     lp['da_lam_k2'], lp['da_norm_g'], ctx_k, ctx_v)
    y_rg, s_rg = rglru_mixer(xr, gr, lp['rg_conv_w'], lp['rg_conv_b'], lp['rg_wa'], lp['rg_ba'],
                             lp['rg_wx'], lp['rg_bx'], lp['rg_lambda'], h0_rg)
    mix = jnp.concatenate([y.astype(h.dtype) for y in (y_ssd, y_pool, y_att, y_rg)], axis=-1)
    out = jnp.einsum('ble,ed->bld', mix, lp['w_out'])
    side = (k_c, v_c, s_ssd, s_rg) if ctx is None else None
    return out, side


def trunk_layer(x, cond, lp, layer, ctx):
    m = jax.nn.silu(cond) @ lp['w_mod'] + lp['b_mod']
    sh1, sc1, g1, sh2, sc2, g2 = jnp.split(m[:, None, :], 6, axis=-1)
    hh = rmsnorm(x, lp['norm1_g']) * (1 + sc1) + sh1
    mix, side = mixers(hh, lp, layer, ctx)
    x = x + g1 * mix
    h2 = rmsnorm(x, lp['norm2_g']) * (1 + sc2) + sh2
    x = x + g2 * hier_moe(h2, lp['moe_w_group'], lp['moe_b_group'], lp['moe_w_expert'], lp['moe_b_expert'],
                          lp['moe_w_gate'], lp['moe_w_up'], lp['moe_w_down'])
    return x, side


def setup_inputs(seed: int = 0) -> dict:
    key = jax.random.key(seed)
    ks = iter(jax.random.split(key, 64))
    f32 = jnp.float32
    nrm = lambda shape, s: jax.random.normal(next(ks), shape, f32) * s
    uni = lambda shape, lo, hi: jax.random.uniform(next(ks), shape, f32, lo, hi)
    D = D_MODEL
    dt0 = jnp.exp(uni((DEPTH, 2, SSD_HEADS), math.log(1e-3), math.log(1e-1)))
    a8 = uni((DEPTH, 2, GROUP_W), 0.9, 0.999) ** (1.0 / RG_C)
    return {
        'x_prompt': nrm((BATCH, SEQ, D), 1.0),
        'x_sample': nrm((DEC_BATCH, DEC_SEQ, D), 1.0),
        'c': nrm((DEC_BATCH, D), 1.0),
        'cache_k': nrm((DEC_BATCH, DEPTH, PAST_LEN, DA_HEADS, 2 * DA_QKDIM), 1.0),
        'cache_v': nrm((DEC_BATCH, DEPTH, PAST_LEN, DA_HEADS, DA_VDIM), 1.0),
        'state_ssd': nrm((DEC_BATCH, DEPTH, 2, SSD_HEADS, SSD_HEADDIM, SSD_STATE), 0.5),
        'state_rglru': nrm((DEC_BATCH, DEPTH, 2, GROUP_W), 0.5),
        'c_ctx': nrm((D,), 1.0),
        'w_mod': nrm((DEPTH, D, 6 * D), 0.5 * D ** -0.5),
        'b_mod': nrm((DEPTH, 6 * D), 0.02),
        'norm1_g': 1.0 + nrm((DEPTH, D), 0.02),
        'norm2_g': 1.0 + nrm((DEPTH, D), 0.02),
        'w_in': nrm((DEPTH, D, IN_TOTAL), D ** -0.5),
        'w_out': nrm((DEPTH, MIX_W, D), MIX_W ** -0.5),
        'ssd_conv_w': nrm((DEPTH, SSD_CONV, SSD_CONV_CH), SSD_CONV ** -0.5),
        'ssd_conv_b': nrm((DEPTH, SSD_CONV_CH), 0.02),
        'ssd_dt_bias': dt0 + jnp.log(-jnp.expm1(-dt0)),
        'ssd_a_log': jnp.log(uni((DEPTH, 2, SSD_HEADS), 1.0, 16.0)),
        'ssd_d': 1.0 + nrm((DEPTH, SSD_HEADS), 0.02),
        'ssd_norm_g': 1.0 + nrm((DEPTH, GROUP_W), 0.02),
        'pool_w': nrm((DEPTH, len(POOL_WINDOWS), POOL_GW, POOL_GW), POOL_GW ** -0.5),
        'pool_scale': 1.0 + nrm((DEPTH, GROUP_W), 0.02),
        'da_lam_q1': nrm((DEPTH, DA_QKDIM), 0.1),
        'da_lam_k1': nrm((DEPTH, DA_QKDIM), 0.1),
        'da_lam_q2': nrm((DEPTH, DA_QKDIM), 0.1),
        'da_lam_k2': nrm((DEPTH, DA_QKDIM), 0.1),
        'da_norm_g': 1.0 + nrm((DEPTH, DA_VDIM), 0.02),
        'rg_conv_w': nrm((DEPTH, RG_CONV, GROUP_W), RG_CONV ** -0.5),
        'rg_conv_b': nrm((DEPTH, GROUP_W), 0.02),
        'rg_wa': nrm((DEPTH, 2, RG_BLOCKS, RG_BW, RG_BW), RG_BW ** -0.5),
        'rg_ba': nrm((DEPTH, 2, GROUP_W), 0.02),
        'rg_wx': nrm((DEPTH, 2, RG_BLOCKS, RG_BW, RG_BW), RG_BW ** -0.5),
        'rg_bx': nrm((DEPTH, 2, GROUP_W), 0.02),
        'rg_lambda': jnp.log(a8) - jnp.log1p(-a8),
        'moe_w_group': nrm((DEPTH, D, N_EGROUPS), D ** -0.5),
        'moe_b_group': nrm((DEPTH, N_EGROUPS), 0.01),
        'moe_w_expert': nrm((DEPTH, D, N_EXPERTS), D ** -0.5),
        'moe_b_expert': nrm((DEPTH, N_EXPERTS), 0.01),
        'moe_w_gate': nrm((DEPTH, N_EXPERTS, D, EXPERT_FF), D ** -0.5),
        'moe_w_up': nrm((DEPTH, N_EXPERTS, D, EXPERT_FF), D ** -0.5),
        'moe_w_down': nrm((DEPTH, N_EXPERTS, EXPERT_FF, D), EXPERT_FF ** -0.5),
        'final_norm_g': 1.0 + nrm((D,), 0.02),
    }


def reference(x_prompt, x_sample, c, cache_k, cache_v, state_ssd, state_rglru, c_ctx,
              w_mod, b_mod, norm1_g, norm2_g, w_in, w_out,
              ssd_conv_w, ssd_conv_b, ssd_dt_bias, ssd_a_log, ssd_d, ssd_norm_g,
              pool_w, pool_scale, da_lam_q1, da_lam_k1, da_lam_q2, da_lam_k2, da_norm_g,
              rg_conv_w, rg_conv_b, rg_wa, rg_ba, rg_wx, rg_bx, rg_lambda,
              moe_w_group, moe_b_group, moe_w_expert, moe_b_expert, moe_w_gate, moe_w_up, moe_w_down,
              final_norm_g):
    cond_ctx = jnp.broadcast_to(c_ctx[None, :], (x_prompt.shape[0], D_MODEL))
    xp, xs = x_prompt, x_sample
    ks_out, vs_out, ssd_out, rg_out = [], [], [], []
    for i in range(DEPTH):
        lp = dict(w_mod=w_mod[i], b_mod=b_mod[i], norm1_g=norm1_g[i], norm2_g=norm2_g[i],
                  w_in=w_in[i], w_out=w_out[i],
                  ssd_conv_w=ssd_conv_w[i], ssd_conv_b=ssd_conv_b[i], ssd_dt_bias=ssd_dt_bias[i],
                  ssd_a_log=ssd_a_log[i], ssd_d=ssd_d[i], ssd_norm_g=ssd_norm_g[i],
                  pool_w=pool_w[i], pool_scale=pool_scale[i],
                  da_lam_q1=da_lam_q1[i], da_lam_k1=da_lam_k1[i], da_lam_q2=da_lam_q2[i],
                  da_lam_k2=da_lam_k2[i], da_norm_g=da_norm_g[i],
                  rg_conv_w=rg_conv_w[i], rg_conv_b=rg_conv_b[i], rg_wa=rg_wa[i], rg_ba=rg_ba[i],
                  rg_wx=rg_wx[i], rg_bx=rg_bx[i], rg_lambda=rg_lambda[i],
                  moe_w_group=moe_w_group[i], moe_b_group=moe_b_group[i],
                  moe_w_expert=moe_w_expert[i], moe_b_expert=moe_b_expert[i],
                  moe_w_gate=moe_w_gate[i], moe_w_up=moe_w_up[i], moe_w_down=moe_w_down[i])
        xp, (k_l, v_l, s_l, r_l) = trunk_layer(xp, cond_ctx, lp, i, None)
        ks_out.append(k_l)
        vs_out.append(v_l)
        ssd_out.append(s_l)
        rg_out.append(r_l)
        xs, _ = trunk_layer(xs, c, lp, i, (cache_k[:, i], cache_v[:, i], state_ssd[:, i], state_rglru[:, i]))
    y_prompt = rmsnorm(xp, final_norm_g)
    y_sample = rmsnorm(xs, final_norm_g)
    new_cache_k = jnp.stack(ks_out, axis=1)
    new_cache_v = jnp.stack(vs_out, axis=1)
    new_state_ssd = jnp.stack(ssd_out, axis=1)
    new_state_rglru = jnp.stack(rg_out, axis=1)
    return (y_prompt, y_sample, new_cache_k, new_cache_v, new_state_ssd, new_state_rglru)
```

```python
import functools
import math

import numpy as np
import jax
import jax.numpy as jnp
from jax import lax
from jax.experimental import pallas as pl
from jax.experimental.pallas import tpu as pltpu

F32 = jnp.float32
BF16 = jnp.bfloat16
HIGHEST = lax.Precision.HIGHEST

D_MODEL = 1024
DEPTH = 4
GRID_W = 64
GROUP_W = 256
EPS = 1e-6
SSD_HEADDIM = 64
SSD_HEADS = 4
SSD_STATE = 64
SSD_BC = 128
SSD_CONV_CH = 512
POOL_WINDOWS = (2, 4, 8, 16)
DA_HEADS = 4
DA_VDIM = 64
DA_QKDIM = 32
ROPE_BASE = 10000.0
RG_BLOCKS = 4
RG_C = 8.0
N_EGROUPS = 4
N_EPG = 4
N_EXPERTS = 16
EXPERT_FF = 256
IN_SIZES = (256, 128, 128, 256, 8, 256, 256, 256, 256, 256, 256)
IN_TOTAL = sum(IN_SIZES)
DT_LO, DT_HI = 768, 776

LANES = 128
SUBLANES = 8
CH = 128
HALO = SUBLANES
MOD_ROWS = 16
VMEM_LIMIT = 56 * 1024 * 1024


def _cparams(n_axes):
    return pltpu.CompilerParams(dimension_semantics=("arbitrary",) * n_axes, vmem_limit_bytes=VMEM_LIMIT)


def _silu(x):
    return x * jax.nn.sigmoid(x)


def _softplus(x):
    return jnp.maximum(x, 0.0) + jnp.log1p(jnp.exp(-jnp.abs(x)))


def _dot(a, b, **kw):
    return jnp.dot(a, b, preferred_element_type=F32, **kw)


def _dot_nt(a, b):
    return lax.dot_general(a, b, (((1,), (1,)), ((), ())), preferred_element_type=F32)


def _window(ref, c, n_steps, seq_len):
    r0 = pl.multiple_of(c * CH, CH)
    main = ref[pl.ds(r0, CH), :]
    lo = pl.multiple_of(jnp.maximum(r0 - HALO, 0), HALO)
    hi = pl.multiple_of(jnp.minimum(r0 + CH, seq_len - HALO), HALO)
    prev = jnp.where(c > 0, ref[pl.ds(lo, HALO), :], 0.0)
    nxt = jnp.where(c < n_steps - 1, ref[pl.ds(hi, HALO), :], 0.0)
    return jnp.concatenate([prev, main, nxt], axis=0)


def _conv4(win, w_ref, b_ref):
    acc = b_ref[...]
    for k in range(4):
        acc = acc + w_ref[k:k + 1, :] * win[HALO - 1 + k:HALO - 1 + k + CH, :]
    return acc


def _mod_kernel(cond_ref, w_ref, b_ref, o_ref):
    cnd = cond_ref[...]
    o_ref[0] = _dot(_silu(cnd), w_ref[0], precision=HIGHEST) + b_ref[0]


def _modulation(cond, w_mod, b_mod):
    nb = 6
    return pl.pallas_call(
        _mod_kernel,
        grid=(DEPTH, nb),
        in_specs=[
            pl.BlockSpec((MOD_ROWS, D_MODEL), lambda l, j: (0, 0)),
            pl.BlockSpec((1, D_MODEL, D_MODEL), lambda l, j: (l, 0, j)),
            pl.BlockSpec((1, 1, D_MODEL), lambda l, j: (l, 0, j)),
        ],
        out_specs=pl.BlockSpec((1, MOD_ROWS, D_MODEL), lambda l, j: (l, 0, j)),
        out_shape=jax.ShapeDtypeStruct((DEPTH, MOD_ROWS, nb * D_MODEL), F32),
        compiler_params=_cparams(2),
        name="modulation",
    )(cond, w_mod, b_mod.reshape(DEPTH, 1, nb * D_MODEL))


IN_COLS = (512, 256, 256, 256, 256, 256, 256, 256, LANES)


def _inproj_kernel(x_ref, m_ref, g_ref, w_ref, *out_refs):
    x = x_ref[...]
    m = m_ref[0]
    var = jnp.mean(x * x, axis=-1, keepdims=True)
    hh = (x * lax.rsqrt(var + EPS)) * g_ref[...] * (1.0 + m[1:2]) + m[0:1]
    u = _dot(hh.astype(BF16), w_ref[...])
    off = 0
    for ref, n in zip(out_refs, IN_COLS):
        ref[...] = u[:, off:off + n]
        off += n


def _inproj(x, mod, mod_row, g, w, tm):
    t = x.shape[0]
    ncol = sum(IN_COLS)
    return pl.pallas_call(
        _inproj_kernel,
        grid=(t // tm,),
        in_specs=[
            pl.BlockSpec((tm, D_MODEL), lambda i: (i, 0)),
            pl.BlockSpec((1, 6, D_MODEL), lambda i: (mod_row(i), 0, 0)),
            pl.BlockSpec((1, D_MODEL), lambda i: (0, 0)),
            pl.BlockSpec((D_MODEL, ncol), lambda i: (0, 0)),
        ],
        out_specs=[pl.BlockSpec((tm, n), lambda i: (i, 0)) for n in IN_COLS],
        out_shape=[jax.ShapeDtypeStruct((t, n), F32) for n in IN_COLS],
        compiler_params=_cparams(1),
        name="inproj",
    )(x, mod, g, w)


def _ssd_kernel(has_ctx, seq_len, xbc_ref, z_ref, dt_ref, cw_ref, cb_ref, dtb_ref, alog_ref, d_ref, ng_ref, *rest):
    if has_ctx:
        h0_ref, y_ref, xc_s, y_s, st_s = rest
        st_ref = None
    else:
        y_ref, st_ref, xc_s, y_s, st_s = rest
    nc = seq_len // CH
    a_row = -jnp.exp(alog_ref[...])
    row = lax.broadcasted_iota(jnp.int32, (CH, CH), 0)
    col = lax.broadcasted_iota(jnp.int32, (CH, CH), 1)

    def conv_step(c, carry):
        r0 = pl.multiple_of(c * CH, CH)
        xc = _silu(_conv4(_window(xbc_ref, c, nc, seq_len), cw_ref, cb_ref))
        xc_s[pl.ds(r0, CH), :] = xc
        y_s[pl.ds(r0, CH), :] = xc[:, :GROUP_W] * d_ref[...]
        return carry

    lax.fori_loop(0, nc, conv_step, 0)

    for d in range(2):
        for h in range(SSD_HEADS):
            st_s[d, h] = h0_ref[0, 0, d, h] if has_ctx else jnp.zeros((SSD_HEADDIM, SSD_STATE), F32)

    def scan_dir(d, ci):
        r0 = pl.multiple_of(ci * CH, CH)
        xc = xc_s[pl.ds(r0, CH), :]
        x = xc[:, :GROUP_W]
        dt = _softplus(dt_ref[pl.ds(r0, CH), :] + dtb_ref[...])
        tri = (row >= col) if d == 0 else (row <= col)
        cs = _dot(tri.astype(F32), dt * a_row, precision=HIGHEST)
        cs_t = cs.T
        tot = cs[CH - 1:CH, :] if d == 0 else cs[0:1, :]
        scores = []
        for g in range(2):
            bm = xc[:, GROUP_W + g * SSD_STATE:GROUP_W + (g + 1) * SSD_STATE]
            cm = xc[:, GROUP_W + SSD_BC + g * SSD_STATE:GROUP_W + SSD_BC + (g + 1) * SSD_STATE]
            scores.append((_dot_nt(cm.astype(BF16), bm.astype(BF16)), bm.astype(BF16), cm.astype(BF16)))
        for h in range(SSD_HEADS):
            k = SSD_HEADS * d + h
            sc, bm, cm = scores[h // 2]
            ccol = cs[:, k:k + 1]
            decay = jnp.exp(jnp.where(tri, ccol - cs_t[k:k + 1, :], -jnp.inf))
            xdt = x[:, h * SSD_HEADDIM:(h + 1) * SSD_HEADDIM] * dt[:, k:k + 1]
            y_diag = _dot((sc * decay).astype(BF16), xdt.astype(BF16))
            s_in = st_s[d, h]
            y_off = _dot_nt(cm, s_in.astype(BF16)) * jnp.exp(ccol)
            tot_k = tot[:, k:k + 1]
            wgt = xdt * jnp.exp(tot_k - ccol)
            st_s[d, h] = s_in * jnp.exp(tot_k) + _dot(wgt.T.astype(BF16), bm)
            y_s[pl.ds(r0, CH), h * SSD_HEADDIM:(h + 1) * SSD_HEADDIM] += y_diag + y_off

    def scan_step(c, carry):
        scan_dir(0, c)
        scan_dir(1, nc - 1 - c)
        return carry

    lax.fori_loop(0, nc, scan_step, 0)

    def out_step(c, carry):
        r0 = pl.multiple_of(c * CH, CH)
        y = y_s[pl.ds(r0, CH), :] * _silu(z_ref[pl.ds(r0, CH), :])
        var = jnp.mean(y * y, axis=-1, keepdims=True)
        y_ref[pl.ds(r0, CH), :] = (y * lax.rsqrt(var + EPS)) * ng_ref[...]
        return carry

    lax.fori_loop(0, nc, out_step, 0)
    if not has_ctx:
        for d in range(2):
            for h in range(SSD_HEADS):
                st_ref[0, d, h] = st_s[d, h]


def _ssd(has_ctx, nb, seq_len, blk0, layer, xbc, z, dt, cw, cb, dtb, alog, dsk, ng, h0=None):
    rows = lambda n: pl.BlockSpec((seq_len, n), lambda b: (blk0 + b, 0))
    full = lambda a: pl.BlockSpec(a.shape, lambda b: (0,) * a.ndim)
    in_specs = [rows(SSD_CONV_CH), rows(GROUP_W), rows(LANES)] + [full(a) for a in (cw, cb, dtb, alog, dsk, ng)]
    args = [xbc, z, dt, cw, cb, dtb, alog, dsk, ng]
    y_spec = pl.BlockSpec((seq_len, GROUP_W), lambda b: (b, 0))
    y_shape = jax.ShapeDtypeStruct((nb * seq_len, GROUP_W), F32)
    st_blk = (1, 2, SSD_HEADS, SSD_HEADDIM, SSD_STATE)
    if has_ctx:
        in_specs.append(pl.BlockSpec((1, 1) + st_blk[1:], lambda b: (b, layer, 0, 0, 0, 0)))
        args.append(h0)
        out_specs, out_shape = y_spec, y_shape
    else:
        out_specs = [y_spec, pl.BlockSpec(st_blk, lambda b: (b, 0, 0, 0, 0))]
        out_shape = [y_shape, jax.ShapeDtypeStruct((nb,) + st_blk[1:], F32)]
    return pl.pallas_call(
        functools.partial(_ssd_kernel, has_ctx, seq_len),
        grid=(nb,),
        in_specs=in_specs,
        out_specs=out_specs,
        out_shape=out_shape,
        scratch_shapes=[
            pltpu.VMEM((seq_len, SSD_CONV_CH), F32),
            pltpu.VMEM((seq_len, GROUP_W), F32),
            pltpu.VMEM((2, SSD_HEADS, SSD_HEADDIM, SSD_STATE), F32),
        ],
        compiler_params=_cparams(1),
        name="ssd_ctx" if has_ctx else "ssd",
    )(*args)


def _pool_kernel(seq_len, x_ref, w_ref, sc_ref, y_ref):
    nc = seq_len // CH
    wn = CH + 2 * HALO
    lane = lax.broadcasted_iota(jnp.int32, (CH, GROUP_W), 1)
    gw = GROUP_W // len(POOL_WINDOWS)
    half = jnp.where(lane < gw, 1, jnp.where(lane < 2 * gw, 2, jnp.where(lane < 3 * gw, 4, 8)))
    w_blk = w_ref[...].astype(BF16)

    def ahead(v, k):
        return pltpu.roll(v, wn - k, axis=0)

    def step(c, carry):
        r0 = pl.multiple_of(c * CH, CH)
        win = _window(x_ref, c, nc, seq_len)
        p2 = win + ahead(win, 1)
        p4 = p2 + ahead(p2, 2)
        p8 = p4 + ahead(p4, 4)
        p16 = p8 + ahead(p8, 8)
        s2 = ahead(p2, HALO - 1)[:CH]
        s4 = ahead(p4, HALO - 2)[:CH]
        s8 = ahead(p8, HALO - 4)[:CH]
        s16 = p16[:CH]
        tot = jnp.where(lane < gw, s2, jnp.where(lane < 2 * gw, s4, jnp.where(lane < 3 * gw, s8, s16)))
        t = r0 + lax.broadcasted_iota(jnp.int32, (CH, GROUP_W), 0)
        cnt = jnp.minimum(t + half, seq_len) - jnp.maximum(t - half, 0)
        x = win[HALO:HALO + CH]
        diff = tot / cnt.astype(F32) - x
        y_ref[pl.ds(r0, CH), :] = _dot(diff.astype(BF16), w_blk) * sc_ref[...]
        return carry

    lax.fori_loop(0, nc, step, 0)


def _pool(nb, seq_len, blk0, x, w_blk, scale):
    return pl.pallas_call(
        functools.partial(_pool_kernel, seq_len),
        grid=(nb,),
        in_specs=[
            pl.BlockSpec((seq_len, GROUP_W), lambda b: (blk0 + b, 0)),
            pl.BlockSpec((GROUP_W, GROUP_W), lambda b: (0, 0)),
            pl.BlockSpec((1, GROUP_W), lambda b: (0, 0)),
        ],
        out_specs=pl.BlockSpec((seq_len, GROUP_W), lambda b: (b, 0)),
        out_shape=jax.ShapeDtypeStruct((nb * seq_len, GROUP_W), F32),
        compiler_params=_cparams(1),
        name="pool",
    )(x, w_blk, scale)


def _rope(x, cos, sin):
    c2 = jnp.concatenate([cos, cos], axis=1)
    s2 = jnp.concatenate([sin, sin], axis=1)
    lane = lax.broadcasted_iota(jnp.int32, x.shape, 1)
    n = x.shape[1]
    partner = jnp.where(lane % 2 == 0, pltpu.roll(x, n - 1, axis=1), pltpu.roll(x, 1, axis=1))
    return x * c2 + partner * s2


def _attn_kernel(has_ctx, seq_len, tq, past, lam_init, q_ref, k_ref, v_ref, lq1, lk1, lq2, lk2, ng_ref, *rest):
    if has_ctx:
        ck_ref, cv_ref, cosq_ref, sinq_ref, cosk_ref, sink_ref, o_ref, kt_s, v_s = rest
    else:
        o_ref, kt_s, v_s = rest
    kb = 256

    @pl.when(pl.program_id(1) == 0)
    def _prepare_keys():
        def put(dst0, kk, vv):
            kt_s[:, dst0:dst0 + kb] = kk.T.astype(BF16)
            for h in range(DA_HEADS):
                v_s[h, dst0:dst0 + kb, :] = vv[:, h * DA_VDIM:(h + 1) * DA_VDIM].astype(BF16)

        if has_ctx:
            put(0, ck_ref[0, 0], cv_ref[0, 0])
        for j in range(seq_len // kb):
            kk = k_ref[j * kb:(j + 1) * kb, :]
            if has_ctx:
                kk = _rope(kk, cosk_ref[j * kb:(j + 1) * kb, :], sink_ref[j * kb:(j + 1) * kb, :])
            put(past + j * kb, kk, v_ref[j * kb:(j + 1) * kb, :])

    q = q_ref[...]
    if has_ctx:
        q = _rope(q, cosq_ref[...], sinq_ref[...])
    q = q * (DA_QKDIM ** -0.5)
    lam = (jnp.exp(jnp.sum(lq1[...] * lk1[...], axis=-1, keepdims=True))
           - jnp.exp(jnp.sum(lq2[...] * lk2[...], axis=-1, keepdims=True)) + lam_init)
    for h in range(DA_HEADS):
        acc = None
        for m in range(2):
            lo = h * 2 * DA_QKDIM + m * DA_QKDIM
            s = _dot(q[:, lo:lo + DA_QKDIM].astype(BF16), kt_s[lo:lo + DA_QKDIM, :])
            e = jnp.exp(s - jnp.max(s, axis=-1, keepdims=True))
            o = _dot(e.astype(BF16), v_s[h]) / jnp.sum(e, axis=-1, keepdims=True)
            acc = o if m == 0 else acc - lam * o
        var = jnp.mean(acc * acc, axis=-1, keepdims=True)
        o_ref[:, h * DA_VDIM:(h + 1) * DA_VDIM] = (acc * lax.rsqrt(var + EPS)) * ng_ref[...] * (1.0 - lam_init)


def _attn(has_ctx, nb, seq_len, blk0, layer, lam_init, q, k, v, lq1, lk1, lq2, lk2, ng,
          ck=None, cv=None, cos=None, sin=None):
    tq = 128
    nq = seq_len // tq
    past = ck.shape[2] if has_ctx else 0
    keys = seq_len + past
    small = lambda a: pl.BlockSpec(a.shape, lambda b, i: (0,) * a.ndim)
    in_specs = [
        pl.BlockSpec((tq, GROUP_W), lambda b, i: ((blk0 + b) * nq + i, 0)),
        pl.BlockSpec((seq_len, GROUP_W), lambda b, i: (blk0 + b, 0)),
        pl.BlockSpec((seq_len, GROUP_W), lambda b, i: (blk0 + b, 0)),
    ] + [small(a) for a in (lq1, lk1, lq2, lk2, ng)]
    args = [q, k, v, lq1, lk1, lq2, lk2, ng]
    if has_ctx:
        in_specs += [
            pl.BlockSpec((1, 1, past, GROUP_W), lambda b, i: (b, layer, 0, 0)),
            pl.BlockSpec((1, 1, past, GROUP_W), lambda b, i: (b, layer, 0, 0)),
            pl.BlockSpec((tq, LANES), lambda b, i: (i, 0)),
            pl.BlockSpec((tq, LANES), lambda b, i: (i, 0)),
            pl.BlockSpec((seq_len, LANES), lambda b, i: (0, 0)),
            pl.BlockSpec((seq_len, LANES), lambda b, i: (0, 0)),
        ]
        args += [ck, cv, cos, sin, cos, sin]
    return pl.pallas_call(
        functools.partial(_attn_kernel, has_ctx, seq_len, tq, past, lam_init),
        grid=(nb, nq),
        in_specs=in_specs,
        out_specs=pl.BlockSpec((tq, GROUP_W), lambda b, i: (b * nq + i, 0)),
        out_shape=jax.ShapeDtypeStruct((nb * seq_len, GROUP_W), F32),
        scratch_shapes=[
            pltpu.VMEM((GROUP_W, keys), BF16),
            pltpu.VMEM((DA_HEADS, keys, DA_VDIM), BF16),
        ],
        compiler_params=_cparams(2),
        name="attn_ctx" if has_ctx else "attn",
    )(*args)


def _rope_tables(seq_len):
    t = np.arange(seq_len)
    rowp = (t // GRID_W).astype(np.float64)
    colp = (t % GRID_W).astype(np.float64)
    n_freq = DA_QKDIM // 4
    inv_freq = ROPE_BASE ** (-np.arange(n_freq, dtype=np.float64) / n_freq)
    ang = np.concatenate([rowp[:, None] * inv_freq, colp[:, None] * inv_freq], axis=-1)
    ang = np.repeat(ang, 2, axis=-1)
    sign = np.where(np.arange(DA_QKDIM) % 2 == 0, -1.0, 1.0)
    cos = np.tile(np.cos(ang), (1, LANES // DA_QKDIM)).astype(np.float32)
    sin = np.tile(np.sin(ang) * sign, (1, LANES // DA_QKDIM)).astype(np.float32)
    return jnp.asarray(cos), jnp.asarray(sin)


def _rglru_kernel(has_ctx, seq_len, x_ref, g_ref, cw_ref, cb_ref, wa_ref, ba_ref, wx_ref, bx_ref, lam_ref, *rest):
    if has_ctx:
        h0_ref, y_ref, a_s, u_s = rest
        st_ref = None
    else:
        y_ref, st_ref, a_s, u_s = rest
    nc = seq_len // CH
    nt = CH // SUBLANES
    sub = lax.broadcasted_iota(jnp.int32, (nt, SUBLANES, GROUP_W), 1)

    def gate_step(c, carry):
        r0 = pl.multiple_of(c * CH, CH)
        xc = _conv4(_window(x_ref, c, nc, seq_len), cw_ref, cb_ref)
        xb = xc.astype(BF16)
        for d in range(2):
            rg = jax.nn.sigmoid(_dot(xb, wa_ref[d].astype(BF16)) + ba_ref[d])
            ig = jax.nn.sigmoid(_dot(xb, wx_ref[d].astype(BF16)) + bx_ref[d])
            log_a = -RG_C * rg * _softplus(-lam_ref[d])
            a = jnp.exp(log_a)
            u = jnp.sqrt(-jnp.tanh(log_a) * (a * a + 1.0)) * (ig * xc)
            a3 = a.reshape(nt, SUBLANES, GROUP_W)
            u3 = u.reshape(nt, SUBLANES, GROUP_W)
            for k in (1, 2, 4):
                if d == 0:
                    ok = sub >= k
                    a_sh, u_sh = pltpu.roll(a3, k, axis=1), pltpu.roll(u3, k, axis=1)
                else:
                    ok = sub < SUBLANES - k
                    a_sh, u_sh = pltpu.roll(a3, SUBLANES - k, axis=1), pltpu.roll(u3, SUBLANES - k, axis=1)
                u3 = u3 + a3 * jnp.where(ok, u_sh, 0.0)
                a3 = a3 * jnp.where(ok, a_sh, 1.0)
            a_s[d, pl.ds(r0, CH), :] = a3.reshape(CH, GROUP_W)
            u_s[d, pl.ds(r0, CH), :] = u3.reshape(CH, GROUP_W)
        return carry

    lax.fori_loop(0, nc, gate_step, 0)

    n_tiles = seq_len // SUBLANES
    if has_ctx:
        hf0, hb0 = h0_ref[0, 0, 0:1, :], h0_ref[0, 0, 1:2, :]
    else:
        hf0 = hb0 = jnp.zeros((1, GROUP_W), F32)

    def carry_step(i, carry):
        hf, hb = carry
        rf = pl.multiple_of(i * SUBLANES, SUBLANES)
        rb = pl.multiple_of((n_tiles - 1 - i) * SUBLANES, SUBLANES)
        tf = u_s[0, pl.ds(rf, SUBLANES), :] + a_s[0, pl.ds(rf, SUBLANES), :] * hf
        tb = u_s[1, pl.ds(rb, SUBLANES), :] + a_s[1, pl.ds(rb, SUBLANES), :] * hb
        u_s[0, pl.ds(rf, SUBLANES), :] = tf
        u_s[1, pl.ds(rb, SUBLANES), :] = tb
        return tf[SUBLANES - 1:SUBLANES, :], tb[0:1, :]

    hf, hb = lax.fori_loop(0, n_tiles, carry_step, (hf0, hb0), unroll=4)
    if not has_ctx:
        st_ref[0, 0:1, :] = hf
        st_ref[0, 1:2, :] = hb

    def out_step(c, carry):
        r0 = pl.multiple_of(c * CH, CH)
        g = g_ref[pl.ds(r0, CH), :]
        gelu = g * (0.5 * (1.0 + jnp.tanh(math.sqrt(2.0 / math.pi) * (g + 0.044715 * (g * g * g)))))
        y_ref[pl.ds(r0, CH), :] = (u_s[0, pl.ds(r0, CH), :] + u_s[1, pl.ds(r0, CH), :]) * gelu
        return carry

    lax.fori_loop(0, nc, out_step, 0)


def _rglru(has_ctx, nb, seq_len, blk0, layer, x, g, cw, cb, wa, ba, wx, bx, lam, h0=None):
    rows = pl.BlockSpec((seq_len, GROUP_W), lambda b: (blk0 + b, 0))
    full = lambda a: pl.BlockSpec(a.shape, lambda b: (0,) * a.ndim)
    in_specs = [rows, rows] + [full(a) for a in (cw, cb, wa, ba, wx, bx, lam)]
    args = [x, g, cw, cb, wa, ba, wx, bx, lam]
    y_spec = pl.BlockSpec((seq_len, GROUP_W), lambda b: (b, 0))
    y_shape = jax.ShapeDtypeStruct((nb * seq_len, GROUP_W), F32)
    if has_ctx:
        in_specs.append(pl.BlockSpec((1, 1, 2, GROUP_W), lambda b: (b, layer, 0, 0)))
        args.append(h0)
        out_specs, out_shape = y_spec, y_shape
    else:
        out_specs = [y_spec, pl.BlockSpec((1, 2, GROUP_W), lambda b: (b, 0, 0))]
        out_shape = [y_shape, jax.ShapeDtypeStruct((nb, 2, GROUP_W), F32)]
    return pl.pallas_call(
        functools.partial(_rglru_kernel, has_ctx, seq_len),
        grid=(nb,),
        in_specs=in_specs,
        out_specs=out_specs,
        out_shape=out_shape,
        scratch_shapes=[pltpu.VMEM((2, seq_len, GROUP_W), F32), pltpu.VMEM((2, seq_len, GROUP_W), F32)],
        compiler_params=_cparams(1),
        name="rglru_ctx" if has_ctx else "rglru",
    )(*args)


ROUTE_OFF = N_EGROUPS


def _routing_gate(logits):
    lane = lax.broadcasted_iota(jnp.int32, logits.shape, 1)
    neg = -jnp.inf
    big = LANES
    gl = jnp.where(lane < N_EGROUPS, logits, neg)
    gmax = jnp.max(gl, axis=-1, keepdims=True)
    g_w = 1.0 / jnp.sum(jnp.exp(gl - gmax), axis=-1, keepdims=True)
    g_sel = jnp.min(jnp.where(gl == gmax, lane, big), axis=-1, keepdims=True)
    e_lane = lane - ROUTE_OFF
    in_grp = (e_lane >= 0) & (e_lane < N_EXPERTS) & ((e_lane // N_EPG) == g_sel)
    el = jnp.where(in_grp, logits, neg)
    m1 = jnp.max(el, axis=-1, keepdims=True)
    i1 = jnp.min(jnp.where(el == m1, lane, big), axis=-1, keepdims=True)
    el2 = jnp.where(lane == i1, neg, el)
    m2 = jnp.max(el2, axis=-1, keepdims=True)
    i2 = jnp.min(jnp.where(el2 == m2, lane, big), axis=-1, keepdims=True)
    r = jnp.exp(m2 - m1)
    p1 = 1.0 / (1.0 + r)
    p2 = r / (1.0 + r)
    return jnp.where(lane == i1, g_w * p1, jnp.where(lane == i2, g_w * p2, 0.0))


def _moe_kernel(x_ref, m_ref, mix0, mix1, mix2, mix3, wout_ref, g2_ref, wr_ref, br_ref, wg_ref, wu_ref, wd_ref,
                o_ref, x1_s, h2_s, gate_s, acc_s):
    e = pl.program_id(1)

    @pl.when(e == 0)
    def _start():
        m = m_ref[0]
        mix = jnp.concatenate([mix0[...], mix1[...], mix2[...], mix3[...]], axis=1).astype(BF16)
        x1 = x_ref[...] + m[2:3] * _dot(mix, wout_ref[...])
        x1_s[...] = x1
        var = jnp.mean(x1 * x1, axis=-1, keepdims=True)
        h2 = (x1 * lax.rsqrt(var + EPS)) * g2_ref[...] * (1.0 + m[4:5]) + m[3:4]
        h2_s[...] = h2.astype(BF16)
        gate_s[...] = _routing_gate(_dot(h2, wr_ref[...], precision=HIGHEST) + br_ref[...])
        acc_s[...] = jnp.zeros_like(acc_s)

    h2 = h2_s[...]
    hid = _silu(_dot(h2, wg_ref[0].astype(BF16))) * _dot(h2, wu_ref[0].astype(BF16))
    lane = lax.broadcasted_iota(jnp.int32, gate_s.shape, 1)
    gcol = jnp.sum(jnp.where(lane == e + ROUTE_OFF, gate_s[...], 0.0), axis=-1, keepdims=True)
    acc_s[...] += _dot((hid * gcol).astype(BF16), wd_ref[0].astype(BF16))

    @pl.when(e == N_EXPERTS - 1)
    def _finish():
        o_ref[...] = x1_s[...] + m_ref[0][5:6] * acc_s[...]


def _outproj_moe(x, mod, mod_row, mixes, wout, g2, wr, br, wg, wu, wd, tm):
    t = x.shape[0]
    tok = lambda n: pl.BlockSpec((tm, n), lambda i, e: (i, 0))
    const = lambda a: pl.BlockSpec(a.shape, lambda i, e: (0,) * a.ndim)
    return pl.pallas_call(
        _moe_kernel,
        grid=(t // tm, N_EXPERTS),
        in_specs=[
            tok(D_MODEL),
            pl.BlockSpec((1, 6, D_MODEL), lambda i, e: (mod_row(i), 0, 0)),
            tok(GROUP_W), tok(GROUP_W), tok(GROUP_W), tok(GROUP_W),
            const(wout), const(g2), const(wr), const(br),
            pl.BlockSpec((1, D_MODEL, EXPERT_FF), lambda i, e: (e, 0, 0)),
            pl.BlockSpec((1, D_MODEL, EXPERT_FF), lambda i, e: (e, 0, 0)),
            pl.BlockSpec((1, EXPERT_FF, D_MODEL), lambda i, e: (e, 0, 0)),
        ],
        out_specs=tok(D_MODEL),
        out_shape=jax.ShapeDtypeStruct((t, D_MODEL), F32),
        scratch_shapes=[
            pltpu.VMEM((tm, D_MODEL), F32),
            pltpu.VMEM((tm, D_MODEL), BF16),
            pltpu.VMEM((tm, LANES), F32),
            pltpu.VMEM((tm, D_MODEL), F32),
        ],
        compiler_params=_cparams(2),
        name="outproj_moe",
    )(x, mod, *mixes, wout, g2, wr, br, wg, wu, wd)


def _final_norm_kernel(x_ref, g_ref, o_ref):
    x = x_ref[...]
    var = jnp.mean(x * x, axis=-1, keepdims=True)
    o_ref[...] = (x * lax.rsqrt(var + EPS)) * g_ref[...]


def _final_norm(x, g, row0, n_rows, tm):
    blk0 = row0 // tm
    return pl.pallas_call(
        _final_norm_kernel,
        grid=(n_rows // tm,),
        in_specs=[pl.BlockSpec((tm, D_MODEL), lambda i: (blk0 + i, 0)), pl.BlockSpec((1, D_MODEL), lambda i: (0, 0))],
        out_specs=pl.BlockSpec((tm, D_MODEL), lambda i: (i, 0)),
        out_shape=jax.ShapeDtypeStruct((n_rows, D_MODEL), F32),
        compiler_params=_cparams(1),
        name="final_norm",
    )(x, g)


def _block_diag(w):
    n, k, _ = w.shape
    eye = jnp.eye(n, dtype=w.dtype)
    return (eye[:, None, :, None] * w[:, :, None, :]).reshape(n * k, n * k)


def _pad_lanes(v, n=LANES):
    return jnp.pad(v, ((0, 0), (0, n - v.shape[-1])))


def kernel(x_prompt, x_sample, c, cache_k, cache_v, state_ssd, state_rglru, c_ctx, w_mod, b_mod, norm1_g, norm2_g, w_in, w_out, ssd_conv_w, ssd_conv_b, ssd_dt_bias, ssd_a_log, ssd_d, ssd_norm_g, pool_w, pool_scale, da_lam_q1, da_lam_k1, da_lam_q2, da_lam_k2, da_norm_g, rg_conv_w, rg_conv_b, rg_wa, rg_ba, rg_wx, rg_bx, rg_lambda, moe_w_group, moe_b_group, moe_w_expert, moe_b_expert, moe_w_gate, moe_w_up, moe_w_down, final_norm_g):
    nbp, lp, _ = x_prompt.shape
    nbs, ls, _ = x_sample.shape
    past = cache_k.shape[2]
    tp, ts = nbp * lp, nbs * ls
    assert nbs + 1 <= MOD_ROWS and lp % CH == 0 and ls % CH == 0 and ls % lp == 0

    x = jnp.concatenate([x_prompt.reshape(tp, D_MODEL), x_sample.reshape(ts, D_MODEL)], axis=0)
    cond = jnp.concatenate([c_ctx[None, :], c, jnp.zeros((MOD_ROWS - 1 - nbs, D_MODEL), F32)], axis=0)
    mod = _modulation(cond, w_mod, b_mod).reshape(DEPTH * MOD_ROWS, 6, D_MODEL)

    tm_in = 256
    tm_moe = 512
    assert lp % tm_in == 0 and ls % tm_moe == 0 and tp % tm_moe == 0 and (lp % tm_moe == 0 or tm_moe % lp == 0)

    def mod_row_fn(layer, tm):
        n_prompt_tiles = tp // tm
        per_sample = ls // tm
        return lambda i: layer * MOD_ROWS + jnp.where(i < n_prompt_tiles, 0, 1 + (i - n_prompt_tiles) // per_sample)

    w_in_r = jnp.concatenate([w_in[:, :, :DT_LO], w_in[:, :, DT_HI:], w_in[:, :, DT_LO:DT_HI],
                              jnp.zeros((DEPTH, D_MODEL, LANES - (DT_HI - DT_LO)), F32)], axis=-1).astype(BF16)
    w_out_b = w_out.astype(BF16)
    w_route = jnp.concatenate([moe_w_group, moe_w_expert,
                               jnp.zeros((DEPTH, D_MODEL, LANES - N_EGROUPS - N_EXPERTS), F32)], axis=-1)
    b_route = _pad_lanes(jnp.concatenate([moe_b_group, moe_b_expert], axis=-1))
    dtb = _pad_lanes(ssd_dt_bias.reshape(DEPTH, 2 * SSD_HEADS))
    alog = _pad_lanes(ssd_a_log.reshape(DEPTH, 2 * SSD_HEADS))
    d_skip = jnp.repeat(ssd_d, SSD_HEADDIM, axis=-1)
    cos, sin = _rope_tables(ls)
    ck = cache_k.reshape(nbs, DEPTH, past, GROUP_W)
    cv = cache_v.reshape(nbs, DEPTH, past, GROUP_W)
    sblk0 = tp // ls

    ks_out, vs_out, ssd_out, rg_out = [], [], [], []
    for l in range(DEPTH):
        row1 = lambda a: a[l][None, :]
        xbc, z, xpool, q, k, v, xr, gr, dt = _inproj(x, mod, mod_row_fn(l, tm_in), row1(norm1_g), w_in_r[l], tm_in)
        lam_init = 0.8 - 0.6 * math.exp(-0.3 * l)
        ssd_w = (ssd_conv_w[l], row1(ssd_conv_b), row1(dtb), row1(alog), row1(d_skip), row1(ssd_norm_g))
        pool_wb = _block_diag(pool_w[l])
        att_w = (row1(da_lam_q1), row1(da_lam_k1), row1(da_lam_q2), row1(da_lam_k2), row1(da_norm_g))
        rg_w = (rg_conv_w[l], row1(rg_conv_b),
                jnp.stack([_block_diag(rg_wa[l, 0]), _block_diag(rg_wa[l, 1])]), rg_ba[l][:, None, :],
                jnp.stack([_block_diag(rg_wx[l, 0]), _block_diag(rg_wx[l, 1])]), rg_bx[l][:, None, :],
                rg_lambda[l][:, None, :])

        ya_p, st_ssd = _ssd(False, nbp, lp, 0, l, xbc, z, dt, *ssd_w)
        yb_p = _pool(nbp, lp, 0, xpool, pool_wb, row1(pool_scale))
        yc_p = _attn(False, nbp, lp, 0, l, lam_init, q, k, v, *att_w)
        yd_p, st_rg = _rglru(False, nbp, lp, 0, l, xr, gr, *rg_w)
        ya_s = _ssd(True, nbs, ls, sblk0, l, xbc, z, dt, *ssd_w, h0=state_ssd)
        yb_s = _pool(nbs, ls, sblk0, xpool, pool_wb, row1(pool_scale))
        yc_s = _attn(True, nbs, ls, sblk0, l, lam_init, q, k, v, *att_w, ck=ck, cv=cv, cos=cos, sin=sin)
        yd_s = _rglru(True, nbs, ls, sblk0, l, xr, gr, *rg_w, h0=state_rglru)

        mixes = [jnp.concatenate([a, b], axis=0) for a, b in ((ya_p, ya_s), (yb_p, yb_s), (yc_p, yc_s), (yd_p, yd_s))]
        x = _outproj_moe(x, mod, mod_row_fn(l, tm_moe), mixes, w_out_b[l], row1(norm2_g), w_route[l], row1(b_route),
                         moe_w_gate[l], moe_w_up[l], moe_w_down[l], tm_moe)

        ks_out.append(k[:tp].reshape(nbp, lp, DA_HEADS, 2 * DA_QKDIM))
        vs_out.append(v[:tp].reshape(nbp, lp, DA_HEADS, DA_VDIM))
        ssd_out.append(st_ssd)
        rg_out.append(st_rg)

    g_fin = final_norm_g[None, :]
    y_prompt = _final_norm(x, g_fin, 0, tp, 512).reshape(nbp, lp, D_MODEL)
    y_sample = _final_norm(x, g_fin, tp, ts, 512).reshape(nbs, ls, D_MODEL)
    return (y_prompt, y_sample, jnp.stack(ks_out, axis=1), jnp.stack(vs_out, axis=1),
            jnp.stack(ssd_out, axis=1), jnp.stack(rg_out, axis=1))
```

```python
import functools
import math

import numpy as np
import jax
import jax.numpy as jnp
from jax import lax
from jax.experimental import pallas as pl
from jax.experimental.pallas import tpu as pltpu

F32 = jnp.float32
BF16 = jnp.bfloat16
HIGHEST = lax.Precision.HIGHEST

D_MODEL = 1024
DEPTH = 4
GRID_W = 64
GROUP_W = 256
EPS = 1e-6
SSD_HEADDIM = 64
SSD_HEADS = 4
SSD_STATE = 64
SSD_BC = 128
SSD_CONV_CH = 512
POOL_WINDOWS = (2, 4, 8, 16)
DA_HEADS = 4
DA_VDIM = 64
DA_QKDIM = 32
ROPE_BASE = 10000.0
RG_C = 8.0
N_EGROUPS = 4
N_EPG = 4
N_EXPERTS = 16
EXPERT_FF = 256
DT_LO, DT_HI = 768, 776

LANES = 128
SUBLANES = 8
CH = 128
HALO = SUBLANES
MOD_ROWS = 16
VMEM_LIMIT = 56 * 1024 * 1024


def _cparams(n_axes):
    return pltpu.CompilerParams(dimension_semantics=("arbitrary",) * n_axes, vmem_limit_bytes=VMEM_LIMIT)


def _silu(x):
    return x * jax.nn.sigmoid(x)


def _softplus(x):
    return jnp.maximum(x, 0.0) + jnp.log1p(jnp.exp(-jnp.abs(x)))


def _dot(a, b, **kw):
    return jnp.dot(a, b, preferred_element_type=F32, **kw)


def _dot_nt(a, b):
    return lax.dot_general(a, b, (((1,), (1,)), ((), ())), preferred_element_type=F32)


def _rms(x):
    return x * lax.rsqrt(jnp.mean(x * x, axis=-1, keepdims=True) + EPS)


def _window(ref, c, n_steps, seq_len):
    r0 = pl.multiple_of(c * CH, CH)
    main = ref[pl.ds(r0, CH), :]
    lo = pl.multiple_of(jnp.maximum(r0 - HALO, 0), HALO)
    hi = pl.multiple_of(jnp.minimum(r0 + CH, seq_len - HALO), HALO)
    prev = jnp.where(c > 0, ref[pl.ds(lo, HALO), :], 0.0)
    nxt = jnp.where(c < n_steps - 1, ref[pl.ds(hi, HALO), :], 0.0)
    return jnp.concatenate([prev, main, nxt], axis=0)


def _conv4(win, w_ref, b_ref):
    acc = b_ref[...]
    for k in range(4):
        acc = acc + w_ref[k:k + 1, :] * win[HALO - 1 + k:HALO - 1 + k + CH, :]
    return acc


def _mod_kernel(cond_ref, w_ref, b_ref, o_ref):
    cnd = cond_ref[...]
    o_ref[0] = _dot(_silu(cnd), w_ref[0], precision=HIGHEST) + b_ref[0]


def _modulation(cond, w_mod, b_mod):
    nb = 6
    return pl.pallas_call(
        _mod_kernel,
        grid=(DEPTH, nb),
        in_specs=[
            pl.BlockSpec((MOD_ROWS, D_MODEL), lambda l, j: (0, 0)),
            pl.BlockSpec((1, D_MODEL, D_MODEL), lambda l, j: (l, 0, j)),
            pl.BlockSpec((1, 1, D_MODEL), lambda l, j: (l, 0, j)),
        ],
        out_specs=pl.BlockSpec((1, MOD_ROWS, D_MODEL), lambda l, j: (l, 0, j)),
        out_shape=jax.ShapeDtypeStruct((DEPTH, MOD_ROWS, nb * D_MODEL), F32),
        compiler_params=_cparams(2),
        name="modulation",
    )(cond, w_mod, b_mod.reshape(DEPTH, 1, nb * D_MODEL))


IN_COLS = (512, 256, 256, 256, 256, 256, 256, 256, LANES)
K_OUT, V_OUT = 4, 5


def _inproj_kernel(to_cache, x_ref, m_ref, g_ref, w_ref, *refs):
    if to_cache:
        refs = refs[2:]
    hh = _rms(x_ref[...]) * g_ref[...] * (1.0 + m_ref[0, 1:2]) + m_ref[0, 0:1]
    u = _dot(hh.astype(BF16), w_ref[...])
    off = 0
    for j, (ref, n) in enumerate(zip(refs, IN_COLS)):
        if to_cache and j in (K_OUT, V_OUT):
            ref[0, 0] = u[:, off:off + n]
        else:
            ref[...] = u[:, off:off + n]
        off += n


def _inproj(x, mod, mod_row, g, w, tm, layer, caches=None):
    t = x.shape[0]
    ncol = sum(IN_COLS)
    to_cache = caches is not None
    in_specs = [
        pl.BlockSpec((tm, D_MODEL), lambda i: (i, 0)),
        pl.BlockSpec((1, 6, D_MODEL), lambda i: (mod_row(i), 0, 0)),
        pl.BlockSpec((1, D_MODEL), lambda i: (0, 0)),
        pl.BlockSpec((D_MODEL, ncol), lambda i: (0, 0)),
    ]
    out_specs = [pl.BlockSpec((tm, n), lambda i: (i, 0)) for n in IN_COLS]
    out_shape = [jax.ShapeDtypeStruct((t, n), F32) for n in IN_COLS]
    args = [x, mod, g, w]
    aliases = {}
    if to_cache:
        assert caches[0].shape[2] == tm
        cache_spec = pl.BlockSpec((1, 1, tm, GROUP_W), lambda i: (i, layer, 0, 0))
        for j, cch in zip((K_OUT, V_OUT), caches):
            in_specs.append(pl.BlockSpec(memory_space=pl.ANY))
            aliases[len(args)] = j
            args.append(cch)
            out_specs[j] = cache_spec
            out_shape[j] = jax.ShapeDtypeStruct(cch.shape, F32)
    return pl.pallas_call(
        functools.partial(_inproj_kernel, to_cache),
        grid=(t // tm,),
        in_specs=in_specs,
        out_specs=out_specs,
        out_shape=out_shape,
        input_output_aliases=aliases,
        compiler_params=_cparams(1),
        name="inproj_ctx" if to_cache else "inproj",
    )(*args)


def _ssd_kernel(has_ctx, seq_len, xbc_ref, z_ref, dt_ref, cw_ref, cb_ref, dtb_ref, alog_ref, d_ref, ng_ref, *rest):
    if has_ctx:
        h0_ref, y_ref, xc_s, y_s, st_s = rest
        st_ref = None
    else:
        y_ref, st_ref, xc_s, y_s, st_s = rest
    nc = seq_len // CH
    hd = SSD_HEADDIM
    a_row = -jnp.exp(alog_ref[...])
    row = lax.broadcasted_iota(jnp.int32, (CH, CH), 0)
    col = lax.broadcasted_iota(jnp.int32, (CH, CH), 1)
    lane_w = lax.broadcasted_iota(jnp.int32, (CH, GROUP_W), 1)
    lane_n = lax.broadcasted_iota(jnp.int32, (CH, SSD_BC), 1)
    own_block = (lax.broadcasted_iota(jnp.int32, (SSD_BC, GROUP_W), 0) // SSD_STATE
                 == lax.broadcasted_iota(jnp.int32, (SSD_BC, GROUP_W), 1) // (2 * hd))

    def conv_step(c, carry):
        r0 = pl.multiple_of(c * CH, CH)
        xc = _silu(_conv4(_window(xbc_ref, c, nc, seq_len), cw_ref, cb_ref))
        xc_s[pl.ds(r0, CH), :] = xc
        y_s[pl.ds(r0, CH), :] = xc[:, :GROUP_W] * d_ref[...]
        return carry

    lax.fori_loop(0, nc, conv_step, 0)

    st_s[...] = jnp.zeros_like(st_s)
    if has_ctx:
        for d in range(2):
            for h in range(SSD_HEADS):
                g = h // 2
                st_s[d, g * SSD_STATE:(g + 1) * SSD_STATE, h * hd:(h + 1) * hd] = h0_ref[0, 0, d, h].T

    def per_head(v, d):
        lanes = lane_w[:v.shape[0]]
        out = jnp.broadcast_to(v[:, 4 * d + 3:4 * d + 4], (v.shape[0], GROUP_W))
        for h in (2, 1, 0):
            out = jnp.where(lanes < (h + 1) * hd, jnp.broadcast_to(v[:, 4 * d + h:4 * d + h + 1], out.shape), out)
        return out

    def scan_dir(d, ci):
        r0 = pl.multiple_of(ci * CH, CH)
        xc = xc_s[pl.ds(r0, CH), :]
        x = xc[:, :GROUP_W]
        bm = xc[:, GROUP_W:GROUP_W + SSD_BC]
        cm = xc[:, GROUP_W + SSD_BC:]
        dt = _softplus(dt_ref[pl.ds(r0, CH), :] + dtb_ref[...])
        tri = (row >= col) if d == 0 else (row <= col)
        cs = _dot(tri.astype(F32), dt * a_row, precision=HIGHEST)
        cs_t = cs.T
        tot = cs[CH - 1:CH, :] if d == 0 else cs[0:1, :]
        dt_e, cs_e, tot_e = per_head(dt, d), per_head(cs, d), per_head(tot, d)
        xdt = x * dt_e
        bmb, cmb = bm.astype(BF16), cm.astype(BF16)
        scores = [_dot_nt(jnp.where((lane_n // SSD_STATE) == g, cm, 0.0).astype(BF16), bmb) for g in range(2)]
        m_parts, r_parts = [], []
        for h in range(SSD_HEADS):
            k = SSD_HEADS * d + h
            decay = jnp.exp(jnp.where(tri, cs[:, k:k + 1] - cs_t[k:k + 1, :], -jnp.inf))
            m_parts.append((scores[h // 2] * decay).astype(BF16))
            r_parts.append(jnp.where((lane_w // hd) == h, xdt, 0.0).astype(BF16))
        y_diag = _dot(jnp.concatenate(m_parts, axis=1), jnp.concatenate(r_parts, axis=0))
        st = st_s[d]
        y_off = _dot(cmb, st.astype(BF16)) * jnp.exp(cs_e)
        wgt = xdt * jnp.exp(tot_e - cs_e)
        upd = _dot(bm.T.astype(BF16), wgt.astype(BF16))
        st_s[d] = st * jnp.exp(tot_e) + jnp.where(own_block, upd, 0.0)
        y_s[pl.ds(r0, CH), :] += y_diag + y_off

    def scan_step(c, carry):
        scan_dir(0, c)
        scan_dir(1, nc - 1 - c)
        return carry

    lax.fori_loop(0, nc, scan_step, 0, unroll=2)

    def out_step(c, carry):
        r0 = pl.multiple_of(c * CH, CH)
        y = y_s[pl.ds(r0, CH), :] * _silu(z_ref[pl.ds(r0, CH), :])
        y_ref[pl.ds(r0, CH), :] = _rms(y) * ng_ref[...]
        return carry

    lax.fori_loop(0, nc, out_step, 0)
    if not has_ctx:
        for d in range(2):
            for h in range(SSD_HEADS):
                g = h // 2
                st_ref[0, d, h] = st_s[d, g * SSD_STATE:(g + 1) * SSD_STATE, h * hd:(h + 1) * hd].T


def _ssd(has_ctx, nb, seq_len, layer, xbc, z, dt, cw, cb, dtb, alog, dsk, ng, h0=None):
    rows = lambda n: pl.BlockSpec((seq_len, n), lambda b: (b, 0))
    full = lambda a: pl.BlockSpec(a.shape, lambda b: (0,) * a.ndim)
    in_specs = [rows(SSD_CONV_CH), rows(GROUP_W), rows(LANES)] + [full(a) for a in (cw, cb, dtb, alog, dsk, ng)]
    args = [xbc, z, dt, cw, cb, dtb, alog, dsk, ng]
    y_spec = rows(GROUP_W)
    y_shape = jax.ShapeDtypeStruct((nb * seq_len, GROUP_W), F32)
    st_blk = (1, 2, SSD_HEADS, SSD_HEADDIM, SSD_STATE)
    if has_ctx:
        in_specs.append(pl.BlockSpec((1, 1) + st_blk[1:], lambda b: (b, layer, 0, 0, 0, 0)))
        args.append(h0)
        out_specs, out_shape = y_spec, y_shape
    else:
        out_specs = [y_spec, pl.BlockSpec(st_blk, lambda b: (b, 0, 0, 0, 0))]
        out_shape = [y_shape, jax.ShapeDtypeStruct((nb,) + st_blk[1:], F32)]
    return pl.pallas_call(
        functools.partial(_ssd_kernel, has_ctx, seq_len),
        grid=(nb,),
        in_specs=in_specs,
        out_specs=out_specs,
        out_shape=out_shape,
        scratch_shapes=[
            pltpu.VMEM((seq_len, SSD_CONV_CH), F32),
            pltpu.VMEM((seq_len, GROUP_W), F32),
            pltpu.VMEM((2, SSD_BC, GROUP_W), F32),
        ],
        compiler_params=_cparams(1),
        name="ssd_ctx" if has_ctx else "ssd",
    )(*args)


def _pool_kernel(seq_len, x_ref, w_ref, sc_ref, y_ref):
    nc = seq_len // CH
    wn = CH + 2 * HALO
    lane = lax.broadcasted_iota(jnp.int32, (CH, GROUP_W), 1)
    gw = GROUP_W // len(POOL_WINDOWS)
    half = jnp.where(lane < gw, 1, jnp.where(lane < 2 * gw, 2, jnp.where(lane < 3 * gw, 4, 8)))
    w_blk = w_ref[...].astype(BF16)

    def ahead(v, k):
        return pltpu.roll(v, wn - k, axis=0)

    def step(c, carry):
        r0 = pl.multiple_of(c * CH, CH)
        win = _window(x_ref, c, nc, seq_len)
        p2 = win + ahead(win, 1)
        p4 = p2 + ahead(p2, 2)
        p8 = p4 + ahead(p4, 4)
        p16 = p8 + ahead(p8, 8)
        s2 = ahead(p2, HALO - 1)[:CH]
        s4 = ahead(p4, HALO - 2)[:CH]
        s8 = ahead(p8, HALO - 4)[:CH]
        s16 = p16[:CH]
        tot = jnp.where(lane < gw, s2, jnp.where(lane < 2 * gw, s4, jnp.where(lane < 3 * gw, s8, s16)))
        t = r0 + lax.broadcasted_iota(jnp.int32, (CH, GROUP_W), 0)
        cnt = jnp.minimum(t + half, seq_len) - jnp.maximum(t - half, 0)
        x = win[HALO:HALO + CH]
        diff = tot / cnt.astype(F32) - x
        y_ref[pl.ds(r0, CH), :] = _dot(diff.astype(BF16), w_blk) * sc_ref[...]
        return carry

    lax.fori_loop(0, nc, step, 0)


def _pool(nb, seq_len, x, w_blk, scale):
    return pl.pallas_call(
        functools.partial(_pool_kernel, seq_len),
        grid=(nb,),
        in_specs=[
            pl.BlockSpec((seq_len, GROUP_W), lambda b: (b, 0)),
            pl.BlockSpec((GROUP_W, GROUP_W), lambda b: (0, 0)),
            pl.BlockSpec((1, GROUP_W), lambda b: (0, 0)),
        ],
        out_specs=pl.BlockSpec((seq_len, GROUP_W), lambda b: (b, 0)),
        out_shape=jax.ShapeDtypeStruct((nb * seq_len, GROUP_W), F32),
        compiler_params=_cparams(1),
        name="pool",
    )(x, w_blk, scale)


KEY_BLK = 256


def _rope(x, cos, sin):
    c2 = jnp.concatenate([cos, cos], axis=1)
    s2 = jnp.concatenate([sin, sin], axis=1)
    lane = lax.broadcasted_iota(jnp.int32, x.shape, 1)
    n = x.shape[1]
    partner = jnp.where(lane % 2 == 0, pltpu.roll(x, n - 1, axis=1), pltpu.roll(x, 1, axis=1))
    return x * c2 + partner * s2


def _attn_kernel(has_ctx, seq_len, tq, past, lam_init, q_ref, k_ref, v_ref, lq1, lk1, lq2, lk2, ng_ref, *rest):
    if has_ctx:
        ck_ref, cv_ref, cosq_ref, sinq_ref, cosk_ref, sink_ref, o_ref, kt_s, v_s, q_s, m_s, acc_s = rest
    else:
        o_ref, kt_s, v_s, q_s, m_s, acc_s = rest
    kb = KEY_BLK
    n_kb = (seq_len + past) // kb
    n_maps = 2 * DA_HEADS

    @pl.when(pl.program_id(1) == 0)
    def _prepare_keys():
        ones = jnp.ones((kb, LANES - DA_VDIM), BF16)

        def put(j, kk, vv):
            kt_s[j] = kk.T.astype(BF16)
            for h in range(DA_HEADS):
                v_s[h, j] = jnp.concatenate([vv[:, h * DA_VDIM:(h + 1) * DA_VDIM].astype(BF16), ones], axis=1)

        if has_ctx:
            for j in range(past // kb):
                put(j, ck_ref[0, 0, j * kb:(j + 1) * kb, :], cv_ref[0, 0, j * kb:(j + 1) * kb, :])
        for j in range(seq_len // kb):
            kk = k_ref[0, 0, j * kb:(j + 1) * kb, :] if not has_ctx else k_ref[j * kb:(j + 1) * kb, :]
            vv = v_ref[0, 0, j * kb:(j + 1) * kb, :] if not has_ctx else v_ref[j * kb:(j + 1) * kb, :]
            if has_ctx:
                kk = _rope(kk, cosk_ref[j * kb:(j + 1) * kb, :], sink_ref[j * kb:(j + 1) * kb, :])
            put(past // kb + j, kk, vv)

    q = q_ref[...]
    if has_ctx:
        q = _rope(q, cosq_ref[...], sinq_ref[...])
    q = q * (DA_QKDIM ** -0.5 * math.log2(math.e))
    for hm in range(n_maps):
        q_s[hm] = q[:, hm * DA_QKDIM:(hm + 1) * DA_QKDIM].astype(BF16)
    m_s[...] = jnp.full(m_s.shape, -jnp.inf, F32)
    acc_s[...] = jnp.zeros_like(acc_s)

    def key_step(j, carry):
        for hm in range(n_maps):
            kt = kt_s[j, hm * DA_QKDIM:(hm + 1) * DA_QKDIM, :]
            vj = v_s[hm // 2, j]
            for r0 in range(0, tq, CH):
                s = _dot(q_s[hm, r0:r0 + CH, :], kt)
                s_max = jnp.max(jnp.maximum(s[:, :LANES], s[:, LANES:]), axis=-1, keepdims=True)
                m_old = m_s[hm, r0:r0 + CH, :]
                m_new = jnp.maximum(m_old, s_max)
                p = jnp.exp2(s - jnp.concatenate([m_new, m_new], axis=1)).astype(BF16)
                acc_s[hm, r0:r0 + CH, :] = acc_s[hm, r0:r0 + CH, :] * jnp.exp2(m_old - m_new) + _dot(p, vj)
                m_s[hm, r0:r0 + CH, :] = m_new
        return carry

    lax.fori_loop(0, n_kb, key_step, 0)

    lam = (jnp.exp(jnp.sum(lq1[...] * lk1[...], axis=-1, keepdims=True))
           - jnp.exp(jnp.sum(lq2[...] * lk2[...], axis=-1, keepdims=True)) + lam_init)
    for h in range(DA_HEADS):
        o0, o1 = acc_s[2 * h], acc_s[2 * h + 1]
        acc = (o0[:, :DA_VDIM] / o0[:, DA_VDIM:DA_VDIM + 1]
               - lam * (o1[:, :DA_VDIM] / o1[:, DA_VDIM:DA_VDIM + 1]))
        o_ref[:, h * DA_VDIM:(h + 1) * DA_VDIM] = _rms(acc) * ng_ref[...] * (1.0 - lam_init)


def _attn(has_ctx, nb, seq_len, layer, lam_init, q, k, v, lq1, lk1, lq2, lk2, ng, ck=None, cv=None, cos=None, sin=None):
    tq = 256
    nq = seq_len // tq
    past = ck.shape[2] if has_ctx else 0
    n_kb = (seq_len + past) // KEY_BLK
    small = lambda a: pl.BlockSpec(a.shape, lambda b, i: (0,) * a.ndim)
    if has_ctx:
        kv_spec = pl.BlockSpec((seq_len, GROUP_W), lambda b, i: (b, 0))
    else:
        kv_spec = pl.BlockSpec((1, 1, seq_len, GROUP_W), lambda b, i: (b, layer, 0, 0))
    in_specs = [pl.BlockSpec((tq, GROUP_W), lambda b, i: (b * nq + i, 0)), kv_spec, kv_spec]
    in_specs += [small(a) for a in (lq1, lk1, lq2, lk2, ng)]
    args = [q, k, v, lq1, lk1, lq2, lk2, ng]
    if has_ctx:
        in_specs += [
            pl.BlockSpec((1, 1, past, GROUP_W), lambda b, i: (b, layer, 0, 0)),
            pl.BlockSpec((1, 1, past, GROUP_W), lambda b, i: (b, layer, 0, 0)),
            pl.BlockSpec((tq, LANES), lambda b, i: (i, 0)),
            pl.BlockSpec((tq, LANES), lambda b, i: (i, 0)),
            pl.BlockSpec((seq_len, LANES), lambda b, i: (0, 0)),
            pl.BlockSpec((seq_len, LANES), lambda b, i: (0, 0)),
        ]
        args += [ck, cv, cos, sin, cos, sin]
    return pl.pallas_call(
        functools.partial(_attn_kernel, has_ctx, seq_len, tq, past, lam_init),
        grid=(nb, nq),
        in_specs=in_specs,
        out_specs=pl.BlockSpec((tq, GROUP_W), lambda b, i: (b * nq + i, 0)),
        out_shape=jax.ShapeDtypeStruct((nb * seq_len, GROUP_W), F32),
        scratch_shapes=[
            pltpu.VMEM((n_kb, GROUP_W, KEY_BLK), BF16),
            pltpu.VMEM((DA_HEADS, n_kb, KEY_BLK, LANES), BF16),
            pltpu.VMEM((2 * DA_HEADS, tq, DA_QKDIM), BF16),
            pltpu.VMEM((2 * DA_HEADS, tq, LANES), F32),
            pltpu.VMEM((2 * DA_HEADS, tq, LANES), F32),
        ],
        compiler_params=_cparams(2),
        name="attn_ctx" if has_ctx else "attn",
    )(*args)


def _rope_tables(seq_len):
    t = np.arange(seq_len)
    rowp = (t // GRID_W).astype(np.float64)
    colp = (t % GRID_W).astype(np.float64)
    n_freq = DA_QKDIM // 4
    inv_freq = ROPE_BASE ** (-np.arange(n_freq, dtype=np.float64) / n_freq)
    ang = np.concatenate([rowp[:, None] * inv_freq, colp[:, None] * inv_freq], axis=-1)
    ang = np.repeat(ang, 2, axis=-1)
    sign = np.where(np.arange(DA_QKDIM) % 2 == 0, -1.0, 1.0)
    cos = np.tile(np.cos(ang), (1, LANES // DA_QKDIM)).astype(np.float32)
    sin = np.tile(np.sin(ang) * sign, (1, LANES // DA_QKDIM)).astype(np.float32)
    return jnp.asarray(cos), jnp.asarray(sin)


def _rglru_kernel(has_ctx, seq_len, x_ref, g_ref, cw_ref, cb_ref, wa_ref, ba_ref, wx_ref, bx_ref, lam_ref, *rest):
    if has_ctx:
        h0_ref, y_ref, a_s, u_s = rest
        st_ref = None
    else:
        y_ref, st_ref, a_s, u_s = rest
    nc = seq_len // CH
    nt = CH // SUBLANES
    sub = lax.broadcasted_iota(jnp.int32, (nt, SUBLANES, GROUP_W), 1)

    def gate_step(c, carry):
        r0 = pl.multiple_of(c * CH, CH)
        xc = _conv4(_window(x_ref, c, nc, seq_len), cw_ref, cb_ref)
        xb = xc.astype(BF16)
        for d in range(2):
            rg = jax.nn.sigmoid(_dot(xb, wa_ref[d].astype(BF16)) + ba_ref[d])
            ig = jax.nn.sigmoid(_dot(xb, wx_ref[d].astype(BF16)) + bx_ref[d])
            log_a = -RG_C * rg * _softplus(-lam_ref[d])
            a = jnp.exp(log_a)
            u = jnp.sqrt(-jnp.tanh(log_a) * (a * a + 1.0)) * (ig * xc)
            a3 = a.reshape(nt, SUBLANES, GROUP_W)
            u3 = u.reshape(nt, SUBLANES, GROUP_W)
            for k in (1, 2, 4):
                if d == 0:
                    ok = sub >= k
                    a_sh, u_sh = pltpu.roll(a3, k, axis=1), pltpu.roll(u3, k, axis=1)
                else:
                    ok = sub < SUBLANES - k
                    a_sh, u_sh = pltpu.roll(a3, SUBLANES - k, axis=1), pltpu.roll(u3, SUBLANES - k, axis=1)
                u3 = u3 + a3 * jnp.where(ok, u_sh, 0.0)
                a3 = a3 * jnp.where(ok, a_sh, 1.0)
            a_s[d, pl.ds(r0, CH), :] = a3.reshape(CH, GROUP_W)
            u_s[d, pl.ds(r0, CH), :] = u3.reshape(CH, GROUP_W)
        return carry

    lax.fori_loop(0, nc, gate_step, 0)

    n_tiles = seq_len // SUBLANES
    if has_ctx:
        hf0, hb0 = h0_ref[0, 0, 0:1, :], h0_ref[0, 0, 1:2, :]
    else:
        hf0 = hb0 = jnp.zeros((1, GROUP_W), F32)

    def carry_step(i, carry):
        hf, hb = carry
        rf = pl.multiple_of(i * SUBLANES, SUBLANES)
        rb = pl.multiple_of((n_tiles - 1 - i) * SUBLANES, SUBLANES)
        tf = u_s[0, pl.ds(rf, SUBLANES), :] + a_s[0, pl.ds(rf, SUBLANES), :] * hf
        tb = u_s[1, pl.ds(rb, SUBLANES), :] + a_s[1, pl.ds(rb, SUBLANES), :] * hb
        u_s[0, pl.ds(rf, SUBLANES), :] = tf
        u_s[1, pl.ds(rb, SUBLANES), :] = tb
        return tf[SUBLANES - 1:SUBLANES, :], tb[0:1, :]

    hf, hb = lax.fori_loop(0, n_tiles, carry_step, (hf0, hb0), unroll=4)
    if not has_ctx:
        st_ref[0, 0:1, :] = hf
        st_ref[0, 1:2, :] = hb

    def out_step(c, carry):
        r0 = pl.multiple_of(c * CH, CH)
        g = g_ref[pl.ds(r0, CH), :]
        gelu = g * (0.5 * (1.0 + jnp.tanh(math.sqrt(2.0 / math.pi) * (g + 0.044715 * (g * g * g)))))
        y_ref[pl.ds(r0, CH), :] = (u_s[0, pl.ds(r0, CH), :] + u_s[1, pl.ds(r0, CH), :]) * gelu
        return carry

    lax.fori_loop(0, nc, out_step, 0)


def _rglru(has_ctx, nb, seq_len, layer, x, g, cw, cb, wa, ba, wx, bx, lam, h0=None):
    rows = pl.BlockSpec((seq_len, GROUP_W), lambda b: (b, 0))
    full = lambda a: pl.BlockSpec(a.shape, lambda b: (0,) * a.ndim)
    in_specs = [rows, rows] + [full(a) for a in (cw, cb, wa, ba, wx, bx, lam)]
    args = [x, g, cw, cb, wa, ba, wx, bx, lam]
    y_shape = jax.ShapeDtypeStruct((nb * seq_len, GROUP_W), F32)
    if has_ctx:
        in_specs.append(pl.BlockSpec((1, 1, 2, GROUP_W), lambda b: (b, layer, 0, 0)))
        args.append(h0)
        out_specs, out_shape = rows, y_shape
    else:
        out_specs = [rows, pl.BlockSpec((1, 2, GROUP_W), lambda b: (b, 0, 0))]
        out_shape = [y_shape, jax.ShapeDtypeStruct((nb, 2, GROUP_W), F32)]
    return pl.pallas_call(
        functools.partial(_rglru_kernel, has_ctx, seq_len),
        grid=(nb,),
        in_specs=in_specs,
        out_specs=out_specs,
        out_shape=out_shape,
        scratch_shapes=[pltpu.VMEM((2, seq_len, GROUP_W), F32), pltpu.VMEM((2, seq_len, GROUP_W), F32)],
        compiler_params=_cparams(1),
        name="rglru_ctx" if has_ctx else "rglru",
    )(*args)


ROUTE_OFF = N_EGROUPS


def _routing_gate(logits):
    lane = lax.broadcasted_iota(jnp.int32, logits.shape, 1)
    neg = -jnp.inf
    big = LANES
    gl = jnp.where(lane < N_EGROUPS, logits, neg)
    gmax = jnp.max(gl, axis=-1, keepdims=True)
    g_w = 1.0 / jnp.sum(jnp.exp(gl - gmax), axis=-1, keepdims=True)
    g_sel = jnp.min(jnp.where(gl == gmax, lane, big), axis=-1, keepdims=True)
    e_lane = lane - ROUTE_OFF
    in_grp = (e_lane >= 0) & (e_lane < N_EXPERTS) & ((e_lane // N_EPG) == g_sel)
    el = jnp.where(in_grp, logits, neg)
    m1 = jnp.max(el, axis=-1, keepdims=True)
    i1 = jnp.min(jnp.where(el == m1, lane, big), axis=-1, keepdims=True)
    el2 = jnp.where(lane == i1, neg, el)
    m2 = jnp.max(el2, axis=-1, keepdims=True)
    i2 = jnp.min(jnp.where(el2 == m2, lane, big), axis=-1, keepdims=True)
    r = jnp.exp(m2 - m1)
    p1 = 1.0 / (1.0 + r)
    p2 = r / (1.0 + r)
    return jnp.where(lane == i1, g_w * p1, jnp.where(lane == i2, g_w * p2, 0.0))


def _moe_kernel(final, x_ref, m_ref, mix0, mix1, mix2, mix3, wout_ref, g2_ref, wr_ref, br_ref, wg_ref, wu_ref, wd_ref,
                gfin_ref, o_ref, h2_s, gate_s, acc_s):
    grp = pl.program_id(1)

    @pl.when(grp == 0)
    def _start():
        mix = jnp.concatenate([mix0[...], mix1[...], mix2[...], mix3[...]], axis=1).astype(BF16)
        x1 = x_ref[...] + m_ref[0, 2:3] * _dot(mix, wout_ref[...])
        o_ref[...] = x1
        h2 = _rms(x1) * g2_ref[...] * (1.0 + m_ref[0, 4:5]) + m_ref[0, 3:4]
        h2_hi = h2.astype(BF16)
        h2_s[...] = h2_hi
        h2_lo = (h2 - h2_hi.astype(F32)).astype(BF16)
        hi = _dot(h2_hi, wr_ref[...])
        logits = hi[:, :LANES] + hi[:, LANES:] + _dot(h2_lo, wr_ref[:, :LANES]) + br_ref[...]
        gate_s[...] = _routing_gate(logits)

    h2 = h2_s[...]
    lane = lax.broadcasted_iota(jnp.int32, gate_s.shape, 1)
    gate = gate_s[...]
    hid = []
    for j in range(N_EPG):
        gcol = jnp.sum(jnp.where(lane == grp * N_EPG + j + ROUTE_OFF, gate, 0.0), axis=-1, keepdims=True)
        hj = _silu(_dot(h2, wg_ref[j])) * _dot(h2, wu_ref[j])
        hid.append((hj * gcol).astype(BF16))
    part = _dot(jnp.concatenate(hid, axis=1), wd_ref[...])

    @pl.when(grp == 0)
    def _first():
        acc_s[...] = part

    @pl.when(grp > 0)
    def _rest():
        acc_s[...] += part

    @pl.when(grp == N_EGROUPS - 1)
    def _finish():
        x2 = o_ref[...] + m_ref[0, 5:6] * acc_s[...]
        o_ref[...] = _rms(x2) * gfin_ref[...] if final else x2


def _outproj_moe(final, x, mod, mod_row, mixes, wout, g2, wr, br, wg, wu, wd, gfin, tm):
    t = x.shape[0]
    tok = lambda n: pl.BlockSpec((tm, n), lambda i, g: (i, 0))
    const = lambda a: pl.BlockSpec(a.shape, lambda i, g: (0,) * a.ndim)
    return pl.pallas_call(
        functools.partial(_moe_kernel, final),
        grid=(t // tm, N_EGROUPS),
        in_specs=[
            tok(D_MODEL),
            pl.BlockSpec((1, 6, D_MODEL), lambda i, g: (mod_row(i), 0, 0)),
            tok(GROUP_W), tok(GROUP_W), tok(GROUP_W), tok(GROUP_W),
            const(wout), const(g2), const(wr), const(br),
            pl.BlockSpec((N_EPG, D_MODEL, EXPERT_FF), lambda i, g: (g, 0, 0)),
            pl.BlockSpec((N_EPG, D_MODEL, EXPERT_FF), lambda i, g: (g, 0, 0)),
            pl.BlockSpec((N_EPG * EXPERT_FF, D_MODEL), lambda i, g: (g, 0)),
            const(gfin),
        ],
        out_specs=tok(D_MODEL),
        out_shape=jax.ShapeDtypeStruct((t, D_MODEL), F32),
        scratch_shapes=[
            pltpu.VMEM((tm, D_MODEL), BF16),
            pltpu.VMEM((tm, LANES), F32),
            pltpu.VMEM((tm, D_MODEL), F32),
        ],
        compiler_params=_cparams(2),
        name="outproj_moe",
    )(x, mod, *mixes, wout, g2, wr, br, wg, wu, wd, gfin)


def _block_diag(w):
    n, k, _ = w.shape
    eye = jnp.eye(n, dtype=w.dtype)
    return (eye[:, None, :, None] * w[:, :, None, :]).reshape(n * k, n * k)


def _pad_lanes(v, n=LANES):
    return jnp.pad(v, ((0, 0), (0, n - v.shape[-1])))


def kernel(x_prompt, x_sample, c, cache_k, cache_v, state_ssd, state_rglru, c_ctx, w_mod, b_mod, norm1_g, norm2_g, w_in, w_out, ssd_conv_w, ssd_conv_b, ssd_dt_bias, ssd_a_log, ssd_d, ssd_norm_g, pool_w, pool_scale, da_lam_q1, da_lam_k1, da_lam_q2, da_lam_k2, da_norm_g, rg_conv_w, rg_conv_b, rg_wa, rg_ba, rg_wx, rg_bx, rg_lambda, moe_w_group, moe_b_group, moe_w_expert, moe_b_expert, moe_w_gate, moe_w_up, moe_w_down, final_norm_g):
    nbp, lp, _ = x_prompt.shape
    nbs, ls, _ = x_sample.shape
    past = cache_k.shape[2]
    tm_in = 256
    tm_moe = 512
    assert nbs + 1 <= MOD_ROWS and lp % CH == 0 and ls % CH == 0
    assert lp == tm_in and ls % tm_in == 0 and ls % tm_moe == 0 and (nbp * lp) % tm_moe == 0
    assert lp % KEY_BLK == 0 and ls % KEY_BLK == 0 and past % KEY_BLK == 0

    cond = jnp.concatenate([c_ctx[None, :], c, jnp.zeros((MOD_ROWS - 1 - nbs, D_MODEL), F32)], axis=0)
    mod = _modulation(cond, w_mod, b_mod).reshape(DEPTH * MOD_ROWS, 6, D_MODEL)

    w_in_r = jnp.concatenate([w_in[:, :, :DT_LO], w_in[:, :, DT_HI:], w_in[:, :, DT_LO:DT_HI],
                              jnp.zeros((DEPTH, D_MODEL, LANES - (DT_HI - DT_LO)), F32)], axis=-1).astype(BF16)
    w_out_b = w_out.astype(BF16)
    w_gate_b = moe_w_gate.astype(BF16)
    w_up_b = moe_w_up.astype(BF16)
    w_down_b = moe_w_down.astype(BF16).reshape(DEPTH, N_EXPERTS * EXPERT_FF, D_MODEL)
    w_route = jnp.concatenate([moe_w_group, moe_w_expert,
                               jnp.zeros((DEPTH, D_MODEL, LANES - N_EGROUPS - N_EXPERTS), F32)], axis=-1)
    w_route_hi = w_route.astype(BF16)
    w_route = jnp.concatenate([w_route_hi, (w_route - w_route_hi.astype(F32)).astype(BF16)], axis=-1)
    b_route = _pad_lanes(jnp.concatenate([moe_b_group, moe_b_expert], axis=-1))
    dtb = _pad_lanes(ssd_dt_bias.reshape(DEPTH, 2 * SSD_HEADS))
    alog = _pad_lanes(ssd_a_log.reshape(DEPTH, 2 * SSD_HEADS))
    d_skip = jnp.repeat(ssd_d, SSD_HEADDIM, axis=-1)
    cos, sin = _rope_tables(ls)
    ck = cache_k.reshape(nbs, DEPTH, past, GROUP_W)
    cv = cache_v.reshape(nbs, DEPTH, past, GROUP_W)
    g_fin = final_norm_g[None, :]

    xp = x_prompt.reshape(nbp * lp, D_MODEL)
    xs = x_sample.reshape(nbs * ls, D_MODEL)
    new_k = jnp.zeros((nbp, DEPTH, lp, GROUP_W), F32)
    new_v = jnp.zeros((nbp, DEPTH, lp, GROUP_W), F32)
    ssd_out, rg_out = [], []
    for l in range(DEPTH):
        row1 = lambda a: a[l][None, :]
        final = l == DEPTH - 1
        lam_init = 0.8 - 0.6 * math.exp(-0.3 * l)
        ssd_w = (ssd_conv_w[l], row1(ssd_conv_b), row1(dtb), row1(alog), row1(d_skip), row1(ssd_norm_g))
        pool_wb = _block_diag(pool_w[l])
        att_w = (row1(da_lam_q1), row1(da_lam_k1), row1(da_lam_q2), row1(da_lam_k2), row1(da_norm_g))
        rg_w = (rg_conv_w[l], row1(rg_conv_b),
                jnp.stack([_block_diag(rg_wa[l, 0]), _block_diag(rg_wa[l, 1])]), rg_ba[l][:, None, :],
                jnp.stack([_block_diag(rg_wx[l, 0]), _block_diag(rg_wx[l, 1])]), rg_bx[l][:, None, :],
                rg_lambda[l][:, None, :])
        moe_w = (w_out_b[l], row1(norm2_g), w_route[l], row1(b_route), w_gate_b[l], w_up_b[l], w_down_b[l], g_fin)
        ctx_row = lambda i, l=l: l * MOD_ROWS
        lat_row = lambda tm: (lambda i, l=l: l * MOD_ROWS + 1 + i // (ls // tm))

        xbc, z, xpool, q, new_k, new_v, xr, gr, dt = _inproj(
            xp, mod, ctx_row, row1(norm1_g), w_in_r[l], tm_in, l, caches=(new_k, new_v))
        ya, st_ssd = _ssd(False, nbp, lp, l, xbc, z, dt, *ssd_w)
        yb = _pool(nbp, lp, xpool, pool_wb, row1(pool_scale))
        yc = _attn(False, nbp, lp, l, lam_init, q, new_k, new_v, *att_w)
        yd, st_rg = _rglru(False, nbp, lp, l, xr, gr, *rg_w)
        xp = _outproj_moe(final, xp, mod, ctx_row, (ya, yb, yc, yd), *moe_w, tm_moe)
        ssd_out.append(st_ssd)
        rg_out.append(st_rg)

        xbc, z, xpool, q, k, v, xr, gr, dt = _inproj(xs, mod, lat_row(tm_in), row1(norm1_g), w_in_r[l], tm_in, l)
        ya = _ssd(True, nbs, ls, l, xbc, z, dt, *ssd_w, h0=state_ssd)
        yb = _pool(nbs, ls, xpool, pool_wb, row1(pool_scale))
        yc = _attn(True, nbs, ls, l, lam_init, q, k, v, *att_w, ck=ck, cv=cv, cos=cos, sin=sin)
        yd = _rglru(True, nbs, ls, l, xr, gr, *rg_w, h0=state_rglru)
        xs = _outproj_moe(final, xs, mod, lat_row(tm_moe), (ya, yb, yc, yd), *moe_w, tm_moe)

    return (xp.reshape(nbp, lp, D_MODEL), xs.reshape(nbs, ls, D_MODEL),
            new_k.reshape(nbp, DEPTH, lp, DA_HEADS, 2 * DA_QKDIM), new_v.reshape(nbp, DEPTH, lp, DA_HEADS, DA_VDIM),
            jnp.stack(ssd_out, axis=1), jnp.stack(rg_out, axis=1))
```

```python
import functools
import math

import numpy as np
import jax
import jax.numpy as jnp
from jax import lax
from jax.experimental import pallas as pl
from jax.experimental.pallas import tpu as pltpu

F32 = jnp.float32
BF16 = jnp.bfloat16
HIGHEST = lax.Precision.HIGHEST

D_MODEL = 1024
DEPTH = 4
GRID_W = 64
GROUP_W = 256
EPS = 1e-6
SSD_HEADDIM = 64
SSD_HEADS = 4
SSD_STATE = 64
SSD_BC = 128
SSD_CONV_CH = 512
POOL_WINDOWS = (2, 4, 8, 16)
DA_HEADS = 4
DA_VDIM = 64
DA_QKDIM = 32
ROPE_BASE = 10000.0
RG_C = 8.0
N_EGROUPS = 4
N_EPG = 4
N_EXPERTS = 16
EXPERT_FF = 256
DT_LO, DT_HI = 768, 776

LANES = 128
SUBLANES = 8
CH = 128
HALO = SUBLANES
MOD_ROWS = 16
VMEM_LIMIT = 56 * 1024 * 1024


def _cparams(n_axes):
    return pltpu.CompilerParams(dimension_semantics=("arbitrary",) * n_axes, vmem_limit_bytes=VMEM_LIMIT)


def _silu(x):
    return x * jax.nn.sigmoid(x)


def _softplus(x):
    return jnp.maximum(x, 0.0) + jnp.log1p(jnp.exp(-jnp.abs(x)))


def _dot(a, b, **kw):
    return jnp.dot(a, b, preferred_element_type=F32, **kw)


def _dot_nt(a, b):
    return lax.dot_general(a, b, (((1,), (1,)), ((), ())), preferred_element_type=F32)


def _rms(x):
    return x * lax.rsqrt(jnp.mean(x * x, axis=-1, keepdims=True) + EPS)


def _window(ref, c, n_steps, seq_len):
    r0 = pl.multiple_of(c * CH, CH)
    main = ref[pl.ds(r0, CH), :]
    lo = pl.multiple_of(jnp.maximum(r0 - HALO, 0), HALO)
    hi = pl.multiple_of(jnp.minimum(r0 + CH, seq_len - HALO), HALO)
    prev = jnp.where(c > 0, ref[pl.ds(lo, HALO), :], 0.0)
    nxt = jnp.where(c < n_steps - 1, ref[pl.ds(hi, HALO), :], 0.0)
    return jnp.concatenate([prev, main, nxt], axis=0)


def _conv4(win, w_ref, b_ref):
    acc = b_ref[...]
    for k in range(4):
        acc = acc + w_ref[k:k + 1, :] * win[HALO - 1 + k:HALO - 1 + k + CH, :]
    return acc


def _mod_kernel(cond_ref, w_ref, b_ref, o_ref):
    cnd = cond_ref[...]
    o_ref[0] = _dot(_silu(cnd), w_ref[0], precision=HIGHEST) + b_ref[0]


def _modulation(cond, w_mod, b_mod):
    nb = 6
    return pl.pallas_call(
        _mod_kernel,
        grid=(DEPTH, nb),
        in_specs=[
            pl.BlockSpec((MOD_ROWS, D_MODEL), lambda l, j: (0, 0)),
            pl.BlockSpec((1, D_MODEL, D_MODEL), lambda l, j: (l, 0, j)),
            pl.BlockSpec((1, 1, D_MODEL), lambda l, j: (l, 0, j)),
        ],
        out_specs=pl.BlockSpec((1, MOD_ROWS, D_MODEL), lambda l, j: (l, 0, j)),
        out_shape=jax.ShapeDtypeStruct((DEPTH, MOD_ROWS, nb * D_MODEL), F32),
        compiler_params=_cparams(2),
        name="modulation",
    )(cond, w_mod, b_mod.reshape(DEPTH, 1, nb * D_MODEL))


IN_COLS = (512, 256, 256, 256, 256, 256, 256, 256, LANES)
K_OUT, V_OUT = 4, 5


def _inproj_kernel(to_cache, x_ref, m_ref, g_ref, w_ref, *refs):
    if to_cache:
        refs = refs[2:]
    hh = _rms(x_ref[...]) * g_ref[...] * (1.0 + m_ref[0, 1:2]) + m_ref[0, 0:1]
    u = _dot(hh.astype(BF16), w_ref[...])
    off = 0
    for j, (ref, n) in enumerate(zip(refs, IN_COLS)):
        if to_cache and j in (K_OUT, V_OUT):
            ref[0, 0] = u[:, off:off + n]
        else:
            ref[...] = u[:, off:off + n]
        off += n


def _inproj(x, mod, mod_row, g, w, tm, layer, caches=None):
    t = x.shape[0]
    ncol = sum(IN_COLS)
    to_cache = caches is not None
    in_specs = [
        pl.BlockSpec((tm, D_MODEL), lambda i: (i, 0)),
        pl.BlockSpec((1, 6, D_MODEL), lambda i: (mod_row(i), 0, 0)),
        pl.BlockSpec((1, D_MODEL), lambda i: (0, 0)),
        pl.BlockSpec((D_MODEL, ncol), lambda i: (0, 0)),
    ]
    out_specs = [pl.BlockSpec((tm, n), lambda i: (i, 0)) for n in IN_COLS]
    out_shape = [jax.ShapeDtypeStruct((t, n), F32) for n in IN_COLS]
    args = [x, mod, g, w]
    aliases = {}
    if to_cache:
        assert caches[0].shape[2] == tm
        cache_spec = pl.BlockSpec((1, 1, tm, GROUP_W), lambda i: (i, layer, 0, 0))
        for j, cch in zip((K_OUT, V_OUT), caches):
            in_specs.append(pl.BlockSpec(memory_space=pl.ANY))
            aliases[len(args)] = j
            args.append(cch)
            out_specs[j] = cache_spec
            out_shape[j] = jax.ShapeDtypeStruct(cch.shape, F32)
    return pl.pallas_call(
        functools.partial(_inproj_kernel, to_cache),
        grid=(t // tm,),
        in_specs=in_specs,
        out_specs=out_specs,
        out_shape=out_shape,
        input_output_aliases=aliases,
        compiler_params=_cparams(1),
        name="inproj_ctx" if to_cache else "inproj",
    )(*args)


def _ssd_kernel(has_ctx, seq_len, xbc_ref, z_ref, dt_ref, cw_ref, cb_ref, dtb_ref, alog_ref, d_ref, ng_ref, *rest):
    if has_ctx:
        h0_ref, y_ref, xc_s, y_s, st_s = rest
        st_ref = None
    else:
        y_ref, st_ref, xc_s, y_s, st_s = rest
    nc = seq_len // CH
    hd = SSD_HEADDIM
    a_row = -jnp.exp(alog_ref[...])
    row = lax.broadcasted_iota(jnp.int32, (CH, CH), 0)
    col = lax.broadcasted_iota(jnp.int32, (CH, CH), 1)
    lane_w = lax.broadcasted_iota(jnp.int32, (CH, GROUP_W), 1)
    lane_n = lax.broadcasted_iota(jnp.int32, (CH, SSD_BC), 1)
    own_block = (lax.broadcasted_iota(jnp.int32, (SSD_BC, GROUP_W), 0) // SSD_STATE
                 == lax.broadcasted_iota(jnp.int32, (SSD_BC, GROUP_W), 1) // (2 * hd))

    def conv_step(c, carry):
        r0 = pl.multiple_of(c * CH, CH)
        xc = _silu(_conv4(_window(xbc_ref, c, nc, seq_len), cw_ref, cb_ref))
        xc_s[pl.ds(r0, CH), :] = xc
        y_s[pl.ds(r0, CH), :] = xc[:, :GROUP_W] * d_ref[...]
        return carry

    lax.fori_loop(0, nc, conv_step, 0)

    st_s[...] = jnp.zeros_like(st_s)
    if has_ctx:
        for d in range(2):
            for h in range(SSD_HEADS):
                g = h // 2
                st_s[d, g * SSD_STATE:(g + 1) * SSD_STATE, h * hd:(h + 1) * hd] = h0_ref[0, 0, d, h].T

    def per_head(v, d):
        lanes = lane_w[:v.shape[0]]
        out = jnp.broadcast_to(v[:, 4 * d + 3:4 * d + 4], (v.shape[0], GROUP_W))
        for h in (2, 1, 0):
            out = jnp.where(lanes < (h + 1) * hd, jnp.broadcast_to(v[:, 4 * d + h:4 * d + h + 1], out.shape), out)
        return out

    def scan_dir(d, ci):
        r0 = pl.multiple_of(ci * CH, CH)
        xc = xc_s[pl.ds(r0, CH), :]
        x = xc[:, :GROUP_W]
        bm = xc[:, GROUP_W:GROUP_W + SSD_BC]
        cm = xc[:, GROUP_W + SSD_BC:]
        dt = _softplus(dt_ref[pl.ds(r0, CH), :] + dtb_ref[...])
        tri = (row >= col) if d == 0 else (row <= col)
        cs = _dot(tri.astype(F32), dt * a_row, precision=HIGHEST)
        cs_t = cs.T
        tot = cs[CH - 1:CH, :] if d == 0 else cs[0:1, :]
        dt_e, cs_e, tot_e = per_head(dt, d), per_head(cs, d), per_head(tot, d)
        xdt = x * dt_e
        bmb, cmb = bm.astype(BF16), cm.astype(BF16)
        scores = [_dot_nt(jnp.where((lane_n // SSD_STATE) == g, cm, 0.0).astype(BF16), bmb) for g in range(2)]
        m_parts, r_parts = [], []
        for h in range(SSD_HEADS):
            k = SSD_HEADS * d + h
            decay = jnp.exp(jnp.where(tri, cs[:, k:k + 1] - cs_t[k:k + 1, :], -jnp.inf))
            m_parts.append((scores[h // 2] * decay).astype(BF16))
            r_parts.append(jnp.where((lane_w // hd) == h, xdt, 0.0).astype(BF16))
        y_diag = _dot(jnp.concatenate(m_parts, axis=1), jnp.concatenate(r_parts, axis=0))
        st = st_s[d]
        y_off = _dot(cmb, st.astype(BF16)) * jnp.exp(cs_e)
        wgt = xdt * jnp.exp(tot_e - cs_e)
        upd = _dot(bm.T.astype(BF16), wgt.astype(BF16))
        st_s[d] = st * jnp.exp(tot_e) + jnp.where(own_block, upd, 0.0)
        y_s[pl.ds(r0, CH), :] += y_diag + y_off

    def scan_step(c, carry):
        scan_dir(0, c)
        scan_dir(1, nc - 1 - c)
        return carry

    lax.fori_loop(0, nc, scan_step, 0, unroll=2)

    def out_step(c, carry):
        r0 = pl.multiple_of(c * CH, CH)
        y = y_s[pl.ds(r0, CH), :] * _silu(z_ref[pl.ds(r0, CH), :])
        y_ref[pl.ds(r0, CH), :] = _rms(y) * ng_ref[...]
        return carry

    lax.fori_loop(0, nc, out_step, 0)
    if not has_ctx:
        for d in range(2):
            for h in range(SSD_HEADS):
                g = h // 2
                st_ref[0, d, h] = st_s[d, g * SSD_STATE:(g + 1) * SSD_STATE, h * hd:(h + 1) * hd].T


def _ssd(has_ctx, nb, seq_len, layer, xbc, z, dt, cw, cb, dtb, alog, dsk, ng, h0=None):
    rows = lambda n: pl.BlockSpec((seq_len, n), lambda b: (b, 0))
    full = lambda a: pl.BlockSpec(a.shape, lambda b: (0,) * a.ndim)
    in_specs = [rows(SSD_CONV_CH), rows(GROUP_W), rows(LANES)] + [full(a) for a in (cw, cb, dtb, alog, dsk, ng)]
    args = [xbc, z, dt, cw, cb, dtb, alog, dsk, ng]
    y_spec = rows(GROUP_W)
    y_shape = jax.ShapeDtypeStruct((nb * seq_len, GROUP_W), F32)
    st_blk = (1, 2, SSD_HEADS, SSD_HEADDIM, SSD_STATE)
    if has_ctx:
        in_specs.append(pl.BlockSpec((1, 1) + st_blk[1:], lambda b: (b, layer, 0, 0, 0, 0)))
        args.append(h0)
        out_specs, out_shape = y_spec, y_shape
    else:
        out_specs = [y_spec, pl.BlockSpec(st_blk, lambda b: (b, 0, 0, 0, 0))]
        out_shape = [y_shape, jax.ShapeDtypeStruct((nb,) + st_blk[1:], F32)]
    return pl.pallas_call(
        functools.partial(_ssd_kernel, has_ctx, seq_len),
        grid=(nb,),
        in_specs=in_specs,
        out_specs=out_specs,
        out_shape=out_shape,
        scratch_shapes=[
            pltpu.VMEM((seq_len, SSD_CONV_CH), F32),
            pltpu.VMEM((seq_len, GROUP_W), F32),
            pltpu.VMEM((2, SSD_BC, GROUP_W), F32),
        ],
        compiler_params=_cparams(1),
        name="ssd_ctx" if has_ctx else "ssd",
    )(*args)


def _pool_kernel(seq_len, x_ref, w_ref, sc_ref, y_ref):
    nc = seq_len // CH
    wn = CH + 2 * HALO
    lane = lax.broadcasted_iota(jnp.int32, (CH, GROUP_W), 1)
    gw = GROUP_W // len(POOL_WINDOWS)
    half = jnp.where(lane < gw, 1, jnp.where(lane < 2 * gw, 2, jnp.where(lane < 3 * gw, 4, 8)))
    w_blk = w_ref[...].astype(BF16)

    def ahead(v, k):
        return pltpu.roll(v, wn - k, axis=0)

    def step(c, carry):
        r0 = pl.multiple_of(c * CH, CH)
        win = _window(x_ref, c, nc, seq_len)
        p2 = win + ahead(win, 1)
        p4 = p2 + ahead(p2, 2)
        p8 = p4 + ahead(p4, 4)
        p16 = p8 + ahead(p8, 8)
        s2 = ahead(p2, HALO - 1)[:CH]
        s4 = ahead(p4, HALO - 2)[:CH]
        s8 = ahead(p8, HALO - 4)[:CH]
        s16 = p16[:CH]
        tot = jnp.where(lane < gw, s2, jnp.where(lane < 2 * gw, s4, jnp.where(lane < 3 * gw, s8, s16)))
        t = r0 + lax.broadcasted_iota(jnp.int32, (CH, GROUP_W), 0)
        cnt = jnp.minimum(t + half, seq_len) - jnp.maximum(t - half, 0)
        x = win[HALO:HALO + CH]
        diff = tot / cnt.astype(F32) - x
        y_ref[pl.ds(r0, CH), :] = _dot(diff.astype(BF16), w_blk) * sc_ref[...]
        return carry

    lax.fori_loop(0, nc, step, 0)


def _pool(nb, seq_len, x, w_blk, scale):
    return pl.pallas_call(
        functools.partial(_pool_kernel, seq_len),
        grid=(nb,),
        in_specs=[
            pl.BlockSpec((seq_len, GROUP_W), lambda b: (b, 0)),
            pl.BlockSpec((GROUP_W, GROUP_W), lambda b: (0, 0)),
            pl.BlockSpec((1, GROUP_W), lambda b: (0, 0)),
        ],
        out_specs=pl.BlockSpec((seq_len, GROUP_W), lambda b: (b, 0)),
        out_shape=jax.ShapeDtypeStruct((nb * seq_len, GROUP_W), F32),
        compiler_params=_cparams(1),
        name="pool",
    )(x, w_blk, scale)


KEY_BLK = 256


def _rope(x, cos, sin):
    c2 = jnp.concatenate([cos, cos], axis=1)
    s2 = jnp.concatenate([sin, sin], axis=1)
    lane = lax.broadcasted_iota(jnp.int32, x.shape, 1)
    n = x.shape[1]
    partner = jnp.where(lane % 2 == 0, pltpu.roll(x, n - 1, axis=1), pltpu.roll(x, 1, axis=1))
    return x * c2 + partner * s2


def _attn_kernel(has_ctx, seq_len, tq, past, lam_init, q_ref, k_ref, v_ref, lq1, lk1, lq2, lk2, ng_ref, *rest):
    if has_ctx:
        ck_ref, cv_ref, cosq_ref, sinq_ref, cosk_ref, sink_ref, o_ref, kt_s, v_s = rest
    else:
        o_ref, kt_s, v_s = rest
    kb = KEY_BLK

    @pl.when(pl.program_id(1) == 0)
    def _prepare_keys():
        def put(dst0, kk, vv):
            kt_s[:, dst0:dst0 + kb] = kk.T.astype(BF16)
            for h in range(DA_HEADS):
                v_s[h, dst0:dst0 + kb, :] = vv[:, h * DA_VDIM:(h + 1) * DA_VDIM].astype(BF16)

        if has_ctx:
            for j in range(past // kb):
                put(j * kb, ck_ref[0, 0, j * kb:(j + 1) * kb, :], cv_ref[0, 0, j * kb:(j + 1) * kb, :])
        for j in range(seq_len // kb):
            kk = k_ref[0, 0, j * kb:(j + 1) * kb, :] if not has_ctx else k_ref[j * kb:(j + 1) * kb, :]
            vv = v_ref[0, 0, j * kb:(j + 1) * kb, :] if not has_ctx else v_ref[j * kb:(j + 1) * kb, :]
            if has_ctx:
                kk = _rope(kk, cosk_ref[j * kb:(j + 1) * kb, :], sink_ref[j * kb:(j + 1) * kb, :])
            put(past + j * kb, kk, vv)

    q = q_ref[...]
    if has_ctx:
        q = _rope(q, cosq_ref[...], sinq_ref[...])
    q = q * (DA_QKDIM ** -0.5 * math.log2(math.e))
    lam = (jnp.exp(jnp.sum(lq1[...] * lk1[...], axis=-1, keepdims=True))
           - jnp.exp(jnp.sum(lq2[...] * lk2[...], axis=-1, keepdims=True)) + lam_init)
    for h in range(DA_HEADS):
        acc = None
        for m in range(2):
            lo = h * 2 * DA_QKDIM + m * DA_QKDIM
            s = _dot(q[:, lo:lo + DA_QKDIM].astype(BF16), kt_s[lo:lo + DA_QKDIM, :])
            e = jnp.exp2(s - jnp.max(s, axis=-1, keepdims=True))
            o = _dot(e.astype(BF16), v_s[h]) / jnp.sum(e, axis=-1, keepdims=True)
            acc = o if m == 0 else acc - lam * o
        o_ref[:, h * DA_VDIM:(h + 1) * DA_VDIM] = _rms(acc) * ng_ref[...] * (1.0 - lam_init)


def _attn(has_ctx, nb, seq_len, layer, lam_init, q, k, v, lq1, lk1, lq2, lk2, ng, ck=None, cv=None, cos=None, sin=None):
    tq = 128
    nq = seq_len // tq
    past = ck.shape[2] if has_ctx else 0
    keys = seq_len + past
    small = lambda a: pl.BlockSpec(a.shape, lambda b, i: (0,) * a.ndim)
    if has_ctx:
        kv_spec = pl.BlockSpec((seq_len, GROUP_W), lambda b, i: (b, 0))
    else:
        kv_spec = pl.BlockSpec((1, 1, seq_len, GROUP_W), lambda b, i: (b, layer, 0, 0))
    in_specs = [pl.BlockSpec((tq, GROUP_W), lambda b, i: (b * nq + i, 0)), kv_spec, kv_spec]
    in_specs += [small(a) for a in (lq1, lk1, lq2, lk2, ng)]
    args = [q, k, v, lq1, lk1, lq2, lk2, ng]
    if has_ctx:
        in_specs += [
            pl.BlockSpec((1, 1, past, GROUP_W), lambda b, i: (b, layer, 0, 0)),
            pl.BlockSpec((1, 1, past, GROUP_W), lambda b, i: (b, layer, 0, 0)),
            pl.BlockSpec((tq, LANES), lambda b, i: (i, 0)),
            pl.BlockSpec((tq, LANES), lambda b, i: (i, 0)),
            pl.BlockSpec((seq_len, LANES), lambda b, i: (0, 0)),
            pl.BlockSpec((seq_len, LANES), lambda b, i: (0, 0)),
        ]
        args += [ck, cv, cos, sin, cos, sin]
    return pl.pallas_call(
        functools.partial(_attn_kernel, has_ctx, seq_len, tq, past, lam_init),
        grid=(nb, nq),
        in_specs=in_specs,
        out_specs=pl.BlockSpec((tq, GROUP_W), lambda b, i: (b * nq + i, 0)),
        out_shape=jax.ShapeDtypeStruct((nb * seq_len, GROUP_W), F32),
        scratch_shapes=[
            pltpu.VMEM((GROUP_W, keys), BF16),
            pltpu.VMEM((DA_HEADS, keys, DA_VDIM), BF16),
        ],
        compiler_params=_cparams(2),
        name="attn_ctx" if has_ctx else "attn",
    )(*args)


def _rope_tables(seq_len):
    t = np.arange(seq_len)
    rowp = (t // GRID_W).astype(np.float64)
    colp = (t % GRID_W).astype(np.float64)
    n_freq = DA_QKDIM // 4
    inv_freq = ROPE_BASE ** (-np.arange(n_freq, dtype=np.float64) / n_freq)
    ang = np.concatenate([rowp[:, None] * inv_freq, colp[:, None] * inv_freq], axis=-1)
    ang = np.repeat(ang, 2, axis=-1)
    sign = np.where(np.arange(DA_QKDIM) % 2 == 0, -1.0, 1.0)
    cos = np.tile(np.cos(ang), (1, LANES // DA_QKDIM)).astype(np.float32)
    sin = np.tile(np.sin(ang) * sign, (1, LANES // DA_QKDIM)).astype(np.float32)
    return jnp.asarray(cos), jnp.asarray(sin)


def _rglru_kernel(has_ctx, seq_len, x_ref, g_ref, cw_ref, cb_ref, wa_ref, ba_ref, wx_ref, bx_ref, lam_ref, *rest):
    if has_ctx:
        h0_ref, y_ref, a_s, u_s = rest
        st_ref = None
    else:
        y_ref, st_ref, a_s, u_s = rest
    nc = seq_len // CH
    nt = CH // SUBLANES
    sub = lax.broadcasted_iota(jnp.int32, (nt, SUBLANES, GROUP_W), 1)

    def gate_step(c, carry):
        r0 = pl.multiple_of(c * CH, CH)
        xc = _conv4(_window(x_ref, c, nc, seq_len), cw_ref, cb_ref)
        xb = xc.astype(BF16)
        for d in range(2):
            rg = jax.nn.sigmoid(_dot(xb, wa_ref[d].astype(BF16)) + ba_ref[d])
            ig = jax.nn.sigmoid(_dot(xb, wx_ref[d].astype(BF16)) + bx_ref[d])
            log_a = -RG_C * rg * _softplus(-lam_ref[d])
            a = jnp.exp(log_a)
            u = jnp.sqrt(-jnp.tanh(log_a) * (a * a + 1.0)) * (ig * xc)
            a3 = a.reshape(nt, SUBLANES, GROUP_W)
            u3 = u.reshape(nt, SUBLANES, GROUP_W)
            for k in (1, 2, 4):
                if d == 0:
                    ok = sub >= k
                    a_sh, u_sh = pltpu.roll(a3, k, axis=1), pltpu.roll(u3, k, axis=1)
                else:
                    ok = sub < SUBLANES - k
                    a_sh, u_sh = pltpu.roll(a3, SUBLANES - k, axis=1), pltpu.roll(u3, SUBLANES - k, axis=1)
                u3 = u3 + a3 * jnp.where(ok, u_sh, 0.0)
                a3 = a3 * jnp.where(ok, a_sh, 1.0)
            a_s[d, pl.ds(r0, CH), :] = a3.reshape(CH, GROUP_W)
            u_s[d, pl.ds(r0, CH), :] = u3.reshape(CH, GROUP_W)
        return carry

    lax.fori_loop(0, nc, gate_step, 0)

    n_tiles = seq_len // SUBLANES
    if has_ctx:
        hf0, hb0 = h0_ref[0, 0, 0:1, :], h0_ref[0, 0, 1:2, :]
    else:
        hf0 = hb0 = jnp.zeros((1, GROUP_W), F32)

    def carry_step(i, carry):
        hf, hb = carry
        rf = pl.multiple_of(i * SUBLANES, SUBLANES)
        rb = pl.multiple_of((n_tiles - 1 - i) * SUBLANES, SUBLANES)
        tf = u_s[0, pl.ds(rf, SUBLANES), :] + a_s[0, pl.ds(rf, SUBLANES), :] * hf
        tb = u_s[1, pl.ds(rb, SUBLANES), :] + a_s[1, pl.ds(rb, SUBLANES), :] * hb
        u_s[0, pl.ds(rf, SUBLANES), :] = tf
        u_s[1, pl.ds(rb, SUBLANES), :] = tb
        return tf[SUBLANES - 1:SUBLANES, :], tb[0:1, :]

    hf, hb = lax.fori_loop(0, n_tiles, carry_step, (hf0, hb0), unroll=4)
    if not has_ctx:
        st_ref[0, 0:1, :] = hf
        st_ref[0, 1:2, :] = hb

    def out_step(c, carry):
        r0 = pl.multiple_of(c * CH, CH)
        g = g_ref[pl.ds(r0, CH), :]
        gelu = g * (0.5 * (1.0 + jnp.tanh(math.sqrt(2.0 / math.pi) * (g + 0.044715 * (g * g * g)))))
        y_ref[pl.ds(r0, CH), :] = (u_s[0, pl.ds(r0, CH), :] + u_s[1, pl.ds(r0, CH), :]) * gelu
        return carry

    lax.fori_loop(0, nc, out_step, 0)


def _rglru(has_ctx, nb, seq_len, layer, x, g, cw, cb, wa, ba, wx, bx, lam, h0=None):
    rows = pl.BlockSpec((seq_len, GROUP_W), lambda b: (b, 0))
    full = lambda a: pl.BlockSpec(a.shape, lambda b: (0,) * a.ndim)
    in_specs = [rows, rows] + [full(a) for a in (cw, cb, wa, ba, wx, bx, lam)]
    args = [x, g, cw, cb, wa, ba, wx, bx, lam]
    y_shape = jax.ShapeDtypeStruct((nb * seq_len, GROUP_W), F32)
    if has_ctx:
        in_specs.append(pl.BlockSpec((1, 1, 2, GROUP_W), lambda b: (b, layer, 0, 0)))
        args.append(h0)
        out_specs, out_shape = rows, y_shape
    else:
        out_specs = [rows, pl.BlockSpec((1, 2, GROUP_W), lambda b: (b, 0, 0))]
        out_shape = [y_shape, jax.ShapeDtypeStruct((nb, 2, GROUP_W), F32)]
    return pl.pallas_call(
        functools.partial(_rglru_kernel, has_ctx, seq_len),
        grid=(nb,),
        in_specs=in_specs,
        out_specs=out_specs,
        out_shape=out_shape,
        scratch_shapes=[pltpu.VMEM((2, seq_len, GROUP_W), F32), pltpu.VMEM((2, seq_len, GROUP_W), F32)],
        compiler_params=_cparams(1),
        name="rglru_ctx" if has_ctx else "rglru",
    )(*args)


ROUTE_OFF = N_EGROUPS
SORT_TM = 256
RUN_PAD = 16
SLOTS = SORT_TM + N_EGROUPS * RUN_PAD
SLOTS_PAD = 384
RUN_BITS = (16, 32, 64, 128, 256)
FFN_BLK = 512
GS_COLS = D_MODEL + LANES
REGROUP_LAG = 32


def _routing_gate(logits):
    lane = lax.broadcasted_iota(jnp.int32, logits.shape, 1)
    neg = -jnp.inf
    big = LANES
    gl = jnp.where(lane < N_EGROUPS, logits, neg)
    gmax = jnp.max(gl, axis=-1, keepdims=True)
    g_w = 1.0 / jnp.sum(jnp.exp(gl - gmax), axis=-1, keepdims=True)
    g_sel = jnp.min(jnp.where(gl == gmax, lane, big), axis=-1, keepdims=True)
    e_lane = lane - ROUTE_OFF
    in_grp = (e_lane >= 0) & (e_lane < N_EXPERTS) & ((e_lane // N_EPG) == g_sel)
    el = jnp.where(in_grp, logits, neg)
    m1 = jnp.max(el, axis=-1, keepdims=True)
    i1 = jnp.min(jnp.where(el == m1, lane, big), axis=-1, keepdims=True)
    el2 = jnp.where(lane == i1, neg, el)
    m2 = jnp.max(el2, axis=-1, keepdims=True)
    i2 = jnp.min(jnp.where(el2 == m2, lane, big), axis=-1, keepdims=True)
    r = jnp.exp(m2 - m1)
    p1 = 1.0 / (1.0 + r)
    p2 = r / (1.0 + r)
    return jnp.where(lane == i1, g_w * p1, jnp.where(lane == i2, g_w * p2, 0.0)), g_sel


def _route_kernel(x_ref, m_ref, mix0, mix1, mix2, mix3, wout_ref, g2_ref, wr_ref, br_ref,
                  x1_ref, srt_ref, cnt_ref, slot_ref):
    tm = SORT_TM
    mix = jnp.concatenate([mix0[...], mix1[...], mix2[...], mix3[...]], axis=1).astype(BF16)
    x1 = x_ref[...] + m_ref[0, 2:3] * _dot(mix, wout_ref[...])
    x1_ref[...] = x1
    h2 = _rms(x1) * g2_ref[...] * (1.0 + m_ref[0, 4:5]) + m_ref[0, 3:4]
    h2_hi = h2.astype(BF16)
    h2_lo = (h2 - h2_hi.astype(F32)).astype(BF16)
    hi = _dot(h2_hi, wr_ref[...])
    logits = hi[:, :LANES] + hi[:, LANES:] + _dot(h2_lo, wr_ref[:, :LANES]) + br_ref[...]
    gate, g_sel = _routing_gate(logits)

    lane = lax.broadcasted_iota(jnp.int32, (tm, LANES), 1)
    onehot = jnp.where(lane == g_sel, 1.0, 0.0)
    cnt_ref[0] = jnp.sum(onehot, axis=0, keepdims=True)
    onehot_t = onehot.T
    earlier = jnp.where(lax.broadcasted_iota(jnp.int32, (tm, tm), 0) < lax.broadcasted_iota(jnp.int32, (tm, tm), 1),
                        1.0, 0.0).astype(BF16)
    rank_t = _dot(onehot_t.astype(BF16), earlier)
    cnt = jnp.sum(onehot_t, axis=1, keepdims=True)
    padded = jnp.ceil(cnt * (1.0 / RUN_PAD)) * RUN_PAD
    grp = lax.broadcasted_iota(jnp.int32, (LANES, 1), 0)
    start = jnp.zeros((LANES, 1), F32)
    for g in range(N_EGROUPS - 1):
        start = start + jnp.where(grp > g, padded[g:g + 1, :], 0.0)
    slot = jnp.sum(onehot_t * (rank_t + start), axis=0, keepdims=True)
    slot_ref[0] = slot
    perm = lax.broadcasted_iota(jnp.int32, (SLOTS, tm), 0).astype(F32) == slot
    srt_h = _dot(jnp.where(perm, 1.0, 0.0).astype(BF16), h2_hi)
    srt_g = _dot(jnp.where(perm, 1.0, 0.0), gate, precision=HIGHEST)
    srt_ref[...] = jnp.concatenate([srt_h, srt_g], axis=1)


def _route(x, mod, mod_row, mixes, wout, g2, wr, br):
    t = x.shape[0]
    tm = SORT_TM
    nt = t // tm
    tok = lambda n: pl.BlockSpec((tm, n), lambda i: (i, 0))
    const = lambda a: pl.BlockSpec(a.shape, lambda i: (0,) * a.ndim)
    return pl.pallas_call(
        _route_kernel,
        grid=(nt,),
        in_specs=[tok(D_MODEL), pl.BlockSpec((1, 6, D_MODEL), lambda i: (mod_row(i), 0, 0)),
                  tok(GROUP_W), tok(GROUP_W), tok(GROUP_W), tok(GROUP_W),
                  const(wout), const(g2), const(wr), const(br)],
        out_specs=[tok(D_MODEL), pl.BlockSpec((SLOTS, GS_COLS), lambda i: (i, 0)),
                   pl.BlockSpec((1, 1, LANES), lambda i: (i, 0, 0)), pl.BlockSpec((1, 1, tm), lambda i: (i, 0, 0))],
        out_shape=[jax.ShapeDtypeStruct((t, D_MODEL), F32), jax.ShapeDtypeStruct((nt * SLOTS, GS_COLS), F32),
                   jax.ShapeDtypeStruct((nt, 1, LANES), F32), jax.ShapeDtypeStruct((nt, 1, tm), F32)],
        compiler_params=_cparams(1),
        name="route",
    )(x, mod, *mixes, wout, g2, wr, br)


def _for_each_piece(n_rows, fn):
    off = 0
    for bit in RUN_BITS:
        has = (n_rows & bit) != 0
        pl.when(has)(functools.partial(fn, off, bit))
        off = off + jnp.where(has, bit, 0)


def _regroup_kernel(n_runs, src0_ref, len_ref, dst0_ref, tail0_ref, tail_len_ref, slack_ref, srt_hbm, zero_hbm, gs_hbm,
                    sem):
    def run_copy(r, off, size):
        return pltpu.make_async_copy(srt_hbm.at[pl.ds(pl.multiple_of(src0_ref[r] + off, RUN_PAD), size)],
                                     gs_hbm.at[pl.ds(pl.multiple_of(dst0_ref[r] + off, RUN_PAD), size)], sem)

    def tail_copy(g, off, size):
        return pltpu.make_async_copy(zero_hbm.at[pl.ds(0, size)],
                                     gs_hbm.at[pl.ds(pl.multiple_of(tail0_ref[g] + off, RUN_PAD), size)], sem)

    def slack_copy(j):
        size = RUN_BITS[-1]
        return pltpu.make_async_copy(zero_hbm, gs_hbm.at[pl.ds(pl.multiple_of(slack_ref[0] + j * size, size), size)], sem)

    def run_pieces(r, do):
        _for_each_piece(len_ref[r], lambda off, size: do(run_copy(r, off, size)))

    start, wait = (lambda cp: cp.start()), (lambda cp: cp.wait())
    for g in range(N_EGROUPS):
        _for_each_piece(tail_len_ref[g], lambda off, size, g=g: start(tail_copy(g, off, size)))

    def step(r, carry):
        run_pieces(r, start)
        pl.when(r >= REGROUP_LAG)(lambda: run_pieces(r - REGROUP_LAG, wait))
        return carry

    lax.fori_loop(0, n_runs, step, 0)
    lax.fori_loop(max(n_runs - REGROUP_LAG, 0), n_runs, lambda r, c: (run_pieces(r, wait), c)[1], 0)
    for g in range(N_EGROUPS):
        _for_each_piece(tail_len_ref[g], lambda off, size, g=g: wait(tail_copy(g, off, size)))
    lax.fori_loop(0, slack_ref[1], lambda j, c: (start(slack_copy(j)), c)[1], 0)
    lax.fori_loop(0, slack_ref[1], lambda j, c: (wait(slack_copy(j)), c)[1], 0)


def _regroup(srt, src0, length, dst0, tail0, tail_len, slack, n_rows):
    zero = jnp.zeros((RUN_BITS[-1], GS_COLS), F32)
    return pl.pallas_call(
        functools.partial(_regroup_kernel, src0.shape[0]),
        grid_spec=pltpu.PrefetchScalarGridSpec(
            num_scalar_prefetch=6,
            grid=(1,),
            in_specs=[pl.BlockSpec(memory_space=pl.ANY), pl.BlockSpec(memory_space=pl.ANY)],
            out_specs=pl.BlockSpec(memory_space=pl.ANY),
            scratch_shapes=[pltpu.SemaphoreType.DMA(())],
        ),
        out_shape=jax.ShapeDtypeStruct((n_rows, GS_COLS), F32),
        compiler_params=pltpu.CompilerParams(dimension_semantics=("arbitrary",), has_side_effects=True),
        name="regroup",
    )(src0, length, dst0, tail0, tail_len, slack, srt, zero)


def _ffn_kernel(blk_grp_ref, n_valid_ref, gs_ref, wg_ref, wu_ref, wd_ref, ys_ref):
    b = pl.program_id(0)

    @pl.when(b >= n_valid_ref[0])
    def _unused():
        ys_ref[...] = jnp.zeros_like(ys_ref)

    @pl.when(b < n_valid_ref[0])
    def _block():
        grp = blk_grp_ref[b]
        xs = gs_ref[:, :D_MODEL].astype(BF16)
        gates = gs_ref[:, D_MODEL:]
        lane = lax.broadcasted_iota(jnp.int32, gates.shape, 1)
        hid = []
        for j in range(N_EPG):
            gcol = jnp.sum(jnp.where(lane == grp * N_EPG + j + ROUTE_OFF, gates, 0.0), axis=-1, keepdims=True)
            hj = _silu(_dot(xs, wg_ref[j])) * _dot(xs, wu_ref[j])
            hid.append((hj * gcol).astype(BF16))
        ys_ref[...] = _dot(jnp.concatenate(hid, axis=1), wd_ref[...]).astype(BF16)


def _group_ffn(gs, blk_grp, n_valid, wg, wu, wd):
    nblk = gs.shape[0] // FFN_BLK
    live = lambda b, bg, nv: jnp.minimum(b, nv[0] - 1)
    return pl.pallas_call(
        _ffn_kernel,
        grid_spec=pltpu.PrefetchScalarGridSpec(
            num_scalar_prefetch=2,
            grid=(nblk,),
            in_specs=[
                pl.BlockSpec((FFN_BLK, GS_COLS), lambda b, bg, nv: (live(b, bg, nv), 0)),
                pl.BlockSpec((N_EPG, D_MODEL, EXPERT_FF), lambda b, bg, nv: (bg[b], 0, 0)),
                pl.BlockSpec((N_EPG, D_MODEL, EXPERT_FF), lambda b, bg, nv: (bg[b], 0, 0)),
                pl.BlockSpec((N_EPG * EXPERT_FF, D_MODEL), lambda b, bg, nv: (bg[b], 0)),
            ],
            out_specs=pl.BlockSpec((FFN_BLK, D_MODEL), lambda b, bg, nv: (b, 0)),
        ),
        out_shape=jax.ShapeDtypeStruct((nblk * FFN_BLK, D_MODEL), BF16),
        compiler_params=_cparams(1),
        name="group_ffn",
    )(blk_grp, n_valid, gs, wg, wu, wd)


def _combine_kernel(final, loc0_ref, len_ref, dst0_ref, x1_ref, m_ref, slot_ref, gfin_ref, ys_hbm, o_ref, run_s, sem):
    i = pl.program_id(0)
    tm = SORT_TM

    def fetch(tile, do):
        buf = tile % 2
        for g in range(N_EGROUPS):
            r = tile * N_EGROUPS + g

            def piece(off, size, r=r):
                do(pltpu.make_async_copy(
                    ys_hbm.at[pl.ds(pl.multiple_of(dst0_ref[r] + off, RUN_PAD), size)],
                    run_s.at[buf, pl.ds(pl.multiple_of(loc0_ref[r] + off, RUN_PAD), size)], sem.at[buf]))

            _for_each_piece(len_ref[r], piece)

    @pl.when(i == 0)
    def _first():
        fetch(i, lambda cp: cp.start())

    @pl.when(i + 1 < pl.num_programs(0))
    def _prefetch():
        fetch(i + 1, lambda cp: cp.start())

    fetch(i, lambda cp: cp.wait())

    used = loc0_ref[i * N_EGROUPS + N_EGROUPS - 1] + len_ref[i * N_EGROUPS + N_EGROUPS - 1]
    rows = lax.broadcasted_iota(jnp.int32, (SLOTS_PAD, D_MODEL), 0)
    y_run = jnp.where(rows < used, run_s[i % 2], jnp.zeros((), BF16))
    slot_b = jnp.broadcast_to(slot_ref[0], (LANES, tm)).T
    lane = lax.broadcasted_iota(jnp.int32, (tm, LANES), 1).astype(F32)
    inv = jnp.concatenate([jnp.where(slot_b == lane + float(c), 1.0, 0.0) for c in range(0, SLOTS_PAD, LANES)],
                          axis=1).astype(BF16)
    x2 = x1_ref[...] + m_ref[0, 5:6] * _dot(inv, y_run)
    o_ref[...] = _rms(x2) * gfin_ref[...] if final else x2


def _combine(final, x1, mod, mod_row, slot, gfin, ys, loc0, length, dst0):
    t = x1.shape[0]
    tm = SORT_TM
    return pl.pallas_call(
        functools.partial(_combine_kernel, final),
        grid_spec=pltpu.PrefetchScalarGridSpec(
            num_scalar_prefetch=3,
            grid=(t // tm,),
            in_specs=[
                pl.BlockSpec((tm, D_MODEL), lambda i, *_: (i, 0)),
                pl.BlockSpec((1, 6, D_MODEL), lambda i, *_: (mod_row(i), 0, 0)),
                pl.BlockSpec((1, 1, tm), lambda i, *_: (i, 0, 0)),
                pl.BlockSpec((1, D_MODEL), lambda i, *_: (0, 0)),
                pl.BlockSpec(memory_space=pl.ANY),
            ],
            out_specs=pl.BlockSpec((tm, D_MODEL), lambda i, *_: (i, 0)),
            scratch_shapes=[pltpu.VMEM((2, SLOTS_PAD, D_MODEL), BF16), pltpu.SemaphoreType.DMA((2,))],
        ),
        out_shape=jax.ShapeDtypeStruct((t, D_MODEL), F32),
        compiler_params=_cparams(1),
        name="combine",
    )(loc0, length, dst0, x1, mod, slot, gfin, ys)


def _outproj_moe(final, x, mod, mod_row, mixes, wout, g2, wr, br, wg, wu, wd, gfin):
    t = x.shape[0]
    nt = t // SORT_TM
    x1, srt, cnt, slot = _route(x, mod, mod_row, mixes, wout, g2, wr, br)
    cnt = cnt[:, 0, :N_EGROUPS].astype(jnp.int32)
    length = (cnt + RUN_PAD - 1) // RUN_PAD * RUN_PAD
    loc0 = jnp.cumsum(length, axis=1) - length
    g_rows = jnp.sum(length, axis=0)
    g_blocks = (g_rows + FFN_BLK - 1) // FFN_BLK
    g_base = (jnp.cumsum(g_blocks) - g_blocks) * FFN_BLK
    dst0 = g_base[None, :] + jnp.cumsum(length, axis=0) - length
    n_blocks = (t + nt * N_EGROUPS * (RUN_PAD - 1) + FFN_BLK - 1) // FFN_BLK + N_EGROUPS
    blk_grp = jnp.minimum(jnp.sum(jnp.arange(n_blocks)[:, None] >= jnp.cumsum(g_blocks)[None, :], axis=1),
                          N_EGROUPS - 1).astype(jnp.int32)
    n_valid = jnp.sum(g_blocks).astype(jnp.int32)[None]
    flat = lambda a: a.reshape(-1).astype(jnp.int32)
    src0 = flat(jnp.arange(nt)[:, None] * SLOTS + loc0)
    slack = jnp.stack([n_valid[0] * FFN_BLK, (n_blocks - n_valid[0]) * (FFN_BLK // RUN_BITS[-1])]).astype(jnp.int32)
    gs = _regroup(srt, src0, flat(length), flat(dst0), flat(g_base + g_rows), flat(g_blocks * FFN_BLK - g_rows),
                  slack, n_blocks * FFN_BLK)
    ys = _group_ffn(gs, blk_grp, n_valid, wg, wu, wd)
    return _combine(final, x1, mod, mod_row, slot, gfin, ys, flat(loc0), flat(length), flat(dst0))


def _block_diag(w):
    n, k, _ = w.shape
    eye = jnp.eye(n, dtype=w.dtype)
    return (eye[:, None, :, None] * w[:, :, None, :]).reshape(n * k, n * k)


def _pad_lanes(v, n=LANES):
    return jnp.pad(v, ((0, 0), (0, n - v.shape[-1])))


def kernel(x_prompt, x_sample, c, cache_k, cache_v, state_ssd, state_rglru, c_ctx, w_mod, b_mod, norm1_g, norm2_g, w_in, w_out, ssd_conv_w, ssd_conv_b, ssd_dt_bias, ssd_a_log, ssd_d, ssd_norm_g, pool_w, pool_scale, da_lam_q1, da_lam_k1, da_lam_q2, da_lam_k2, da_norm_g, rg_conv_w, rg_conv_b, rg_wa, rg_ba, rg_wx, rg_bx, rg_lambda, moe_w_group, moe_b_group, moe_w_expert, moe_b_expert, moe_w_gate, moe_w_up, moe_w_down, final_norm_g):
    nbp, lp, _ = x_prompt.shape
    nbs, ls, _ = x_sample.shape
    past = cache_k.shape[2]
    tm_in = 256
    assert nbs + 1 <= MOD_ROWS and lp % CH == 0 and ls % CH == 0
    assert lp == tm_in and ls % tm_in == 0 and lp % SORT_TM == 0 and ls % SORT_TM == 0
    assert lp % KEY_BLK == 0 and ls % KEY_BLK == 0 and past % KEY_BLK == 0

    cond = jnp.concatenate([c_ctx[None, :], c, jnp.zeros((MOD_ROWS - 1 - nbs, D_MODEL), F32)], axis=0)
    mod = _modulation(cond, w_mod, b_mod).reshape(DEPTH * MOD_ROWS, 6, D_MODEL)

    w_in_r = jnp.concatenate([w_in[:, :, :DT_LO], w_in[:, :, DT_HI:], w_in[:, :, DT_LO:DT_HI],
                              jnp.zeros((DEPTH, D_MODEL, LANES - (DT_HI - DT_LO)), F32)], axis=-1).astype(BF16)
    w_out_b = w_out.astype(BF16)
    w_gate_b = moe_w_gate.astype(BF16)
    w_up_b = moe_w_up.astype(BF16)
    w_down_b = moe_w_down.astype(BF16).reshape(DEPTH, N_EXPERTS * EXPERT_FF, D_MODEL)
    w_route = jnp.concatenate([moe_w_group, moe_w_expert,
                               jnp.zeros((DEPTH, D_MODEL, LANES - N_EGROUPS - N_EXPERTS), F32)], axis=-1)
    w_route_hi = w_route.astype(BF16)
    w_route = jnp.concatenate([w_route_hi, (w_route - w_route_hi.astype(F32)).astype(BF16)], axis=-1)
    b_route = _pad_lanes(jnp.concatenate([moe_b_group, moe_b_expert], axis=-1))
    dtb = _pad_lanes(ssd_dt_bias.reshape(DEPTH, 2 * SSD_HEADS))
    alog = _pad_lanes(ssd_a_log.reshape(DEPTH, 2 * SSD_HEADS))
    d_skip = jnp.repeat(ssd_d, SSD_HEADDIM, axis=-1)
    cos, sin = _rope_tables(ls)
    ck = cache_k.reshape(nbs, DEPTH, past, GROUP_W)
    cv = cache_v.reshape(nbs, DEPTH, past, GROUP_W)
    g_fin = final_norm_g[None, :]

    xp = x_prompt.reshape(nbp * lp, D_MODEL)
    xs = x_sample.reshape(nbs * ls, D_MODEL)
    new_k = jnp.zeros((nbp, DEPTH, lp, GROUP_W), F32)
    new_v = jnp.zeros((nbp, DEPTH, lp, GROUP_W), F32)
    ssd_out, rg_out = [], []
    for l in range(DEPTH):
        row1 = lambda a: a[l][None, :]
        final = l == DEPTH - 1
        lam_init = 0.8 - 0.6 * math.exp(-0.3 * l)
        ssd_w = (ssd_conv_w[l], row1(ssd_conv_b), row1(dtb), row1(alog), row1(d_skip), row1(ssd_norm_g))
        pool_wb = _block_diag(pool_w[l])
        att_w = (row1(da_lam_q1), row1(da_lam_k1), row1(da_lam_q2), row1(da_lam_k2), row1(da_norm_g))
        rg_w = (rg_conv_w[l], row1(rg_conv_b),
                jnp.stack([_block_diag(rg_wa[l, 0]), _block_diag(rg_wa[l, 1])]), rg_ba[l][:, None, :],
                jnp.stack([_block_diag(rg_wx[l, 0]), _block_diag(rg_wx[l, 1])]), rg_bx[l][:, None, :],
                rg_lambda[l][:, None, :])
        moe_w = (w_out_b[l], row1(norm2_g), w_route[l], row1(b_route), w_gate_b[l], w_up_b[l], w_down_b[l], g_fin)
        ctx_row = lambda i, l=l: l * MOD_ROWS
        lat_row = lambda tm: (lambda i, l=l: l * MOD_ROWS + 1 + i // (ls // tm))
        assert tm_in == SORT_TM

        xbc, z, xpool, q, new_k, new_v, xr, gr, dt = _inproj(
            xp, mod, ctx_row, row1(norm1_g), w_in_r[l], tm_in, l, caches=(new_k, new_v))
        ya, st_ssd = _ssd(False, nbp, lp, l, xbc, z, dt, *ssd_w)
        yb = _pool(nbp, lp, xpool, pool_wb, row1(pool_scale))
        yc = _attn(False, nbp, lp, l, lam_init, q, new_k, new_v, *att_w)
        yd, st_rg = _rglru(False, nbp, lp, l, xr, gr, *rg_w)
        xp = _outproj_moe(final, xp, mod, ctx_row, (ya, yb, yc, yd), *moe_w)
        ssd_out.append(st_ssd)
        rg_out.append(st_rg)

        xbc, z, xpool, q, k, v, xr, gr, dt = _inproj(xs, mod, lat_row(tm_in), row1(norm1_g), w_in_r[l], tm_in, l)
        ya = _ssd(True, nbs, ls, l, xbc, z, dt, *ssd_w, h0=state_ssd)
        yb = _pool(nbs, ls, xpool, pool_wb, row1(pool_scale))
        yc = _attn(True, nbs, ls, l, lam_init, q, k, v, *att_w, ck=ck, cv=cv, cos=cos, sin=sin)
        yd = _rglru(True, nbs, ls, l, xr, gr, *rg_w, h0=state_rglru)
        xs = _outproj_moe(final, xs, mod, lat_row(SORT_TM), (ya, yb, yc, yd), *moe_w)

    return (xp.reshape(nbp, lp, D_MODEL), xs.reshape(nbs, ls, D_MODEL),
            new_k.reshape(nbp, DEPTH, lp, DA_HEADS, 2 * DA_QKDIM), new_v.reshape(nbp, DEPTH, lp, DA_HEADS, DA_VDIM),
            jnp.stack(ssd_out, axis=1), jnp.stack(rg_out, axis=1))
```

```python
import functools
import math

import numpy as np
import jax
import jax.numpy as jnp
from jax import lax
from jax.experimental import pallas as pl
from jax.experimental.pallas import tpu as pltpu

F32 = jnp.float32
BF16 = jnp.bfloat16
HIGHEST = lax.Precision.HIGHEST

D_MODEL = 1024
DEPTH = 4
GRID_W = 64
GROUP_W = 256
EPS = 1e-6
SSD_HEADDIM = 64
SSD_HEADS = 4
SSD_STATE = 64
SSD_BC = 128
SSD_CONV_CH = 512
POOL_WINDOWS = (2, 4, 8, 16)
DA_HEADS = 4
DA_VDIM = 64
DA_QKDIM = 32
ROPE_BASE = 10000.0
RG_C = 8.0
N_EGROUPS = 4
N_EPG = 4
N_EXPERTS = 16
EXPERT_FF = 256
DT_LO, DT_HI = 768, 776

LANES = 128
SUBLANES = 8
CH = 128
HALO = SUBLANES
MOD_ROWS = 16
VMEM_LIMIT = 56 * 1024 * 1024


def _cparams(n_axes):
    return pltpu.CompilerParams(dimension_semantics=("arbitrary",) * n_axes, vmem_limit_bytes=VMEM_LIMIT)


def _silu(x):
    return x * jax.nn.sigmoid(x)


def _softplus(x):
    return jnp.maximum(x, 0.0) + jnp.log1p(jnp.exp(-jnp.abs(x)))


def _dot(a, b, **kw):
    return jnp.dot(a, b, preferred_element_type=F32, **kw)


def _dot_nt(a, b):
    return lax.dot_general(a, b, (((1,), (1,)), ((), ())), preferred_element_type=F32)


def _rms(x):
    return x * lax.rsqrt(jnp.mean(x * x, axis=-1, keepdims=True) + EPS)


def _window(ref, c, n_steps, seq_len):
    r0 = pl.multiple_of(c * CH, CH)
    main = ref[pl.ds(r0, CH), :]
    lo = pl.multiple_of(jnp.maximum(r0 - HALO, 0), HALO)
    hi = pl.multiple_of(jnp.minimum(r0 + CH, seq_len - HALO), HALO)
    prev = jnp.where(c > 0, ref[pl.ds(lo, HALO), :], 0.0)
    nxt = jnp.where(c < n_steps - 1, ref[pl.ds(hi, HALO), :], 0.0)
    return jnp.concatenate([prev, main, nxt], axis=0)


def _conv4(win, w_ref, b_ref):
    acc = b_ref[...]
    for k in range(4):
        acc = acc + w_ref[k:k + 1, :] * win[HALO - 1 + k:HALO - 1 + k + CH, :]
    return acc


def _mod_kernel(cond_ref, w_ref, b_ref, o_ref):
    cnd = cond_ref[...]
    o_ref[0] = _dot(_silu(cnd), w_ref[0], precision=HIGHEST) + b_ref[0]


def _modulation(cond, w_mod, b_mod):
    nb = 6
    return pl.pallas_call(
        _mod_kernel,
        grid=(DEPTH, nb),
        in_specs=[
            pl.BlockSpec((MOD_ROWS, D_MODEL), lambda l, j: (0, 0)),
            pl.BlockSpec((1, D_MODEL, D_MODEL), lambda l, j: (l, 0, j)),
            pl.BlockSpec((1, 1, D_MODEL), lambda l, j: (l, 0, j)),
        ],
        out_specs=pl.BlockSpec((1, MOD_ROWS, D_MODEL), lambda l, j: (l, 0, j)),
        out_shape=jax.ShapeDtypeStruct((DEPTH, MOD_ROWS, nb * D_MODEL), F32),
        compiler_params=_cparams(2),
        name="modulation",
    )(cond, w_mod, b_mod.reshape(DEPTH, 1, nb * D_MODEL))


IN_COLS = (512, 256, 256, 256, 256, 256, 256, 256, LANES)
K_OUT, V_OUT = 4, 5


def _inproj_kernel(to_cache, x_ref, m_ref, g_ref, w_ref, *refs):
    if to_cache:
        refs = refs[2:]
    hh = _rms(x_ref[...]) * g_ref[...] * (1.0 + m_ref[0, 1:2]) + m_ref[0, 0:1]
    u = _dot(hh.astype(BF16), w_ref[...])
    off = 0
    for j, (ref, n) in enumerate(zip(refs, IN_COLS)):
        if to_cache and j in (K_OUT, V_OUT):
            ref[0, 0] = u[:, off:off + n]
        else:
            ref[...] = u[:, off:off + n]
        off += n


def _inproj(x, mod, mod_row, g, w, tm, layer, caches=None):
    t = x.shape[0]
    ncol = sum(IN_COLS)
    to_cache = caches is not None
    in_specs = [
        pl.BlockSpec((tm, D_MODEL), lambda i: (i, 0)),
        pl.BlockSpec((1, 6, D_MODEL), lambda i: (mod_row(i), 0, 0)),
        pl.BlockSpec((1, D_MODEL), lambda i: (0, 0)),
        pl.BlockSpec((D_MODEL, ncol), lambda i: (0, 0)),
    ]
    out_specs = [pl.BlockSpec((tm, n), lambda i: (i, 0)) for n in IN_COLS]
    out_shape = [jax.ShapeDtypeStruct((t, n), F32) for n in IN_COLS]
    args = [x, mod, g, w]
    aliases = {}
    if to_cache:
        assert caches[0].shape[2] == tm
        cache_spec = pl.BlockSpec((1, 1, tm, GROUP_W), lambda i: (i, layer, 0, 0))
        for j, cch in zip((K_OUT, V_OUT), caches):
            in_specs.append(pl.BlockSpec(memory_space=pl.ANY))
            aliases[len(args)] = j
            args.append(cch)
            out_specs[j] = cache_spec
            out_shape[j] = jax.ShapeDtypeStruct(cch.shape, F32)
    return pl.pallas_call(
        functools.partial(_inproj_kernel, to_cache),
        grid=(t // tm,),
        in_specs=in_specs,
        out_specs=out_specs,
        out_shape=out_shape,
        input_output_aliases=aliases,
        compiler_params=_cparams(1),
        name="inproj_ctx" if to_cache else "inproj",
    )(*args)


def _ssd_kernel(has_ctx, seq_len, xbc_ref, z_ref, dt_ref, cw_ref, cb_ref, dtb_ref, alog_ref, d_ref, ng_ref, *rest):
    if has_ctx:
        h0_ref, y_ref, xc_s, y_s, st_s = rest
        st_ref = None
    else:
        y_ref, st_ref, xc_s, y_s, st_s = rest
    nc = seq_len // CH
    hd = SSD_HEADDIM
    a_row = -jnp.exp(alog_ref[...])
    row = lax.broadcasted_iota(jnp.int32, (CH, CH), 0)
    col = lax.broadcasted_iota(jnp.int32, (CH, CH), 1)
    lane_w = lax.broadcasted_iota(jnp.int32, (CH, GROUP_W), 1)
    lane_n = lax.broadcasted_iota(jnp.int32, (CH, SSD_BC), 1)
    own_block = (lax.broadcasted_iota(jnp.int32, (SSD_BC, GROUP_W), 0) // SSD_STATE
                 == lax.broadcasted_iota(jnp.int32, (SSD_BC, GROUP_W), 1) // (2 * hd))

    def conv_step(c, carry):
        r0 = pl.multiple_of(c * CH, CH)
        xc = _silu(_conv4(_window(xbc_ref, c, nc, seq_len), cw_ref, cb_ref))
        xc_s[pl.ds(r0, CH), :] = xc
        y_s[pl.ds(r0, CH), :] = xc[:, :GROUP_W] * d_ref[...]
        return carry

    lax.fori_loop(0, nc, conv_step, 0)

    st_s[...] = jnp.zeros_like(st_s)
    if has_ctx:
        for d in range(2):
            for h in range(SSD_HEADS):
                g = h // 2
                st_s[d, g * SSD_STATE:(g + 1) * SSD_STATE, h * hd:(h + 1) * hd] = h0_ref[0, 0, d, h].T

    def per_head(v, d):
        lanes = lane_w[:v.shape[0]]
        out = jnp.broadcast_to(v[:, 4 * d + 3:4 * d + 4], (v.shape[0], GROUP_W))
        for h in (2, 1, 0):
            out = jnp.where(lanes < (h + 1) * hd, jnp.broadcast_to(v[:, 4 * d + h:4 * d + h + 1], out.shape), out)
        return out

    def scan_dir(d, ci):
        r0 = pl.multiple_of(ci * CH, CH)
        xc = xc_s[pl.ds(r0, CH), :]
        x = xc[:, :GROUP_W]
        bm = xc[:, GROUP_W:GROUP_W + SSD_BC]
        cm = xc[:, GROUP_W + SSD_BC:]
        dt = _softplus(dt_ref[pl.ds(r0, CH), :] + dtb_ref[...])
        tri = (row >= col) if d == 0 else (row <= col)
        cs = _dot(tri.astype(F32), dt * a_row, precision=HIGHEST)
        cs_t = cs.T
        tot = cs[CH - 1:CH, :] if d == 0 else cs[0:1, :]
        dt_e, cs_e, tot_e = per_head(dt, d), per_head(cs, d), per_head(tot, d)
        xdt = x * dt_e
        bmb, cmb = bm.astype(BF16), cm.astype(BF16)
        scores = [_dot_nt(jnp.where((lane_n // SSD_STATE) == g, cm, 0.0).astype(BF16), bmb) for g in range(2)]
        m_parts, r_parts = [], []
        for h in range(SSD_HEADS):
            k = SSD_HEADS * d + h
            decay = jnp.exp(jnp.where(tri, cs[:, k:k + 1] - cs_t[k:k + 1, :], -jnp.inf))
            m_parts.append((scores[h // 2] * decay).astype(BF16))
            r_parts.append(jnp.where((lane_w // hd) == h, xdt, 0.0).astype(BF16))
        y_diag = _dot(jnp.concatenate(m_parts, axis=1), jnp.concatenate(r_parts, axis=0))
        st = st_s[d]
        y_off = _dot(cmb, st.astype(BF16)) * jnp.exp(cs_e)
        wgt = xdt * jnp.exp(tot_e - cs_e)
        upd = _dot(bm.T.astype(BF16), wgt.astype(BF16))
        st_s[d] = st * jnp.exp(tot_e) + jnp.where(own_block, upd, 0.0)
        y_s[pl.ds(r0, CH), :] += y_diag + y_off

    def scan_step(c, carry):
        scan_dir(0, c)
        scan_dir(1, nc - 1 - c)
        return carry

    lax.fori_loop(0, nc, scan_step, 0, unroll=2)

    def out_step(c, carry):
        r0 = pl.multiple_of(c * CH, CH)
        y = y_s[pl.ds(r0, CH), :] * _silu(z_ref[pl.ds(r0, CH), :])
        y_ref[pl.ds(r0, CH), :] = _rms(y) * ng_ref[...]
        return carry

    lax.fori_loop(0, nc, out_step, 0)
    if not has_ctx:
        for d in range(2):
            for h in range(SSD_HEADS):
                g = h // 2
                st_ref[0, d, h] = st_s[d, g * SSD_STATE:(g + 1) * SSD_STATE, h * hd:(h + 1) * hd].T


def _ssd(has_ctx, nb, seq_len, layer, xbc, z, dt, cw, cb, dtb, alog, dsk, ng, h0=None):
    rows = lambda n: pl.BlockSpec((seq_len, n), lambda b: (b, 0))
    full = lambda a: pl.BlockSpec(a.shape, lambda b: (0,) * a.ndim)
    in_specs = [rows(SSD_CONV_CH), rows(GROUP_W), rows(LANES)] + [full(a) for a in (cw, cb, dtb, alog, dsk, ng)]
    args = [xbc, z, dt, cw, cb, dtb, alog, dsk, ng]
    y_spec = rows(GROUP_W)
    y_shape = jax.ShapeDtypeStruct((nb * seq_len, GROUP_W), F32)
    st_blk = (1, 2, SSD_HEADS, SSD_HEADDIM, SSD_STATE)
    if has_ctx:
        in_specs.append(pl.BlockSpec((1, 1) + st_blk[1:], lambda b: (b, layer, 0, 0, 0, 0)))
        args.append(h0)
        out_specs, out_shape = y_spec, y_shape
    else:
        out_specs = [y_spec, pl.BlockSpec(st_blk, lambda b: (b, 0, 0, 0, 0))]
        out_shape = [y_shape, jax.ShapeDtypeStruct((nb,) + st_blk[1:], F32)]
    return pl.pallas_call(
        functools.partial(_ssd_kernel, has_ctx, seq_len),
        grid=(nb,),
        in_specs=in_specs,
        out_specs=out_specs,
        out_shape=out_shape,
        scratch_shapes=[
            pltpu.VMEM((seq_len, SSD_CONV_CH), F32),
            pltpu.VMEM((seq_len, GROUP_W), F32),
            pltpu.VMEM((2, SSD_BC, GROUP_W), F32),
        ],
        compiler_params=_cparams(1),
        name="ssd_ctx" if has_ctx else "ssd",
    )(*args)


def _pool_kernel(seq_len, x_ref, w_ref, sc_ref, y_ref):
    nc = seq_len // CH
    wn = CH + 2 * HALO
    lane = lax.broadcasted_iota(jnp.int32, (CH, GROUP_W), 1)
    gw = GROUP_W // len(POOL_WINDOWS)
    half = jnp.where(lane < gw, 1, jnp.where(lane < 2 * gw, 2, jnp.where(lane < 3 * gw, 4, 8)))
    w_blk = w_ref[...].astype(BF16)

    def ahead(v, k):
        return pltpu.roll(v, wn - k, axis=0)

    def step(c, carry):
        r0 = pl.multiple_of(c * CH, CH)
        win = _window(x_ref, c, nc, seq_len)
        p2 = win + ahead(win, 1)
        p4 = p2 + ahead(p2, 2)
        p8 = p4 + ahead(p4, 4)
        p16 = p8 + ahead(p8, 8)
        s2 = ahead(p2, HALO - 1)[:CH]
        s4 = ahead(p4, HALO - 2)[:CH]
        s8 = ahead(p8, HALO - 4)[:CH]
        s16 = p16[:CH]
        tot = jnp.where(lane < gw, s2, jnp.where(lane < 2 * gw, s4, jnp.where(lane < 3 * gw, s8, s16)))
        t = r0 + lax.broadcasted_iota(jnp.int32, (CH, GROUP_W), 0)
        cnt = jnp.minimum(t + half, seq_len) - jnp.maximum(t - half, 0)
        x = win[HALO:HALO + CH]
        diff = tot / cnt.astype(F32) - x
        y_ref[pl.ds(r0, CH), :] = _dot(diff.astype(BF16), w_blk) * sc_ref[...]
        return carry

    lax.fori_loop(0, nc, step, 0)


def _pool(nb, seq_len, x, w_blk, scale):
    return pl.pallas_call(
        functools.partial(_pool_kernel, seq_len),
        grid=(nb,),
        in_specs=[
            pl.BlockSpec((seq_len, GROUP_W), lambda b: (b, 0)),
            pl.BlockSpec((GROUP_W, GROUP_W), lambda b: (0, 0)),
            pl.BlockSpec((1, GROUP_W), lambda b: (0, 0)),
        ],
        out_specs=pl.BlockSpec((seq_len, GROUP_W), lambda b: (b, 0)),
        out_shape=jax.ShapeDtypeStruct((nb * seq_len, GROUP_W), F32),
        compiler_params=_cparams(1),
        name="pool",
    )(x, w_blk, scale)


KEY_BLK = 256


def _rope(x, cos, sin):
    c2 = jnp.concatenate([cos, cos], axis=1)
    s2 = jnp.concatenate([sin, sin], axis=1)
    lane = lax.broadcasted_iota(jnp.int32, x.shape, 1)
    n = x.shape[1]
    partner = jnp.where(lane % 2 == 0, pltpu.roll(x, n - 1, axis=1), pltpu.roll(x, 1, axis=1))
    return x * c2 + partner * s2


def _attn_kernel(has_ctx, seq_len, tq, past, lam_init, q_ref, k_ref, v_ref, lq1, lk1, lq2, lk2, ng_ref, *rest):
    if has_ctx:
        ck_ref, cv_ref, cosq_ref, sinq_ref, cosk_ref, sink_ref, o_ref, kt_s, v_s = rest
    else:
        o_ref, kt_s, v_s = rest
    kb = KEY_BLK

    @pl.when(pl.program_id(1) == 0)
    def _prepare_keys():
        def put(dst0, kk, vv):
            kt_s[:, dst0:dst0 + kb] = kk.T.astype(BF16)
            for h in range(DA_HEADS):
                v_s[h, dst0:dst0 + kb, :] = vv[:, h * DA_VDIM:(h + 1) * DA_VDIM].astype(BF16)

        if has_ctx:
            for j in range(past // kb):
                put(j * kb, ck_ref[0, 0, j * kb:(j + 1) * kb, :], cv_ref[0, 0, j * kb:(j + 1) * kb, :])
        for j in range(seq_len // kb):
            kk = k_ref[0, 0, j * kb:(j + 1) * kb, :] if not has_ctx else k_ref[j * kb:(j + 1) * kb, :]
            vv = v_ref[0, 0, j * kb:(j + 1) * kb, :] if not has_ctx else v_ref[j * kb:(j + 1) * kb, :]
            if has_ctx:
                kk = _rope(kk, cosk_ref[j * kb:(j + 1) * kb, :], sink_ref[j * kb:(j + 1) * kb, :])
            put(past + j * kb, kk, vv)

    q = q_ref[...]
    if has_ctx:
        q = _rope(q, cosq_ref[...], sinq_ref[...])
    q = q * (DA_QKDIM ** -0.5 * math.log2(math.e))
    lam = (jnp.exp(jnp.sum(lq1[...] * lk1[...], axis=-1, keepdims=True))
           - jnp.exp(jnp.sum(lq2[...] * lk2[...], axis=-1, keepdims=True)) + lam_init)
    for h in range(DA_HEADS):
        acc = None
        for m in range(2):
            lo = h * 2 * DA_QKDIM + m * DA_QKDIM
            s = _dot(q[:, lo:lo + DA_QKDIM].astype(BF16), kt_s[lo:lo + DA_QKDIM, :])
            e = jnp.exp2(s - jnp.max(s, axis=-1, keepdims=True))
            o = _dot(e.astype(BF16), v_s[h]) / jnp.sum(e, axis=-1, keepdims=True)
            acc = o if m == 0 else acc - lam * o
        o_ref[:, h * DA_VDIM:(h + 1) * DA_VDIM] = _rms(acc) * ng_ref[...] * (1.0 - lam_init)


def _attn(has_ctx, nb, seq_len, layer, lam_init, q, k, v, lq1, lk1, lq2, lk2, ng, ck=None, cv=None, cos=None, sin=None):
    tq = 128
    nq = seq_len // tq
    past = ck.shape[2] if has_ctx else 0
    keys = seq_len + past
    small = lambda a: pl.BlockSpec(a.shape, lambda b, i: (0,) * a.ndim)
    if has_ctx:
        kv_spec = pl.BlockSpec((seq_len, GROUP_W), lambda b, i: (b, 0))
    else:
        kv_spec = pl.BlockSpec((1, 1, seq_len, GROUP_W), lambda b, i: (b, layer, 0, 0))
    in_specs = [pl.BlockSpec((tq, GROUP_W), lambda b, i: (b * nq + i, 0)), kv_spec, kv_spec]
    in_specs += [small(a) for a in (lq1, lk1, lq2, lk2, ng)]
    args = [q, k, v, lq1, lk1, lq2, lk2, ng]
    if has_ctx:
        in_specs += [
            pl.BlockSpec((1, 1, past, GROUP_W), lambda b, i: (b, layer, 0, 0)),
            pl.BlockSpec((1, 1, past, GROUP_W), lambda b, i: (b, layer, 0, 0)),
            pl.BlockSpec((tq, LANES), lambda b, i: (i, 0)),
            pl.BlockSpec((tq, LANES), lambda b, i: (i, 0)),
            pl.BlockSpec((seq_len, LANES), lambda b, i: (0, 0)),
            pl.BlockSpec((seq_len, LANES), lambda b, i: (0, 0)),
        ]
        args += [ck, cv, cos, sin, cos, sin]
    return pl.pallas_call(
        functools.partial(_attn_kernel, has_ctx, seq_len, tq, past, lam_init),
        grid=(nb, nq),
        in_specs=in_specs,
        out_specs=pl.BlockSpec((tq, GROUP_W), lambda b, i: (b * nq + i, 0)),
        out_shape=jax.ShapeDtypeStruct((nb * seq_len, GROUP_W), F32),
        scratch_shapes=[
            pltpu.VMEM((GROUP_W, keys), BF16),
            pltpu.VMEM((DA_HEADS, keys, DA_VDIM), BF16),
        ],
        compiler_params=_cparams(2),
        name="attn_ctx" if has_ctx else "attn",
    )(*args)


def _rope_tables(seq_len):
    t = np.arange(seq_len)
    rowp = (t // GRID_W).astype(np.float64)
    colp = (t % GRID_W).astype(np.float64)
    n_freq = DA_QKDIM // 4
    inv_freq = ROPE_BASE ** (-np.arange(n_freq, dtype=np.float64) / n_freq)
    ang = np.concatenate([rowp[:, None] * inv_freq, colp[:, None] * inv_freq], axis=-1)
    ang = np.repeat(ang, 2, axis=-1)
    sign = np.where(np.arange(DA_QKDIM) % 2 == 0, -1.0, 1.0)
    cos = np.tile(np.cos(ang), (1, LANES // DA_QKDIM)).astype(np.float32)
    sin = np.tile(np.sin(ang) * sign, (1, LANES // DA_QKDIM)).astype(np.float32)
    return jnp.asarray(cos), jnp.asarray(sin)


def _rglru_kernel(has_ctx, seq_len, x_ref, g_ref, cw_ref, cb_ref, wa_ref, ba_ref, wx_ref, bx_ref, lam_ref, *rest):
    if has_ctx:
        h0_ref, y_ref, a_s, u_s = rest
        st_ref = None
    else:
        y_ref, st_ref, a_s, u_s = rest
    nc = seq_len // CH
    nt = CH // SUBLANES
    sub = lax.broadcasted_iota(jnp.int32, (nt, SUBLANES, GROUP_W), 1)

    def gate_step(c, carry):
        r0 = pl.multiple_of(c * CH, CH)
        xc = _conv4(_window(x_ref, c, nc, seq_len), cw_ref, cb_ref)
        xb = xc.astype(BF16)
        for d in range(2):
            rg = jax.nn.sigmoid(_dot(xb, wa_ref[d].astype(BF16)) + ba_ref[d])
            ig = jax.nn.sigmoid(_dot(xb, wx_ref[d].astype(BF16)) + bx_ref[d])
            log_a = -RG_C * rg * _softplus(-lam_ref[d])
            a = jnp.exp(log_a)
            u = jnp.sqrt(-jnp.tanh(log_a) * (a * a + 1.0)) * (ig * xc)
            a3 = a.reshape(nt, SUBLANES, GROUP_W)
            u3 = u.reshape(nt, SUBLANES, GROUP_W)
            for k in (1, 2, 4):
                if d == 0:
                    ok = sub >= k
                    a_sh, u_sh = pltpu.roll(a3, k, axis=1), pltpu.roll(u3, k, axis=1)
                else:
                    ok = sub < SUBLANES - k
                    a_sh, u_sh = pltpu.roll(a3, SUBLANES - k, axis=1), pltpu.roll(u3, SUBLANES - k, axis=1)
                u3 = u3 + a3 * jnp.where(ok, u_sh, 0.0)
                a3 = a3 * jnp.where(ok, a_sh, 1.0)
            a_s[d, pl.ds(r0, CH), :] = a3.reshape(CH, GROUP_W)
            u_s[d, pl.ds(r0, CH), :] = u3.reshape(CH, GROUP_W)
        return carry

    lax.fori_loop(0, nc, gate_step, 0)

    n_tiles = seq_len // SUBLANES
    if has_ctx:
        hf0, hb0 = h0_ref[0, 0, 0:1, :], h0_ref[0, 0, 1:2, :]
    else:
        hf0 = hb0 = jnp.zeros((1, GROUP_W), F32)

    def carry_step(i, carry):
        hf, hb = carry
        rf = pl.multiple_of(i * SUBLANES, SUBLANES)
        rb = pl.multiple_of((n_tiles - 1 - i) * SUBLANES, SUBLANES)
        tf = u_s[0, pl.ds(rf, SUBLANES), :] + a_s[0, pl.ds(rf, SUBLANES), :] * hf
        tb = u_s[1, pl.ds(rb, SUBLANES), :] + a_s[1, pl.ds(rb, SUBLANES), :] * hb
        u_s[0, pl.ds(rf, SUBLANES), :] = tf
        u_s[1, pl.ds(rb, SUBLANES), :] = tb
        return tf[SUBLANES - 1:SUBLANES, :], tb[0:1, :]

    hf, hb = lax.fori_loop(0, n_tiles, carry_step, (hf0, hb0), unroll=4)
    if not has_ctx:
        st_ref[0, 0:1, :] = hf
        st_ref[0, 1:2, :] = hb

    def out_step(c, carry):
        r0 = pl.multiple_of(c * CH, CH)
        g = g_ref[pl.ds(r0, CH), :]
        gelu = g * (0.5 * (1.0 + jnp.tanh(math.sqrt(2.0 / math.pi) * (g + 0.044715 * (g * g * g)))))
        y_ref[pl.ds(r0, CH), :] = (u_s[0, pl.ds(r0, CH), :] + u_s[1, pl.ds(r0, CH), :]) * gelu
        return carry

    lax.fori_loop(0, nc, out_step, 0)


def _rglru(has_ctx, nb, seq_len, layer, x, g, cw, cb, wa, ba, wx, bx, lam, h0=None):
    rows = pl.BlockSpec((seq_len, GROUP_W), lambda b: (b, 0))
    full = lambda a: pl.BlockSpec(a.shape, lambda b: (0,) * a.ndim)
    in_specs = [rows, rows] + [full(a) for a in (cw, cb, wa, ba, wx, bx, lam)]
    args = [x, g, cw, cb, wa, ba, wx, bx, lam]
    y_shape = jax.ShapeDtypeStruct((nb * seq_len, GROUP_W), F32)
    if has_ctx:
        in_specs.append(pl.BlockSpec((1, 1, 2, GROUP_W), lambda b: (b, layer, 0, 0)))
        args.append(h0)
        out_specs, out_shape = rows, y_shape
    else:
        out_specs = [rows, pl.BlockSpec((1, 2, GROUP_W), lambda b: (b, 0, 0))]
        out_shape = [y_shape, jax.ShapeDtypeStruct((nb, 2, GROUP_W), F32)]
    return pl.pallas_call(
        functools.partial(_rglru_kernel, has_ctx, seq_len),
        grid=(nb,),
        in_specs=in_specs,
        out_specs=out_specs,
        out_shape=out_shape,
        scratch_shapes=[pltpu.VMEM((2, seq_len, GROUP_W), F32), pltpu.VMEM((2, seq_len, GROUP_W), F32)],
        compiler_params=_cparams(1),
        name="rglru_ctx" if has_ctx else "rglru",
    )(*args)


ROUTE_OFF = N_EGROUPS
SORT_TM = 256
RUN_PAD = 16
SLOTS = SORT_TM + N_EGROUPS * RUN_PAD
SLOTS_PAD = 384
RUN_BITS = (16, 32, 64, 128, 256)
FFN_BLK = 512
GS_COLS = D_MODEL + LANES


def _routing_gate(logits):
    lane = lax.broadcasted_iota(jnp.int32, logits.shape, 1)
    neg = -jnp.inf
    big = LANES
    gl = jnp.where(lane < N_EGROUPS, logits, neg)
    gmax = jnp.max(gl, axis=-1, keepdims=True)
    g_w = 1.0 / jnp.sum(jnp.exp(gl - gmax), axis=-1, keepdims=True)
    g_sel = jnp.min(jnp.where(gl == gmax, lane, big), axis=-1, keepdims=True)
    e_lane = lane - ROUTE_OFF
    in_grp = (e_lane >= 0) & (e_lane < N_EXPERTS) & ((e_lane // N_EPG) == g_sel)
    el = jnp.where(in_grp, logits, neg)
    m1 = jnp.max(el, axis=-1, keepdims=True)
    i1 = jnp.min(jnp.where(el == m1, lane, big), axis=-1, keepdims=True)
    el2 = jnp.where(lane == i1, neg, el)
    m2 = jnp.max(el2, axis=-1, keepdims=True)
    i2 = jnp.min(jnp.where(el2 == m2, lane, big), axis=-1, keepdims=True)
    r = jnp.exp(m2 - m1)
    p1 = 1.0 / (1.0 + r)
    p2 = r / (1.0 + r)
    return jnp.where(lane == i1, g_w * p1, jnp.where(lane == i2, g_w * p2, 0.0)), g_sel


def _route_kernel(x_ref, m_ref, mix0, mix1, mix2, mix3, wout_ref, g2_ref, wr_ref, br_ref,
                  x1_ref, h2_ref, gate_ref, cnt_ref, slot_ref):
    tm = SORT_TM
    mix = jnp.concatenate([mix0[...], mix1[...], mix2[...], mix3[...]], axis=1).astype(BF16)
    x1 = x_ref[...] + m_ref[0, 2:3] * _dot(mix, wout_ref[...])
    x1_ref[...] = x1
    h2 = _rms(x1) * g2_ref[...] * (1.0 + m_ref[0, 4:5]) + m_ref[0, 3:4]
    h2_hi = h2.astype(BF16)
    h2_ref[...] = h2_hi
    h2_lo = (h2 - h2_hi.astype(F32)).astype(BF16)
    hi = _dot(h2_hi, wr_ref[...])
    logits = hi[:, :LANES] + hi[:, LANES:] + _dot(h2_lo, wr_ref[:, :LANES]) + br_ref[...]
    gate, g_sel = _routing_gate(logits)
    gate_ref[...] = gate

    lane = lax.broadcasted_iota(jnp.int32, (tm, LANES), 1)
    onehot = jnp.where(lane == g_sel, 1.0, 0.0)
    cnt_ref[0] = jnp.sum(onehot, axis=0, keepdims=True)
    onehot_t = onehot.T
    earlier = jnp.where(lax.broadcasted_iota(jnp.int32, (tm, tm), 0) < lax.broadcasted_iota(jnp.int32, (tm, tm), 1),
                        1.0, 0.0).astype(BF16)
    rank_t = _dot(onehot_t.astype(BF16), earlier)
    cnt = jnp.sum(onehot_t, axis=1, keepdims=True)
    padded = jnp.ceil(cnt * (1.0 / RUN_PAD)) * RUN_PAD
    grp = lax.broadcasted_iota(jnp.int32, (LANES, 1), 0)
    start = jnp.zeros((LANES, 1), F32)
    for g in range(N_EGROUPS - 1):
        start = start + jnp.where(grp > g, padded[g:g + 1, :], 0.0)
    slot = jnp.sum(onehot_t * (rank_t + start), axis=0, keepdims=True)
    slot_ref[0] = slot


def _route(x, mod, mod_row, mixes, wout, g2, wr, br):
    t = x.shape[0]
    tm = SORT_TM
    nt = t // tm
    tok = lambda n: pl.BlockSpec((tm, n), lambda i: (i, 0))
    const = lambda a: pl.BlockSpec(a.shape, lambda i: (0,) * a.ndim)
    return pl.pallas_call(
        _route_kernel,
        grid=(nt,),
        in_specs=[tok(D_MODEL), pl.BlockSpec((1, 6, D_MODEL), lambda i: (mod_row(i), 0, 0)),
                  tok(GROUP_W), tok(GROUP_W), tok(GROUP_W), tok(GROUP_W),
                  const(wout), const(g2), const(wr), const(br)],
        out_specs=[tok(D_MODEL), tok(D_MODEL), tok(LANES),
                   pl.BlockSpec((1, 1, LANES), lambda i: (i, 0, 0)), pl.BlockSpec((1, 1, tm), lambda i: (i, 0, 0))],
        out_shape=[jax.ShapeDtypeStruct((t, D_MODEL), F32), jax.ShapeDtypeStruct((t, D_MODEL), BF16),
                   jax.ShapeDtypeStruct((t, LANES), F32),
                   jax.ShapeDtypeStruct((nt, 1, LANES), F32), jax.ShapeDtypeStruct((nt, 1, tm), F32)],
        compiler_params=_cparams(1),
        name="route",
    )(x, mod, *mixes, wout, g2, wr, br)


def _for_each_piece(n_rows, fn):
    off = 0
    for bit in RUN_BITS:
        has = (n_rows & bit) != 0
        pl.when(has)(functools.partial(fn, off, bit))
        off = off + jnp.where(has, bit, 0)


def _regroup_kernel(loc0_ref, len_ref, dst0_ref, tail0_ref, tail_len_ref, slack_ref, h2_ref, gate_ref, slot_ref,
                    gs_hbm, srt_s, zero_s, sem, zsem):
    i = pl.program_id(0)
    n = pl.num_programs(0)
    tm = SORT_TM
    start, wait = (lambda cp: cp.start()), (lambda cp: cp.wait())

    def zero_pieces(do):
        def tail_copy(g, off, size):
            return pltpu.make_async_copy(zero_s.at[pl.ds(0, size)],
                                         gs_hbm.at[pl.ds(pl.multiple_of(tail0_ref[g] + off, RUN_PAD), size)], zsem)

        def slack_copy(j):
            size = RUN_BITS[-1]
            return pltpu.make_async_copy(zero_s, gs_hbm.at[pl.ds(pl.multiple_of(slack_ref[0] + j * size, size), size)], zsem)

        for g in range(N_EGROUPS):
            _for_each_piece(tail_len_ref[g], lambda off, size, g=g: do(tail_copy(g, off, size)))
        lax.fori_loop(0, slack_ref[1], lambda j, c: (do(slack_copy(j)), c)[1], 0)

    def run_pieces(tile, do):
        buf = tile % 2
        for g in range(N_EGROUPS):
            r = tile * N_EGROUPS + g

            def piece(off, size, r=r):
                do(pltpu.make_async_copy(
                    srt_s.at[buf, pl.ds(pl.multiple_of(loc0_ref[r] + off, RUN_PAD), size)],
                    gs_hbm.at[pl.ds(pl.multiple_of(dst0_ref[r] + off, RUN_PAD), size)], sem.at[buf]))

            _for_each_piece(len_ref[r], piece)

    @pl.when(i == 0)
    def _zeros():
        zero_s[...] = jnp.zeros_like(zero_s)
        zero_pieces(start)

    @pl.when(i >= 2)
    def _reuse():
        run_pieces(i - 2, wait)

    perm = lax.broadcasted_iota(jnp.int32, (SLOTS, tm), 0).astype(F32) == slot_ref[0]
    srt_h = _dot(jnp.where(perm, 1.0, 0.0).astype(BF16), h2_ref[...])
    srt_g = _dot(jnp.where(perm, 1.0, 0.0), gate_ref[...], precision=HIGHEST)
    srt_s[i % 2] = jnp.concatenate([srt_h, srt_g], axis=1)
    run_pieces(i, start)

    @pl.when(i == n - 1)
    def _drain():
        pl.when(i >= 1)(lambda: run_pieces(i - 1, wait))
        run_pieces(i, wait)
        zero_pieces(wait)


def _regroup(h2, gate, slot, loc0, length, dst0, tail0, tail_len, slack, n_rows):
    t = h2.shape[0]
    tm = SORT_TM
    return pl.pallas_call(
        _regroup_kernel,
        grid_spec=pltpu.PrefetchScalarGridSpec(
            num_scalar_prefetch=6,
            grid=(t // tm,),
            in_specs=[pl.BlockSpec((tm, D_MODEL), lambda i, *_: (i, 0)),
                      pl.BlockSpec((tm, LANES), lambda i, *_: (i, 0)),
                      pl.BlockSpec((1, 1, tm), lambda i, *_: (i, 0, 0))],
            out_specs=pl.BlockSpec(memory_space=pl.ANY),
            scratch_shapes=[pltpu.VMEM((2, SLOTS, GS_COLS), F32), pltpu.VMEM((RUN_BITS[-1], GS_COLS), F32),
                            pltpu.SemaphoreType.DMA((2,)), pltpu.SemaphoreType.DMA(())],
        ),
        out_shape=jax.ShapeDtypeStruct((n_rows, GS_COLS), F32),
        compiler_params=_cparams(1),
        name="regroup",
    )(loc0, length, dst0, tail0, tail_len, slack, h2, gate, slot)


def _ffn_kernel(blk_grp_ref, n_valid_ref, gs_ref, wg_ref, wu_ref, wd_ref, ys_ref):
    b = pl.program_id(0)

    @pl.when(b >= n_valid_ref[0])
    def _unused():
        ys_ref[...] = jnp.zeros_like(ys_ref)

    @pl.when(b < n_valid_ref[0])
    def _block():
        grp = blk_grp_ref[b]
        xs = gs_ref[:, :D_MODEL].astype(BF16)
        gates = gs_ref[:, D_MODEL:]
        lane = lax.broadcasted_iota(jnp.int32, gates.shape, 1)
        hid = []
        for j in range(N_EPG):
            gcol = jnp.sum(jnp.where(lane == grp * N_EPG + j + ROUTE_OFF, gates, 0.0), axis=-1, keepdims=True)
            hj = _silu(_dot(xs, wg_ref[j])) * _dot(xs, wu_ref[j])
            hid.append((hj * gcol).astype(BF16))
        ys_ref[...] = _dot(jnp.concatenate(hid, axis=1), wd_ref[...]).astype(BF16)


def _group_ffn(gs, blk_grp, n_valid, wg, wu, wd):
    nblk = gs.shape[0] // FFN_BLK
    live = lambda b, bg, nv: jnp.minimum(b, nv[0] - 1)
    return pl.pallas_call(
        _ffn_kernel,
        grid_spec=pltpu.PrefetchScalarGridSpec(
            num_scalar_prefetch=2,
            grid=(nblk,),
            in_specs=[
                pl.BlockSpec((FFN_BLK, GS_COLS), lambda b, bg, nv: (live(b, bg, nv), 0)),
                pl.BlockSpec((N_EPG, D_MODEL, EXPERT_FF), lambda b, bg, nv: (bg[b], 0, 0)),
                pl.BlockSpec((N_EPG, D_MODEL, EXPERT_FF), lambda b, bg, nv: (bg[b], 0, 0)),
                pl.BlockSpec((N_EPG * EXPERT_FF, D_MODEL), lambda b, bg, nv: (bg[b], 0)),
            ],
            out_specs=pl.BlockSpec((FFN_BLK, D_MODEL), lambda b, bg, nv: (b, 0)),
        ),
        out_shape=jax.ShapeDtypeStruct((nblk * FFN_BLK, D_MODEL), BF16),
        compiler_params=_cparams(1),
        name="group_ffn",
    )(blk_grp, n_valid, gs, wg, wu, wd)


def _combine_kernel(final, loc0_ref, len_ref, dst0_ref, x1_ref, m_ref, slot_ref, gfin_ref, ys_hbm, o_ref, run_s, sem):
    i = pl.program_id(0)
    tm = SORT_TM

    def fetch(tile, do):
        buf = tile % 2
        for g in range(N_EGROUPS):
            r = tile * N_EGROUPS + g

            def piece(off, size, r=r):
                do(pltpu.make_async_copy(
                    ys_hbm.at[pl.ds(pl.multiple_of(dst0_ref[r] + off, RUN_PAD), size)],
                    run_s.at[buf, pl.ds(pl.multiple_of(loc0_ref[r] + off, RUN_PAD), size)], sem.at[buf]))

            _for_each_piece(len_ref[r], piece)

    @pl.when(i == 0)
    def _first():
        fetch(i, lambda cp: cp.start())

    @pl.when(i + 1 < pl.num_programs(0))
    def _prefetch():
        fetch(i + 1, lambda cp: cp.start())

    fetch(i, lambda cp: cp.wait())

    used = loc0_ref[i * N_EGROUPS + N_EGROUPS - 1] + len_ref[i * N_EGROUPS + N_EGROUPS - 1]
    rows = lax.broadcasted_iota(jnp.int32, (SLOTS_PAD, D_MODEL), 0)
    y_run = jnp.where(rows < used, run_s[i % 2], jnp.zeros((), BF16))
    slot_b = jnp.broadcast_to(slot_ref[0], (LANES, tm)).T
    lane = lax.broadcasted_iota(jnp.int32, (tm, LANES), 1).astype(F32)
    inv = jnp.concatenate([jnp.where(slot_b == lane + float(c), 1.0, 0.0) for c in range(0, SLOTS_PAD, LANES)],
                          axis=1).astype(BF16)
    x2 = x1_ref[...] + m_ref[0, 5:6] * _dot(inv, y_run)
    o_ref[...] = _rms(x2) * gfin_ref[...] if final else x2


def _combine(final, x1, mod, mod_row, slot, gfin, ys, loc0, length, dst0):
    t = x1.shape[0]
    tm = SORT_TM
    return pl.pallas_call(
        functools.partial(_combine_kernel, final),
        grid_spec=pltpu.PrefetchScalarGridSpec(
            num_scalar_prefetch=3,
            grid=(t // tm,),
            in_specs=[
                pl.BlockSpec((tm, D_MODEL), lambda i, *_: (i, 0)),
                pl.BlockSpec((1, 6, D_MODEL), lambda i, *_: (mod_row(i), 0, 0)),
                pl.BlockSpec((1, 1, tm), lambda i, *_: (i, 0, 0)),
                pl.BlockSpec((1, D_MODEL), lambda i, *_: (0, 0)),
                pl.BlockSpec(memory_space=pl.ANY),
            ],
            out_specs=pl.BlockSpec((tm, D_MODEL), lambda i, *_: (i, 0)),
            scratch_shapes=[pltpu.VMEM((2, SLOTS_PAD, D_MODEL), BF16), pltpu.SemaphoreType.DMA((2,))],
        ),
        out_shape=jax.ShapeDtypeStruct((t, D_MODEL), F32),
        compiler_params=_cparams(1),
        name="combine",
    )(loc0, length, dst0, x1, mod, slot, gfin, ys)


def _outproj_moe(final, x, mod, mod_row, mixes, wout, g2, wr, br, wg, wu, wd, gfin):
    t = x.shape[0]
    nt = t // SORT_TM
    x1, h2, gate, cnt, slot = _route(x, mod, mod_row, mixes, wout, g2, wr, br)
    cnt = cnt[:, 0, :N_EGROUPS].astype(jnp.int32)
    length = (cnt + RUN_PAD - 1) // RUN_PAD * RUN_PAD
    loc0 = jnp.cumsum(length, axis=1) - length
    g_rows = jnp.sum(length, axis=0)
    g_blocks = (g_rows + FFN_BLK - 1) // FFN_BLK
    g_base = (jnp.cumsum(g_blocks) - g_blocks) * FFN_BLK
    dst0 = g_base[None, :] + jnp.cumsum(length, axis=0) - length
    n_blocks = (t + nt * N_EGROUPS * (RUN_PAD - 1) + FFN_BLK - 1) // FFN_BLK + N_EGROUPS
    blk_grp = jnp.minimum(jnp.sum(jnp.arange(n_blocks)[:, None] >= jnp.cumsum(g_blocks)[None, :], axis=1),
                          N_EGROUPS - 1).astype(jnp.int32)
    n_valid = jnp.sum(g_blocks).astype(jnp.int32)[None]
    flat = lambda a: a.reshape(-1).astype(jnp.int32)
    slack = jnp.stack([n_valid[0] * FFN_BLK, (n_blocks - n_valid[0]) * (FFN_BLK // RUN_BITS[-1])]).astype(jnp.int32)
    gs = _regroup(h2, gate, slot, flat(loc0), flat(length), flat(dst0), flat(g_base + g_rows),
                  flat(g_blocks * FFN_BLK - g_rows), slack, n_blocks * FFN_BLK)
    ys = _group_ffn(gs, blk_grp, n_valid, wg, wu, wd)
    return _combine(final, x1, mod, mod_row, slot, gfin, ys, flat(loc0), flat(length), flat(dst0))


def _block_diag(w):
    n, k, _ = w.shape
    eye = jnp.eye(n, dtype=w.dtype)
    return (eye[:, None, :, None] * w[:, :, None, :]).reshape(n * k, n * k)


def _pad_lanes(v, n=LANES):
    return jnp.pad(v, ((0, 0), (0, n - v.shape[-1])))


def kernel(x_prompt, x_sample, c, cache_k, cache_v, state_ssd, state_rglru, c_ctx, w_mod, b_mod, norm1_g, norm2_g, w_in, w_out, ssd_conv_w, ssd_conv_b, ssd_dt_bias, ssd_a_log, ssd_d, ssd_norm_g, pool_w, pool_scale, da_lam_q1, da_lam_k1, da_lam_q2, da_lam_k2, da_norm_g, rg_conv_w, rg_conv_b, rg_wa, rg_ba, rg_wx, rg_bx, rg_lambda, moe_w_group, moe_b_group, moe_w_expert, moe_b_expert, moe_w_gate, moe_w_up, moe_w_down, final_norm_g):
    nbp, lp, _ = x_prompt.shape
    nbs, ls, _ = x_sample.shape
    past = cache_k.shape[2]
    tm_in = 256
    assert nbs + 1 <= MOD_ROWS and lp % CH == 0 and ls % CH == 0
    assert lp == tm_in and ls % tm_in == 0 and lp % SORT_TM == 0 and ls % SORT_TM == 0
    assert lp % KEY_BLK == 0 and ls % KEY_BLK == 0 and past % KEY_BLK == 0

    cond = jnp.concatenate([c_ctx[None, :], c, jnp.zeros((MOD_ROWS - 1 - nbs, D_MODEL), F32)], axis=0)
    mod = _modulation(cond, w_mod, b_mod).reshape(DEPTH * MOD_ROWS, 6, D_MODEL)

    w_in_r = jnp.concatenate([w_in[:, :, :DT_LO], w_in[:, :, DT_HI:], w_in[:, :, DT_LO:DT_HI],
                              jnp.zeros((DEPTH, D_MODEL, LANES - (DT_HI - DT_LO)), F32)], axis=-1).astype(BF16)
    w_out_b = w_out.astype(BF16)
    w_gate_b = moe_w_gate.astype(BF16)
    w_up_b = moe_w_up.astype(BF16)
    w_down_b = moe_w_down.astype(BF16).reshape(DEPTH, N_EXPERTS * EXPERT_FF, D_MODEL)
    w_route = jnp.concatenate([moe_w_group, moe_w_expert,
                               jnp.zeros((DEPTH, D_MODEL, LANES - N_EGROUPS - N_EXPERTS), F32)], axis=-1)
    w_route_hi = w_route.astype(BF16)
    w_route = jnp.concatenate([w_route_hi, (w_route - w_route_hi.astype(F32)).astype(BF16)], axis=-1)
    b_route = _pad_lanes(jnp.concatenate([moe_b_group, moe_b_expert], axis=-1))
    dtb = _pad_lanes(ssd_dt_bias.reshape(DEPTH, 2 * SSD_HEADS))
    alog = _pad_lanes(ssd_a_log.reshape(DEPTH, 2 * SSD_HEADS))
    d_skip = jnp.repeat(ssd_d, SSD_HEADDIM, axis=-1)
    cos, sin = _rope_tables(ls)
    ck = cache_k.reshape(nbs, DEPTH, past, GROUP_W)
    cv = cache_v.reshape(nbs, DEPTH, past, GROUP_W)
    g_fin = final_norm_g[None, :]

    xp = x_prompt.reshape(nbp * lp, D_MODEL)
    xs = x_sample.reshape(nbs * ls, D_MODEL)
    new_k = jnp.zeros((nbp, DEPTH, lp, GROUP_W), F32)
    new_v = jnp.zeros((nbp, DEPTH, lp, GROUP_W), F32)
    ssd_out, rg_out = [], []
    for l in range(DEPTH):
        row1 = lambda a: a[l][None, :]
        final = l == DEPTH - 1
        lam_init = 0.8 - 0.6 * math.exp(-0.3 * l)
        ssd_w = (ssd_conv_w[l], row1(ssd_conv_b), row1(dtb), row1(alog), row1(d_skip), row1(ssd_norm_g))
        pool_wb = _block_diag(pool_w[l])
        att_w = (row1(da_lam_q1), row1(da_lam_k1), row1(da_lam_q2), row1(da_lam_k2), row1(da_norm_g))
        rg_w = (rg_conv_w[l], row1(rg_conv_b),
                jnp.stack([_block_diag(rg_wa[l, 0]), _block_diag(rg_wa[l, 1])]), rg_ba[l][:, None, :],
                jnp.stack([_block_diag(rg_wx[l, 0]), _block_diag(rg_wx[l, 1])]), rg_bx[l][:, None, :],
                rg_lambda[l][:, None, :])
        moe_w = (w_out_b[l], row1(norm2_g), w_route[l], row1(b_route), w_gate_b[l], w_up_b[l], w_down_b[l], g_fin)
        ctx_row = lambda i, l=l: l * MOD_ROWS
        lat_row = lambda tm: (lambda i, l=l: l * MOD_ROWS + 1 + i // (ls // tm))
        assert tm_in == SORT_TM

        xbc, z, xpool, q, new_k, new_v, xr, gr, dt = _inproj(
            xp, mod, ctx_row, row1(norm1_g), w_in_r[l], tm_in, l, caches=(new_k, new_v))
        ya, st_ssd = _ssd(False, nbp, lp, l, xbc, z, dt, *ssd_w)
        yb = _pool(nbp, lp, xpool, pool_wb, row1(pool_scale))
        yc = _attn(False, nbp, lp, l, lam_init, q, new_k, new_v, *att_w)
        yd, st_rg = _rglru(False, nbp, lp, l, xr, gr, *rg_w)
        xp = _outproj_moe(final, xp, mod, ctx_row, (ya, yb, yc, yd), *moe_w)
        ssd_out.append(st_ssd)
        rg_out.append(st_rg)

        xbc, z, xpool, q, k, v, xr, gr, dt = _inproj(xs, mod, lat_row(tm_in), row1(norm1_g), w_in_r[l], tm_in, l)
        ya = _ssd(True, nbs, ls, l, xbc, z, dt, *ssd_w, h0=state_ssd)
        yb = _pool(nbs, ls, xpool, pool_wb, row1(pool_scale))
        yc = _attn(True, nbs, ls, l, lam_init, q, k, v, *att_w, ck=ck, cv=cv, cos=cos, sin=sin)
        yd = _rglru(True, nbs, ls, l, xr, gr, *rg_w, h0=state_rglru)
        xs = _outproj_moe(final, xs, mod, lat_row(SORT_TM), (ya, yb, yc, yd), *moe_w)

    return (xp.reshape(nbp, lp, D_MODEL), xs.reshape(nbs, ls, D_MODEL),
            new_k.reshape(nbp, DEPTH, lp, DA_HEADS, 2 * DA_QKDIM), new_v.reshape(nbp, DEPTH, lp, DA_HEADS, DA_VDIM),
            jnp.stack(ssd_out, axis=1), jnp.stack(rg_out, axis=1))
```

```python
import functools
import math

import numpy as np
import jax
import jax.numpy as jnp
from jax import lax
from jax.experimental import pallas as pl
from jax.experimental.pallas import tpu as pltpu

F32 = jnp.float32
BF16 = jnp.bfloat16
HIGHEST = lax.Precision.HIGHEST

D_MODEL = 1024
DEPTH = 4
GRID_W = 64
GROUP_W = 256
EPS = 1e-6
SSD_HEADDIM = 64
SSD_HEADS = 4
SSD_STATE = 64
SSD_BC = 128
SSD_CONV_CH = 512
POOL_WINDOWS = (2, 4, 8, 16)
DA_HEADS = 4
DA_VDIM = 64
DA_QKDIM = 32
ROPE_BASE = 10000.0
RG_C = 8.0
N_EGROUPS = 4
N_EPG = 4
N_EXPERTS = 16
EXPERT_FF = 256
DT_LO, DT_HI = 768, 776

LANES = 128
SUBLANES = 8
CH = 128
HALO = SUBLANES
MOD_ROWS = 16
VMEM_LIMIT = 56 * 1024 * 1024


def _cparams(n_axes):
    return pltpu.CompilerParams(dimension_semantics=("arbitrary",) * n_axes, vmem_limit_bytes=VMEM_LIMIT)


def _silu(x):
    return x * jax.nn.sigmoid(x)


def _softplus(x):
    return jnp.maximum(x, 0.0) + jnp.log1p(jnp.exp(-jnp.abs(x)))


def _dot(a, b, **kw):
    return jnp.dot(a, b, preferred_element_type=F32, **kw)


def _dot_nt(a, b):
    return lax.dot_general(a, b, (((1,), (1,)), ((), ())), preferred_element_type=F32)


def _rms(x):
    return x * lax.rsqrt(jnp.mean(x * x, axis=-1, keepdims=True) + EPS)


def _window(ref, c, n_steps, seq_len):
    r0 = pl.multiple_of(c * CH, CH)
    main = ref[pl.ds(r0, CH), :]
    lo = pl.multiple_of(jnp.maximum(r0 - HALO, 0), HALO)
    hi = pl.multiple_of(jnp.minimum(r0 + CH, seq_len - HALO), HALO)
    prev = jnp.where(c > 0, ref[pl.ds(lo, HALO), :], 0.0)
    nxt = jnp.where(c < n_steps - 1, ref[pl.ds(hi, HALO), :], 0.0)
    return jnp.concatenate([prev, main, nxt], axis=0)


def _conv4(win, w_ref, b_ref):
    acc = b_ref[...]
    for k in range(4):
        acc = acc + w_ref[k:k + 1, :] * win[HALO - 1 + k:HALO - 1 + k + CH, :]
    return acc


def _mod_kernel(cond_ref, w_ref, b_ref, o_ref):
    cnd = cond_ref[...]
    o_ref[0] = _dot(_silu(cnd), w_ref[0], precision=HIGHEST) + b_ref[0]


def _modulation(cond, w_mod, b_mod):
    nb = 6
    return pl.pallas_call(
        _mod_kernel,
        grid=(DEPTH, nb),
        in_specs=[
            pl.BlockSpec((MOD_ROWS, D_MODEL), lambda l, j: (0, 0)),
            pl.BlockSpec((1, D_MODEL, D_MODEL), lambda l, j: (l, 0, j)),
            pl.BlockSpec((1, 1, D_MODEL), lambda l, j: (l, 0, j)),
        ],
        out_specs=pl.BlockSpec((1, MOD_ROWS, D_MODEL), lambda l, j: (l, 0, j)),
        out_shape=jax.ShapeDtypeStruct((DEPTH, MOD_ROWS, nb * D_MODEL), F32),
        compiler_params=_cparams(2),
        name="modulation",
    )(cond, w_mod, b_mod.reshape(DEPTH, 1, nb * D_MODEL))


IN_COLS = (512, 256, 256, 256, 256, 256, 256, 256, LANES)
K_OUT, V_OUT = 4, 5


def _inproj_kernel(to_cache, x_ref, m_ref, g_ref, w_ref, *refs):
    if to_cache:
        refs = refs[2:]
    hh = _rms(x_ref[...]) * g_ref[...] * (1.0 + m_ref[0, 1:2]) + m_ref[0, 0:1]
    u = _dot(hh.astype(BF16), w_ref[...])
    off = 0
    for j, (ref, n) in enumerate(zip(refs, IN_COLS)):
        if to_cache and j in (K_OUT, V_OUT):
            ref[0, 0] = u[:, off:off + n]
        else:
            ref[...] = u[:, off:off + n]
        off += n


def _inproj(x, mod, mod_row, g, w, tm, layer, caches=None):
    t = x.shape[0]
    ncol = sum(IN_COLS)
    to_cache = caches is not None
    in_specs = [
        pl.BlockSpec((tm, D_MODEL), lambda i: (i, 0)),
        pl.BlockSpec((1, 6, D_MODEL), lambda i: (mod_row(i), 0, 0)),
        pl.BlockSpec((1, D_MODEL), lambda i: (0, 0)),
        pl.BlockSpec((D_MODEL, ncol), lambda i: (0, 0)),
    ]
    out_specs = [pl.BlockSpec((tm, n), lambda i: (i, 0)) for n in IN_COLS]
    out_shape = [jax.ShapeDtypeStruct((t, n), F32) for n in IN_COLS]
    args = [x, mod, g, w]
    aliases = {}
    if to_cache:
        assert caches[0].shape[2] == tm
        cache_spec = pl.BlockSpec((1, 1, tm, GROUP_W), lambda i: (i, layer, 0, 0))
        for j, cch in zip((K_OUT, V_OUT), caches):
            in_specs.append(pl.BlockSpec(memory_space=pl.ANY))
            aliases[len(args)] = j
            args.append(cch)
            out_specs[j] = cache_spec
            out_shape[j] = jax.ShapeDtypeStruct(cch.shape, F32)
    return pl.pallas_call(
        functools.partial(_inproj_kernel, to_cache),
        grid=(t // tm,),
        in_specs=in_specs,
        out_specs=out_specs,
        out_shape=out_shape,
        input_output_aliases=aliases,
        compiler_params=_cparams(1),
        name="inproj_ctx" if to_cache else "inproj",
    )(*args)


def _ssd_kernel(has_ctx, seq_len, xbc_ref, z_ref, dt_ref, cw_ref, cb_ref, dtb_ref, alog_ref, d_ref, ng_ref, *rest):
    if has_ctx:
        h0_ref, y_ref, xc_s, y_s, st_s, upd_s, grow_s, keep_s = rest
        st_ref = None
    else:
        y_ref, st_ref, xc_s, y_s, st_s, upd_s, grow_s, keep_s = rest
    nc = seq_len // CH
    hd = SSD_HEADDIM
    a_row = -jnp.exp(alog_ref[...])
    row = lax.broadcasted_iota(jnp.int32, (CH, CH), 0)
    col = lax.broadcasted_iota(jnp.int32, (CH, CH), 1)
    lane_w = lax.broadcasted_iota(jnp.int32, (CH, GROUP_W), 1)
    lane_n = lax.broadcasted_iota(jnp.int32, (CH, SSD_BC), 1)
    own_block = (lax.broadcasted_iota(jnp.int32, (SSD_BC, GROUP_W), 0) // SSD_STATE
                 == lax.broadcasted_iota(jnp.int32, (SSD_BC, GROUP_W), 1) // (2 * hd))

    def conv_step(c, carry):
        r0 = pl.multiple_of(c * CH, CH)
        xc = _silu(_conv4(_window(xbc_ref, c, nc, seq_len), cw_ref, cb_ref))
        xc_s[pl.ds(r0, CH), :] = xc
        y_s[pl.ds(r0, CH), :] = xc[:, :GROUP_W] * d_ref[...]
        return carry

    lax.fori_loop(0, nc, conv_step, 0)

    st_s[...] = jnp.zeros_like(st_s)
    if has_ctx:
        for d in range(2):
            for h in range(SSD_HEADS):
                g = h // 2
                st_s[d, g * SSD_STATE:(g + 1) * SSD_STATE, h * hd:(h + 1) * hd] = h0_ref[0, 0, d, h].T

    def per_head(v, d):
        lanes = lane_w[:v.shape[0]]
        out = jnp.broadcast_to(v[:, 4 * d + 3:4 * d + 4], (v.shape[0], GROUP_W))
        for h in (2, 1, 0):
            out = jnp.where(lanes < (h + 1) * hd, jnp.broadcast_to(v[:, 4 * d + h:4 * d + h + 1], out.shape), out)
        return out

    def local_step(c, carry):
        r0 = pl.multiple_of(c * CH, CH)
        xc = xc_s[pl.ds(r0, CH), :]
        x = xc[:, :GROUP_W]
        bm = xc[:, GROUP_W:GROUP_W + SSD_BC]
        cm = xc[:, GROUP_W + SSD_BC:]
        dt = _softplus(dt_ref[pl.ds(r0, CH), :] + dtb_ref[...])
        bmb = bm.astype(BF16)
        bm_t = bm.T.astype(BF16)
        scores = [_dot_nt(jnp.where((lane_n // SSD_STATE) == g, cm, 0.0).astype(BF16), bmb) for g in range(2)]
        y_diag = None
        da = dt * a_row
        pre = da
        for k in (1, 2, 4, 8, 16, 32, 64):
            pre = pre + jnp.where(row >= k, pltpu.roll(pre, k, axis=0), 0.0)
        for d in range(2):
            tri = (row >= col) if d == 0 else (row <= col)
            cs = pre if d == 0 else pre[CH - 1:CH, :] - pre + da
            cs_t = cs.T
            tot = cs[CH - 1:CH, :] if d == 0 else cs[0:1, :]
            dt_e, cs_e, tot_e = per_head(dt, d), per_head(cs, d), per_head(tot, d)
            xdt = x * dt_e
            m_parts, r_parts = [], []
            for h in range(SSD_HEADS):
                k = SSD_HEADS * d + h
                decay = jnp.exp(jnp.where(tri, cs[:, k:k + 1] - cs_t[k:k + 1, :], -jnp.inf))
                m_parts.append((scores[h // 2] * decay).astype(BF16))
                r_parts.append(jnp.where((lane_w // hd) == h, xdt, 0.0).astype(BF16))
            yd = _dot(jnp.concatenate(m_parts, axis=1), jnp.concatenate(r_parts, axis=0))
            y_diag = yd if y_diag is None else y_diag + yd
            wgt = xdt * jnp.exp(tot_e - cs_e)
            upd_s[d, c] = jnp.where(own_block, _dot(bm_t, wgt.astype(BF16)), 0.0)
            grow_s[d, pl.ds(r0, CH), :] = jnp.exp(cs_e)
            keep_s[d, pl.ds(c, 1), :] = jnp.exp(tot_e)
        y_s[pl.ds(r0, CH), :] += y_diag
        return carry

    lax.fori_loop(0, nc, local_step, 0, unroll=2)

    def state_dir(d, ci):
        r0 = pl.multiple_of(ci * CH, CH)
        st = st_s[d]
        cmb = xc_s[pl.ds(r0, CH), GROUP_W + SSD_BC:].astype(BF16)
        y_s[pl.ds(r0, CH), :] += _dot(cmb, st.astype(BF16)) * grow_s[d, pl.ds(r0, CH), :]
        st_s[d] = st * keep_s[d, pl.ds(ci, 1), :] + upd_s[d, ci]

    def scan_step(c, carry):
        state_dir(0, c)
        state_dir(1, nc - 1 - c)
        return carry

    lax.fori_loop(0, nc, scan_step, 0, unroll=2)

    def out_step(c, carry):
        r0 = pl.multiple_of(c * CH, CH)
        y = y_s[pl.ds(r0, CH), :] * _silu(z_ref[pl.ds(r0, CH), :])
        y_ref[pl.ds(r0, CH), :] = _rms(y) * ng_ref[...]
        return carry

    lax.fori_loop(0, nc, out_step, 0)
    if not has_ctx:
        for d in range(2):
            for h in range(SSD_HEADS):
                g = h // 2
                st_ref[0, d, h] = st_s[d, g * SSD_STATE:(g + 1) * SSD_STATE, h * hd:(h + 1) * hd].T


def _ssd(has_ctx, nb, seq_len, layer, xbc, z, dt, cw, cb, dtb, alog, dsk, ng, h0=None):
    rows = lambda n: pl.BlockSpec((seq_len, n), lambda b: (b, 0))
    full = lambda a: pl.BlockSpec(a.shape, lambda b: (0,) * a.ndim)
    in_specs = [rows(SSD_CONV_CH), rows(GROUP_W), rows(LANES)] + [full(a) for a in (cw, cb, dtb, alog, dsk, ng)]
    args = [xbc, z, dt, cw, cb, dtb, alog, dsk, ng]
    y_spec = rows(GROUP_W)
    y_shape = jax.ShapeDtypeStruct((nb * seq_len, GROUP_W), F32)
    st_blk = (1, 2, SSD_HEADS, SSD_HEADDIM, SSD_STATE)
    if has_ctx:
        in_specs.append(pl.BlockSpec((1, 1) + st_blk[1:], lambda b: (b, layer, 0, 0, 0, 0)))
        args.append(h0)
        out_specs, out_shape = y_spec, y_shape
    else:
        out_specs = [y_spec, pl.BlockSpec(st_blk, lambda b: (b, 0, 0, 0, 0))]
        out_shape = [y_shape, jax.ShapeDtypeStruct((nb,) + st_blk[1:], F32)]
    return pl.pallas_call(
        functools.partial(_ssd_kernel, has_ctx, seq_len),
        grid=(nb,),
        in_specs=in_specs,
        out_specs=out_specs,
        out_shape=out_shape,
        scratch_shapes=[
            pltpu.VMEM((seq_len, SSD_CONV_CH), F32),
            pltpu.VMEM((seq_len, GROUP_W), F32),
            pltpu.VMEM((2, SSD_BC, GROUP_W), F32),
            pltpu.VMEM((2, seq_len // CH, SSD_BC, GROUP_W), F32),
            pltpu.VMEM((2, seq_len, GROUP_W), F32),
            pltpu.VMEM((2, max(seq_len // CH, SUBLANES), GROUP_W), F32),
        ],
        compiler_params=_cparams(1),
        name="ssd_ctx" if has_ctx else "ssd",
    )(*args)


def _pool_kernel(seq_len, x_ref, w_ref, sc_ref, y_ref):
    nc = seq_len // CH
    wn = CH + 2 * HALO
    lane = lax.broadcasted_iota(jnp.int32, (CH, GROUP_W), 1)
    gw = GROUP_W // len(POOL_WINDOWS)
    half = jnp.where(lane < gw, 1, jnp.where(lane < 2 * gw, 2, jnp.where(lane < 3 * gw, 4, 8)))
    w_blk = w_ref[...].astype(BF16)

    def ahead(v, k):
        return pltpu.roll(v, wn - k, axis=0)

    def step(c, carry):
        r0 = pl.multiple_of(c * CH, CH)
        win = _window(x_ref, c, nc, seq_len)
        p2 = win + ahead(win, 1)
        p4 = p2 + ahead(p2, 2)
        p8 = p4 + ahead(p4, 4)
        p16 = p8 + ahead(p8, 8)
        s2 = ahead(p2, HALO - 1)[:CH]
        s4 = ahead(p4, HALO - 2)[:CH]
        s8 = ahead(p8, HALO - 4)[:CH]
        s16 = p16[:CH]
        tot = jnp.where(lane < gw, s2, jnp.where(lane < 2 * gw, s4, jnp.where(lane < 3 * gw, s8, s16)))
        t = r0 + lax.broadcasted_iota(jnp.int32, (CH, GROUP_W), 0)
        cnt = jnp.minimum(t + half, seq_len) - jnp.maximum(t - half, 0)
        x = win[HALO:HALO + CH]
        diff = tot / cnt.astype(F32) - x
        y_ref[pl.ds(r0, CH), :] = _dot(diff.astype(BF16), w_blk) * sc_ref[...]
        return carry

    lax.fori_loop(0, nc, step, 0)


def _pool(nb, seq_len, x, w_blk, scale):
    return pl.pallas_call(
        functools.partial(_pool_kernel, seq_len),
        grid=(nb,),
        in_specs=[
            pl.BlockSpec((seq_len, GROUP_W), lambda b: (b, 0)),
            pl.BlockSpec((GROUP_W, GROUP_W), lambda b: (0, 0)),
            pl.BlockSpec((1, GROUP_W), lambda b: (0, 0)),
        ],
        out_specs=pl.BlockSpec((seq_len, GROUP_W), lambda b: (b, 0)),
        out_shape=jax.ShapeDtypeStruct((nb * seq_len, GROUP_W), F32),
        compiler_params=_cparams(1),
        name="pool",
    )(x, w_blk, scale)


KEY_BLK = 256


def _rope(x, cos, sin):
    c2 = jnp.concatenate([cos, cos], axis=1)
    s2 = jnp.concatenate([sin, sin], axis=1)
    lane = lax.broadcasted_iota(jnp.int32, x.shape, 1)
    n = x.shape[1]
    partner = jnp.where(lane % 2 == 0, pltpu.roll(x, n - 1, axis=1), pltpu.roll(x, 1, axis=1))
    return x * c2 + partner * s2


def _attn_kernel(has_ctx, seq_len, tq, past, lam_init, q_ref, k_ref, v_ref, lq1, lk1, lq2, lk2, ng_ref, *rest):
    if has_ctx:
        ck_ref, cv_ref, cosq_ref, sinq_ref, cosk_ref, sink_ref, o_ref, kt_s, v_s = rest
    else:
        o_ref, kt_s, v_s = rest
    kb = KEY_BLK

    @pl.when(pl.program_id(1) == 0)
    def _prepare_keys():
        def put(dst0, kk, vv):
            kt_s[:, dst0:dst0 + kb] = kk.T.astype(BF16)
            for h in range(DA_HEADS):
                v_s[h, dst0:dst0 + kb, :] = vv[:, h * DA_VDIM:(h + 1) * DA_VDIM].astype(BF16)

        if has_ctx:
            for j in range(past // kb):
                put(j * kb, ck_ref[0, 0, j * kb:(j + 1) * kb, :], cv_ref[0, 0, j * kb:(j + 1) * kb, :])
        for j in range(seq_len // kb):
            kk = k_ref[0, 0, j * kb:(j + 1) * kb, :] if not has_ctx else k_ref[j * kb:(j + 1) * kb, :]
            vv = v_ref[0, 0, j * kb:(j + 1) * kb, :] if not has_ctx else v_ref[j * kb:(j + 1) * kb, :]
            if has_ctx:
                kk = _rope(kk, cosk_ref[j * kb:(j + 1) * kb, :], sink_ref[j * kb:(j + 1) * kb, :])
            put(past + j * kb, kk, vv)

    q = q_ref[...]
    if has_ctx:
        q = _rope(q, cosq_ref[...], sinq_ref[...])
    q = q * (DA_QKDIM ** -0.5 * math.log2(math.e))
    lam = (jnp.exp(jnp.sum(lq1[...] * lk1[...], axis=-1, keepdims=True))
           - jnp.exp(jnp.sum(lq2[...] * lk2[...], axis=-1, keepdims=True)) + lam_init)
    for h in range(DA_HEADS):
        acc = None
        for m in range(2):
            lo = h * 2 * DA_QKDIM + m * DA_QKDIM
            s = _dot(q[:, lo:lo + DA_QKDIM].astype(BF16), kt_s[lo:lo + DA_QKDIM, :])
            e = jnp.exp2(s - jnp.max(s, axis=-1, keepdims=True))
            o = _dot(e.astype(BF16), v_s[h]) / jnp.sum(e, axis=-1, keepdims=True)
            acc = o if m == 0 else acc - lam * o
        o_ref[:, h * DA_VDIM:(h + 1) * DA_VDIM] = _rms(acc) * ng_ref[...] * (1.0 - lam_init)


def _attn(has_ctx, nb, seq_len, layer, lam_init, q, k, v, lq1, lk1, lq2, lk2, ng, ck=None, cv=None, cos=None, sin=None):
    tq = 128
    nq = seq_len // tq
    past = ck.shape[2] if has_ctx else 0
    keys = seq_len + past
    small = lambda a: pl.BlockSpec(a.shape, lambda b, i: (0,) * a.ndim)
    if has_ctx:
        kv_spec = pl.BlockSpec((seq_len, GROUP_W), lambda b, i: (b, 0))
    else:
        kv_spec = pl.BlockSpec((1, 1, seq_len, GROUP_W), lambda b, i: (b, layer, 0, 0))
    in_specs = [pl.BlockSpec((tq, GROUP_W), lambda b, i: (b * nq + i, 0)), kv_spec, kv_spec]
    in_specs += [small(a) for a in (lq1, lk1, lq2, lk2, ng)]
    args = [q, k, v, lq1, lk1, lq2, lk2, ng]
    if has_ctx:
        in_specs += [
            pl.BlockSpec((1, 1, past, GROUP_W), lambda b, i: (b, layer, 0, 0)),
            pl.BlockSpec((1, 1, past, GROUP_W), lambda b, i: (b, layer, 0, 0)),
            pl.BlockSpec((tq, LANES), lambda b, i: (i, 0)),
            pl.BlockSpec((tq, LANES), lambda b, i: (i, 0)),
            pl.BlockSpec((seq_len, LANES), lambda b, i: (0, 0)),
            pl.BlockSpec((seq_len, LANES), lambda b, i: (0, 0)),
        ]
        args += [ck, cv, cos, sin, cos, sin]
    return pl.pallas_call(
        functools.partial(_attn_kernel, has_ctx, seq_len, tq, past, lam_init),
        grid=(nb, nq),
        in_specs=in_specs,
        out_specs=pl.BlockSpec((tq, GROUP_W), lambda b, i: (b * nq + i, 0)),
        out_shape=jax.ShapeDtypeStruct((nb * seq_len, GROUP_W), F32),
        scratch_shapes=[
            pltpu.VMEM((GROUP_W, keys), BF16),
            pltpu.VMEM((DA_HEADS, keys, DA_VDIM), BF16),
        ],
        compiler_params=_cparams(2),
        name="attn_ctx" if has_ctx else "attn",
    )(*args)


def _rope_tables(seq_len):
    t = np.arange(seq_len)
    rowp = (t // GRID_W).astype(np.float64)
    colp = (t % GRID_W).astype(np.float64)
    n_freq = DA_QKDIM // 4
    inv_freq = ROPE_BASE ** (-np.arange(n_freq, dtype=np.float64) / n_freq)
    ang = np.concatenate([rowp[:, None] * inv_freq, colp[:, None] * inv_freq], axis=-1)
    ang = np.repeat(ang, 2, axis=-1)
    sign = np.where(np.arange(DA_QKDIM) % 2 == 0, -1.0, 1.0)
    cos = np.tile(np.cos(ang), (1, LANES // DA_QKDIM)).astype(np.float32)
    sin = np.tile(np.sin(ang) * sign, (1, LANES // DA_QKDIM)).astype(np.float32)
    return jnp.asarray(cos), jnp.asarray(sin)


def _rglru_kernel(has_ctx, seq_len, x_ref, g_ref, cw_ref, cb_ref, wa_ref, ba_ref, wx_ref, bx_ref, lam_ref, *rest):
    if has_ctx:
        h0_ref, y_ref, a_s, u_s = rest
        st_ref = None
    else:
        y_ref, st_ref, a_s, u_s = rest
    nc = seq_len // CH
    nt = CH // SUBLANES
    sub = lax.broadcasted_iota(jnp.int32, (nt, SUBLANES, GROUP_W), 1)

    def gate_step(c, carry):
        r0 = pl.multiple_of(c * CH, CH)
        xc = _conv4(_window(x_ref, c, nc, seq_len), cw_ref, cb_ref)
        xb = xc.astype(BF16)
        for d in range(2):
            rg = jax.nn.sigmoid(_dot(xb, wa_ref[d].astype(BF16)) + ba_ref[d])
            ig = jax.nn.sigmoid(_dot(xb, wx_ref[d].astype(BF16)) + bx_ref[d])
            log_a = -RG_C * rg * _softplus(-lam_ref[d])
            a = jnp.exp(log_a)
            u = jnp.sqrt(-jnp.tanh(log_a) * (a * a + 1.0)) * (ig * xc)
            a3 = a.reshape(nt, SUBLANES, GROUP_W)
            u3 = u.reshape(nt, SUBLANES, GROUP_W)
            for k in (1, 2, 4):
                if d == 0:
                    ok = sub >= k
                    a_sh, u_sh = pltpu.roll(a3, k, axis=1), pltpu.roll(u3, k, axis=1)
                else:
                    ok = sub < SUBLANES - k
                    a_sh, u_sh = pltpu.roll(a3, SUBLANES - k, axis=1), pltpu.roll(u3, SUBLANES - k, axis=1)
                u3 = u3 + a3 * jnp.where(ok, u_sh, 0.0)
                a3 = a3 * jnp.where(ok, a_sh, 1.0)
            a_s[d, pl.ds(r0, CH), :] = a3.reshape(CH, GROUP_W)
            u_s[d, pl.ds(r0, CH), :] = u3.reshape(CH, GROUP_W)
        return carry

    lax.fori_loop(0, nc, gate_step, 0)

    n_tiles = seq_len // SUBLANES
    if has_ctx:
        hf0, hb0 = h0_ref[0, 0, 0:1, :], h0_ref[0, 0, 1:2, :]
    else:
        hf0 = hb0 = jnp.zeros((1, GROUP_W), F32)

    def carry_step(i, carry):
        hf, hb = carry
        rf = pl.multiple_of(i * SUBLANES, SUBLANES)
        rb = pl.multiple_of((n_tiles - 1 - i) * SUBLANES, SUBLANES)
        tf = u_s[0, pl.ds(rf, SUBLANES), :] + a_s[0, pl.ds(rf, SUBLANES), :] * hf
        tb = u_s[1, pl.ds(rb, SUBLANES), :] + a_s[1, pl.ds(rb, SUBLANES), :] * hb
        u_s[0, pl.ds(rf, SUBLANES), :] = tf
        u_s[1, pl.ds(rb, SUBLANES), :] = tb
        return tf[SUBLANES - 1:SUBLANES, :], tb[0:1, :]

    hf, hb = lax.fori_loop(0, n_tiles, carry_step, (hf0, hb0), unroll=4)
    if not has_ctx:
        st_ref[0, 0:1, :] = hf
        st_ref[0, 1:2, :] = hb

    def out_step(c, carry):
        r0 = pl.multiple_of(c * CH, CH)
        g = g_ref[pl.ds(r0, CH), :]
        gelu = g * (0.5 * (1.0 + jnp.tanh(math.sqrt(2.0 / math.pi) * (g + 0.044715 * (g * g * g)))))
        y_ref[pl.ds(r0, CH), :] = (u_s[0, pl.ds(r0, CH), :] + u_s[1, pl.ds(r0, CH), :]) * gelu
        return carry

    lax.fori_loop(0, nc, out_step, 0)


def _rglru(has_ctx, nb, seq_len, layer, x, g, cw, cb, wa, ba, wx, bx, lam, h0=None):
    rows = pl.BlockSpec((seq_len, GROUP_W), lambda b: (b, 0))
    full = lambda a: pl.BlockSpec(a.shape, lambda b: (0,) * a.ndim)
    in_specs = [rows, rows] + [full(a) for a in (cw, cb, wa, ba, wx, bx, lam)]
    args = [x, g, cw, cb, wa, ba, wx, bx, lam]
    y_shape = jax.ShapeDtypeStruct((nb * seq_len, GROUP_W), F32)
    if has_ctx:
        in_specs.append(pl.BlockSpec((1, 1, 2, GROUP_W), lambda b: (b, layer, 0, 0)))
        args.append(h0)
        out_specs, out_shape = rows, y_shape
    else:
        out_specs = [rows, pl.BlockSpec((1, 2, GROUP_W), lambda b: (b, 0, 0))]
        out_shape = [y_shape, jax.ShapeDtypeStruct((nb, 2, GROUP_W), F32)]
    return pl.pallas_call(
        functools.partial(_rglru_kernel, has_ctx, seq_len),
        grid=(nb,),
        in_specs=in_specs,
        out_specs=out_specs,
        out_shape=out_shape,
        scratch_shapes=[pltpu.VMEM((2, seq_len, GROUP_W), F32), pltpu.VMEM((2, seq_len, GROUP_W), F32)],
        compiler_params=_cparams(1),
        name="rglru_ctx" if has_ctx else "rglru",
    )(*args)


ROUTE_OFF = N_EGROUPS
SORT_TM = 256
RUN_PAD = 16
SLOTS = SORT_TM + N_EGROUPS * RUN_PAD
SLOTS_PAD = 384
RUN_BITS = (16, 32, 64, 128, 256)
FFN_BLK = 512
GS_COLS = D_MODEL + LANES


def _routing_gate(logits):
    lane = lax.broadcasted_iota(jnp.int32, logits.shape, 1)
    lane_f = lane.astype(F32)
    neg = -jnp.inf
    big = float(LANES)
    gl = jnp.where(lane < N_EGROUPS, logits, neg)
    gmax = jnp.max(gl, axis=-1, keepdims=True)
    g_w = 1.0 / jnp.sum(jnp.exp(gl - gmax), axis=-1, keepdims=True)
    g_sel = jnp.min(jnp.where(gl == gmax, lane_f, big), axis=-1, keepdims=True)
    e_lane = lane - ROUTE_OFF
    in_grp = (e_lane >= 0) & (e_lane < N_EXPERTS) & ((e_lane // N_EPG).astype(F32) == g_sel)
    el = jnp.where(in_grp, logits, neg)
    m1 = jnp.max(el, axis=-1, keepdims=True)
    i1 = jnp.min(jnp.where(el == m1, lane_f, big), axis=-1, keepdims=True)
    el2 = jnp.where(lane_f == i1, neg, el)
    m2 = jnp.max(el2, axis=-1, keepdims=True)
    i2 = jnp.min(jnp.where(el2 == m2, lane_f, big), axis=-1, keepdims=True)
    r = jnp.exp(m2 - m1)
    p1 = 1.0 / (1.0 + r)
    p2 = r / (1.0 + r)
    return jnp.where(lane_f == i1, g_w * p1, jnp.where(lane_f == i2, g_w * p2, 0.0)), g_sel


def _route_kernel(x_ref, m_ref, mix0, mix1, mix2, mix3, wout_ref, g2_ref, wr_ref, br_ref,
                  x1_ref, h2_ref, gate_ref, cnt_ref, slot_ref):
    tm = SORT_TM
    mix = jnp.concatenate([mix0[...], mix1[...], mix2[...], mix3[...]], axis=1).astype(BF16)
    x1 = x_ref[...] + m_ref[0, 2:3] * _dot(mix, wout_ref[...])
    x1_ref[...] = x1
    h2 = _rms(x1) * g2_ref[...] * (1.0 + m_ref[0, 4:5]) + m_ref[0, 3:4]
    h2_hi = h2.astype(BF16)
    h2_ref[...] = h2_hi
    h2_lo = (h2 - h2_hi.astype(F32)).astype(BF16)
    hi = _dot(h2_hi, wr_ref[...])
    logits = hi[:, :LANES] + hi[:, LANES:] + _dot(h2_lo, wr_ref[:, :LANES]) + br_ref[...]
    gate, g_sel = _routing_gate(logits)
    gate_ref[...] = gate

    lane = lax.broadcasted_iota(jnp.int32, (tm, LANES), 1).astype(F32)
    onehot = jnp.where(lane == g_sel, 1.0, 0.0)
    cnt_ref[0] = jnp.sum(onehot, axis=0, keepdims=True)
    onehot_t = onehot.T
    earlier = jnp.where(lax.broadcasted_iota(jnp.int32, (tm, tm), 0) < lax.broadcasted_iota(jnp.int32, (tm, tm), 1),
                        1.0, 0.0).astype(BF16)
    rank_t = _dot(onehot_t.astype(BF16), earlier)
    cnt = jnp.sum(onehot_t, axis=1, keepdims=True)
    padded = jnp.ceil(cnt * (1.0 / RUN_PAD)) * RUN_PAD
    grp = lax.broadcasted_iota(jnp.int32, (LANES, 1), 0)
    start = jnp.zeros((LANES, 1), F32)
    for g in range(N_EGROUPS - 1):
        start = start + jnp.where(grp > g, padded[g:g + 1, :], 0.0)
    slot = jnp.sum(onehot_t * (rank_t + start), axis=0, keepdims=True)
    slot_ref[0] = slot


def _route(x, mod, mod_row, mixes, wout, g2, wr, br):
    t = x.shape[0]
    tm = SORT_TM
    nt = t // tm
    tok = lambda n: pl.BlockSpec((tm, n), lambda i: (i, 0))
    const = lambda a: pl.BlockSpec(a.shape, lambda i: (0,) * a.ndim)
    return pl.pallas_call(
        _route_kernel,
        grid=(nt,),
        in_specs=[tok(D_MODEL), pl.BlockSpec((1, 6, D_MODEL), lambda i: (mod_row(i), 0, 0)),
                  tok(GROUP_W), tok(GROUP_W), tok(GROUP_W), tok(GROUP_W),
                  const(wout), const(g2), const(wr), const(br)],
        out_specs=[tok(D_MODEL), tok(D_MODEL), tok(LANES),
                   pl.BlockSpec((1, 1, LANES), lambda i: (i, 0, 0)), pl.BlockSpec((1, 1, tm), lambda i: (i, 0, 0))],
        out_shape=[jax.ShapeDtypeStruct((t, D_MODEL), F32), jax.ShapeDtypeStruct((t, D_MODEL), BF16),
                   jax.ShapeDtypeStruct((t, LANES), F32),
                   jax.ShapeDtypeStruct((nt, 1, LANES), F32), jax.ShapeDtypeStruct((nt, 1, tm), F32)],
        compiler_params=_cparams(1),
        name="route",
    )(x, mod, *mixes, wout, g2, wr, br)


def _for_each_piece(n_rows, fn):
    off = 0
    for bit in RUN_BITS:
        has = (n_rows & bit) != 0
        pl.when(has)(functools.partial(fn, off, bit))
        off = off + jnp.where(has, bit, 0)


def _regroup_kernel(loc0_ref, len_ref, dst0_ref, tail0_ref, tail_len_ref, slack_ref, h2_ref, gate_ref, slot_ref,
                    gs_hbm, srt_s, zero_s, sem, zsem):
    i = pl.program_id(0)
    n = pl.num_programs(0)
    tm = SORT_TM
    start, wait = (lambda cp: cp.start()), (lambda cp: cp.wait())

    def zero_pieces(do):
        def tail_copy(g, off, size):
            return pltpu.make_async_copy(zero_s.at[pl.ds(0, size)],
                                         gs_hbm.at[pl.ds(pl.multiple_of(tail0_ref[g] + off, RUN_PAD), size)], zsem)

        def slack_copy(j):
            size = RUN_BITS[-1]
            return pltpu.make_async_copy(zero_s, gs_hbm.at[pl.ds(pl.multiple_of(slack_ref[0] + j * size, size), size)], zsem)

        for g in range(N_EGROUPS):
            _for_each_piece(tail_len_ref[g], lambda off, size, g=g: do(tail_copy(g, off, size)))
        lax.fori_loop(0, slack_ref[1], lambda j, c: (do(slack_copy(j)), c)[1], 0)

    def run_pieces(tile, do):
        buf = tile % 2
        for g in range(N_EGROUPS):
            r = tile * N_EGROUPS + g

            def piece(off, size, r=r):
                do(pltpu.make_async_copy(
                    srt_s.at[buf, pl.ds(pl.multiple_of(loc0_ref[r] + off, RUN_PAD), size)],
                    gs_hbm.at[pl.ds(pl.multiple_of(dst0_ref[r] + off, RUN_PAD), size)], sem.at[buf]))

            _for_each_piece(len_ref[r], piece)

    @pl.when(i == 0)
    def _zeros():
        zero_s[...] = jnp.zeros_like(zero_s)
        zero_pieces(start)

    @pl.when(i >= 2)
    def _reuse():
        run_pieces(i - 2, wait)

    perm = lax.broadcasted_iota(jnp.int32, (SLOTS, tm), 0).astype(F32) == slot_ref[0]
    srt_h = _dot(jnp.where(perm, 1.0, 0.0).astype(BF16), h2_ref[...])
    srt_g = _dot(jnp.where(perm, 1.0, 0.0), gate_ref[...], precision=HIGHEST)
    srt_s[i % 2] = jnp.concatenate([srt_h, srt_g], axis=1)
    run_pieces(i, start)

    @pl.when(i == n - 1)
    def _drain():
        pl.when(i >= 1)(lambda: run_pieces(i - 1, wait))
        run_pieces(i, wait)
        zero_pieces(wait)


def _regroup(h2, gate, slot, loc0, length, dst0, tail0, tail_len, slack, n_rows):
    t = h2.shape[0]
    tm = SORT_TM
    return pl.pallas_call(
        _regroup_kernel,
        grid_spec=pltpu.PrefetchScalarGridSpec(
            num_scalar_prefetch=6,
            grid=(t // tm,),
            in_specs=[pl.BlockSpec((tm, D_MODEL), lambda i, *_: (i, 0)),
                      pl.BlockSpec((tm, LANES), lambda i, *_: (i, 0)),
                      pl.BlockSpec((1, 1, tm), lambda i, *_: (i, 0, 0))],
            out_specs=pl.BlockSpec(memory_space=pl.ANY),
            scratch_shapes=[pltpu.VMEM((2, SLOTS, GS_COLS), F32), pltpu.VMEM((RUN_BITS[-1], GS_COLS), F32),
                            pltpu.SemaphoreType.DMA((2,)), pltpu.SemaphoreType.DMA(())],
        ),
        out_shape=jax.ShapeDtypeStruct((n_rows, GS_COLS), F32),
        compiler_params=_cparams(1),
        name="regroup",
    )(loc0, length, dst0, tail0, tail_len, slack, h2, gate, slot)


def _ffn_kernel(blk_grp_ref, n_valid_ref, gs_ref, wg_ref, wu_ref, wd_ref, ys_ref):
    b = pl.program_id(0)

    @pl.when(b >= n_valid_ref[0])
    def _unused():
        ys_ref[...] = jnp.zeros_like(ys_ref)

    @pl.when(b < n_valid_ref[0])
    def _block():
        grp = blk_grp_ref[b]
        xs = gs_ref[:, :D_MODEL].astype(BF16)
        gates = gs_ref[:, D_MODEL:]
        lane = lax.broadcasted_iota(jnp.int32, gates.shape, 1)
        hid = []
        for j in range(N_EPG):
            gcol = jnp.sum(jnp.where(lane == grp * N_EPG + j + ROUTE_OFF, gates, 0.0), axis=-1, keepdims=True)
            hj = _silu(_dot(xs, wg_ref[j])) * _dot(xs, wu_ref[j])
            hid.append((hj * gcol).astype(BF16))
        ys_ref[...] = _dot(jnp.concatenate(hid, axis=1), wd_ref[...]).astype(BF16)


def _group_ffn(gs, blk_grp, n_valid, wg, wu, wd):
    nblk = gs.shape[0] // FFN_BLK
    live = lambda b, bg, nv: jnp.minimum(b, nv[0] - 1)
    return pl.pallas_call(
        _ffn_kernel,
        grid_spec=pltpu.PrefetchScalarGridSpec(
            num_scalar_prefetch=2,
            grid=(nblk,),
            in_specs=[
                pl.BlockSpec((FFN_BLK, GS_COLS), lambda b, bg, nv: (live(b, bg, nv), 0)),
                pl.BlockSpec((N_EPG, D_MODEL, EXPERT_FF), lambda b, bg, nv: (bg[b], 0, 0)),
                pl.BlockSpec((N_EPG, D_MODEL, EXPERT_FF), lambda b, bg, nv: (bg[b], 0, 0)),
                pl.BlockSpec((N_EPG * EXPERT_FF, D_MODEL), lambda b, bg, nv: (bg[b], 0)),
            ],
            out_specs=pl.BlockSpec((FFN_BLK, D_MODEL), lambda b, bg, nv: (b, 0)),
        ),
        out_shape=jax.ShapeDtypeStruct((nblk * FFN_BLK, D_MODEL), BF16),
        compiler_params=_cparams(1),
        name="group_ffn",
    )(blk_grp, n_valid, gs, wg, wu, wd)


def _combine_kernel(final, loc0_ref, len_ref, dst0_ref, x1_ref, m_ref, slot_ref, gfin_ref, ys_hbm, o_ref, run_s, sem):
    i = pl.program_id(0)
    tm = SORT_TM

    def fetch(tile, do):
        buf = tile % 2
        for g in range(N_EGROUPS):
            r = tile * N_EGROUPS + g

            def piece(off, size, r=r):
                do(pltpu.make_async_copy(
                    ys_hbm.at[pl.ds(pl.multiple_of(dst0_ref[r] + off, RUN_PAD), size)],
                    run_s.at[buf, pl.ds(pl.multiple_of(loc0_ref[r] + off, RUN_PAD), size)], sem.at[buf]))

            _for_each_piece(len_ref[r], piece)

    @pl.when(i == 0)
    def _first():
        fetch(i, lambda cp: cp.start())

    @pl.when(i + 1 < pl.num_programs(0))
    def _prefetch():
        fetch(i + 1, lambda cp: cp.start())

    fetch(i, lambda cp: cp.wait())

    used = loc0_ref[i * N_EGROUPS + N_EGROUPS - 1] + len_ref[i * N_EGROUPS + N_EGROUPS - 1]
    rows = lax.broadcasted_iota(jnp.int32, (SLOTS_PAD, D_MODEL), 0)
    y_run = jnp.where(rows < used, run_s[i % 2], jnp.zeros((), BF16))
    slot_b = jnp.broadcast_to(slot_ref[0], (LANES, tm)).T
    lane = lax.broadcasted_iota(jnp.int32, (tm, LANES), 1).astype(F32)
    inv = jnp.concatenate([jnp.where(slot_b == lane + float(c), 1.0, 0.0) for c in range(0, SLOTS_PAD, LANES)],
                          axis=1).astype(BF16)
    x2 = x1_ref[...] + m_ref[0, 5:6] * _dot(inv, y_run)
    o_ref[...] = _rms(x2) * gfin_ref[...] if final else x2


def _combine(final, x1, mod, mod_row, slot, gfin, ys, loc0, length, dst0):
    t = x1.shape[0]
    tm = SORT_TM
    return pl.pallas_call(
        functools.partial(_combine_kernel, final),
        grid_spec=pltpu.PrefetchScalarGridSpec(
            num_scalar_prefetch=3,
            grid=(t // tm,),
            in_specs=[
                pl.BlockSpec((tm, D_MODEL), lambda i, *_: (i, 0)),
                pl.BlockSpec((1, 6, D_MODEL), lambda i, *_: (mod_row(i), 0, 0)),
                pl.BlockSpec((1, 1, tm), lambda i, *_: (i, 0, 0)),
                pl.BlockSpec((1, D_MODEL), lambda i, *_: (0, 0)),
                pl.BlockSpec(memory_space=pl.ANY),
            ],
            out_specs=pl.BlockSpec((tm, D_MODEL), lambda i, *_: (i, 0)),
            scratch_shapes=[pltpu.VMEM((2, SLOTS_PAD, D_MODEL), BF16), pltpu.SemaphoreType.DMA((2,))],
        ),
        out_shape=jax.ShapeDtypeStruct((t, D_MODEL), F32),
        compiler_params=_cparams(1),
        name="combine",
    )(loc0, length, dst0, x1, mod, slot, gfin, ys)


def _outproj_moe(final, x, mod, mod_row, mixes, wout, g2, wr, br, wg, wu, wd, gfin):
    t = x.shape[0]
    nt = t // SORT_TM
    x1, h2, gate, cnt, slot = _route(x, mod, mod_row, mixes, wout, g2, wr, br)
    cnt = cnt[:, 0, :N_EGROUPS].astype(jnp.int32)
    length = (cnt + RUN_PAD - 1) // RUN_PAD * RUN_PAD
    loc0 = jnp.cumsum(length, axis=1) - length
    g_rows = jnp.sum(length, axis=0)
    g_blocks = (g_rows + FFN_BLK - 1) // FFN_BLK
    g_base = (jnp.cumsum(g_blocks) - g_blocks) * FFN_BLK
    dst0 = g_base[None, :] + jnp.cumsum(length, axis=0) - length
    n_blocks = (t + nt * N_EGROUPS * (RUN_PAD - 1) + FFN_BLK - 1) // FFN_BLK + N_EGROUPS
    blk_grp = jnp.minimum(jnp.sum(jnp.arange(n_blocks)[:, None] >= jnp.cumsum(g_blocks)[None, :], axis=1),
                          N_EGROUPS - 1).astype(jnp.int32)
    n_valid = jnp.sum(g_blocks).astype(jnp.int32)[None]
    flat = lambda a: a.reshape(-1).astype(jnp.int32)
    slack = jnp.stack([n_valid[0] * FFN_BLK, (n_blocks - n_valid[0]) * (FFN_BLK // RUN_BITS[-1])]).astype(jnp.int32)
    gs = _regroup(h2, gate, slot, flat(loc0), flat(length), flat(dst0), flat(g_base + g_rows),
                  flat(g_blocks * FFN_BLK - g_rows), slack, n_blocks * FFN_BLK)
    ys = _group_ffn(gs, blk_grp, n_valid, wg, wu, wd)
    return _combine(final, x1, mod, mod_row, slot, gfin, ys, flat(loc0), flat(length), flat(dst0))


def _block_diag(w):
    n, k, _ = w.shape
    eye = jnp.eye(n, dtype=w.dtype)
    return (eye[:, None, :, None] * w[:, :, None, :]).reshape(n * k, n * k)


def _pad_lanes(v, n=LANES):
    return jnp.pad(v, ((0, 0), (0, n - v.shape[-1])))


def kernel(x_prompt, x_sample, c, cache_k, cache_v, state_ssd, state_rglru, c_ctx, w_mod, b_mod, norm1_g, norm2_g, w_in, w_out, ssd_conv_w, ssd_conv_b, ssd_dt_bias, ssd_a_log, ssd_d, ssd_norm_g, pool_w, pool_scale, da_lam_q1, da_lam_k1, da_lam_q2, da_lam_k2, da_norm_g, rg_conv_w, rg_conv_b, rg_wa, rg_ba, rg_wx, rg_bx, rg_lambda, moe_w_group, moe_b_group, moe_w_expert, moe_b_expert, moe_w_gate, moe_w_up, moe_w_down, final_norm_g):
    nbp, lp, _ = x_prompt.shape
    nbs, ls, _ = x_sample.shape
    past = cache_k.shape[2]
    tm_in = 256
    assert nbs + 1 <= MOD_ROWS and lp % CH == 0 and ls % CH == 0
    assert lp == tm_in and ls % tm_in == 0 and lp % SORT_TM == 0 and ls % SORT_TM == 0
    assert lp % KEY_BLK == 0 and ls % KEY_BLK == 0 and past % KEY_BLK == 0

    cond = jnp.concatenate([c_ctx[None, :], c, jnp.zeros((MOD_ROWS - 1 - nbs, D_MODEL), F32)], axis=0)
    mod = _modulation(cond, w_mod, b_mod).reshape(DEPTH * MOD_ROWS, 6, D_MODEL)

    w_in_r = jnp.concatenate([w_in[:, :, :DT_LO], w_in[:, :, DT_HI:], w_in[:, :, DT_LO:DT_HI],
                              jnp.zeros((DEPTH, D_MODEL, LANES - (DT_HI - DT_LO)), F32)], axis=-1).astype(BF16)
    w_out_b = w_out.astype(BF16)
    w_gate_b = moe_w_gate.astype(BF16)
    w_up_b = moe_w_up.astype(BF16)
    w_down_b = moe_w_down.astype(BF16).reshape(DEPTH, N_EXPERTS * EXPERT_FF, D_MODEL)
    w_route = jnp.concatenate([moe_w_group, moe_w_expert,
                               jnp.zeros((DEPTH, D_MODEL, LANES - N_EGROUPS - N_EXPERTS), F32)], axis=-1)
    w_route_hi = w_route.astype(BF16)
    w_route = jnp.concatenate([w_route_hi, (w_route - w_route_hi.astype(F32)).astype(BF16)], axis=-1)
    b_route = _pad_lanes(jnp.concatenate([moe_b_group, moe_b_expert], axis=-1))
    dtb = _pad_lanes(ssd_dt_bias.reshape(DEPTH, 2 * SSD_HEADS))
    alog = _pad_lanes(ssd_a_log.reshape(DEPTH, 2 * SSD_HEADS))
    d_skip = jnp.repeat(ssd_d, SSD_HEADDIM, axis=-1)
    cos, sin = _rope_tables(ls)
    ck = cache_k.reshape(nbs, DEPTH, past, GROUP_W)
    cv = cache_v.reshape(nbs, DEPTH, past, GROUP_W)
    g_fin = final_norm_g[None, :]

    xp = x_prompt.reshape(nbp * lp, D_MODEL)
    xs = x_sample.reshape(nbs * ls, D_MODEL)
    new_k = jnp.zeros((nbp, DEPTH, lp, GROUP_W), F32)
    new_v = jnp.zeros((nbp, DEPTH, lp, GROUP_W), F32)
    ssd_out, rg_out = [], []
    for l in range(DEPTH):
        row1 = lambda a: a[l][None, :]
        final = l == DEPTH - 1
        lam_init = 0.8 - 0.6 * math.exp(-0.3 * l)
        ssd_w = (ssd_conv_w[l], row1(ssd_conv_b), row1(dtb), row1(alog), row1(d_skip), row1(ssd_norm_g))
        pool_wb = _block_diag(pool_w[l])
        att_w = (row1(da_lam_q1), row1(da_lam_k1), row1(da_lam_q2), row1(da_lam_k2), row1(da_norm_g))
        rg_w = (rg_conv_w[l], row1(rg_conv_b),
                jnp.stack([_block_diag(rg_wa[l, 0]), _block_diag(rg_wa[l, 1])]), rg_ba[l][:, None, :],
                jnp.stack([_block_diag(rg_wx[l, 0]), _block_diag(rg_wx[l, 1])]), rg_bx[l][:, None, :],
                rg_lambda[l][:, None, :])
        moe_w = (w_out_b[l], row1(norm2_g), w_route[l], row1(b_route), w_gate_b[l], w_up_b[l], w_down_b[l], g_fin)
        ctx_row = lambda i, l=l: l * MOD_ROWS
        lat_row = lambda tm: (lambda i, l=l: l * MOD_ROWS + 1 + i // (ls // tm))
        assert tm_in == SORT_TM

        xbc, z, xpool, q, new_k, new_v, xr, gr, dt = _inproj(
            xp, mod, ctx_row, row1(norm1_g), w_in_r[l], tm_in, l, caches=(new_k, new_v))
        ya, st_ssd = _ssd(False, nbp, lp, l, xbc, z, dt, *ssd_w)
        yb = _pool(nbp, lp, xpool, pool_wb, row1(pool_scale))
        yc = _attn(False, nbp, lp, l, lam_init, q, new_k, new_v, *att_w)
        yd, st_rg = _rglru(False, nbp, lp, l, xr, gr, *rg_w)
        xp = _outproj_moe(final, xp, mod, ctx_row, (ya, yb, yc, yd), *moe_w)
        ssd_out.append(st_ssd)
        rg_out.append(st_rg)

        xbc, z, xpool, q, k, v, xr, gr, dt = _inproj(xs, mod, lat_row(tm_in), row1(norm1_g), w_in_r[l], tm_in, l)
        ya = _ssd(True, nbs, ls, l, xbc, z, dt, *ssd_w, h0=state_ssd)
        yb = _pool(nbs, ls, xpool, pool_wb, row1(pool_scale))
        yc = _attn(True, nbs, ls, l, lam_init, q, k, v, *att_w, ck=ck, cv=cv, cos=cos, sin=sin)
        yd = _rglru(True, nbs, ls, l, xr, gr, *rg_w, h0=state_rglru)
        xs = _outproj_moe(final, xs, mod, lat_row(SORT_TM), (ya, yb, yc, yd), *moe_w)

    return (xp.reshape(nbp, lp, D_MODEL), xs.reshape(nbs, ls, D_MODEL),
            new_k.reshape(nbp, DEPTH, lp, DA_HEADS, 2 * DA_QKDIM), new_v.reshape(nbp, DEPTH, lp, DA_HEADS, DA_VDIM),
            jnp.stack(ssd_out, axis=1), jnp.stack(rg_out, axis=1))
```

```python
import functools
import math

import numpy as np
import jax
import jax.numpy as jnp
from jax import lax
from jax.experimental import pallas as pl
from jax.experimental.pallas import tpu as pltpu

F32 = jnp.float32
BF16 = jnp.bfloat16
HIGHEST = lax.Precision.HIGHEST

D_MODEL = 1024
DEPTH = 4
GRID_W = 64
GROUP_W = 256
EPS = 1e-6
SSD_HEADDIM = 64
SSD_HEADS = 4
SSD_STATE = 64
SSD_BC = 128
SSD_CONV_CH = 512
POOL_WINDOWS = (2, 4, 8, 16)
DA_HEADS = 4
DA_VDIM = 64
DA_QKDIM = 32
ROPE_BASE = 10000.0
RG_C = 8.0
N_EGROUPS = 4
N_EPG = 4
N_EXPERTS = 16
EXPERT_FF = 256
DT_LO, DT_HI = 768, 776

LANES = 128
SUBLANES = 8
CH = 128
HALO = SUBLANES
MOD_ROWS = 16
VMEM_LIMIT = 56 * 1024 * 1024


def _cparams(n_axes):
    return pltpu.CompilerParams(dimension_semantics=("arbitrary",) * n_axes, vmem_limit_bytes=VMEM_LIMIT)


def _silu(x):
    return x * jax.nn.sigmoid(x)


def _softplus(x):
    return jnp.maximum(x, 0.0) + jnp.log1p(jnp.exp(-jnp.abs(x)))


def _dot(a, b, **kw):
    return jnp.dot(a, b, preferred_element_type=F32, **kw)


def _dot_nt(a, b):
    return lax.dot_general(a, b, (((1,), (1,)), ((), ())), preferred_element_type=F32)


def _rms(x):
    return x * lax.rsqrt(jnp.mean(x * x, axis=-1, keepdims=True) + EPS)


def _window(ref, c, n_steps, seq_len):
    r0 = pl.multiple_of(c * CH, CH)
    main = ref[pl.ds(r0, CH), :]
    lo = pl.multiple_of(jnp.maximum(r0 - HALO, 0), HALO)
    hi = pl.multiple_of(jnp.minimum(r0 + CH, seq_len - HALO), HALO)
    prev = jnp.where(c > 0, ref[pl.ds(lo, HALO), :], 0.0)
    nxt = jnp.where(c < n_steps - 1, ref[pl.ds(hi, HALO), :], 0.0)
    return jnp.concatenate([prev, main, nxt], axis=0)


def _conv4(win, w_ref, b_ref):
    acc = b_ref[...]
    for k in range(4):
        acc = acc + w_ref[k:k + 1, :] * win[HALO - 1 + k:HALO - 1 + k + CH, :]
    return acc


def _mod_kernel(cond_ref, w_ref, b_ref, o_ref):
    cnd = cond_ref[...]
    o_ref[0] = _dot(_silu(cnd), w_ref[0], precision=HIGHEST) + b_ref[0]


def _modulation(cond, w_mod, b_mod):
    nb = 6
    return pl.pallas_call(
        _mod_kernel,
        grid=(DEPTH, nb),
        in_specs=[
            pl.BlockSpec((MOD_ROWS, D_MODEL), lambda l, j: (0, 0)),
            pl.BlockSpec((1, D_MODEL, D_MODEL), lambda l, j: (l, 0, j)),
            pl.BlockSpec((1, 1, D_MODEL), lambda l, j: (l, 0, j)),
        ],
        out_specs=pl.BlockSpec((1, MOD_ROWS, D_MODEL), lambda l, j: (l, 0, j)),
        out_shape=jax.ShapeDtypeStruct((DEPTH, MOD_ROWS, nb * D_MODEL), F32),
        compiler_params=_cparams(2),
        name="modulation",
    )(cond, w_mod, b_mod.reshape(DEPTH, 1, nb * D_MODEL))


IN_COLS = (512, 256, 256, 256, 256, 256, 256, 256, LANES)
K_OUT, V_OUT = 4, 5


def _inproj_kernel(to_cache, x_ref, m_ref, g_ref, w_ref, *refs):
    if to_cache:
        refs = refs[2:]
    hh = _rms(x_ref[...]) * g_ref[...] * (1.0 + m_ref[0, 1:2]) + m_ref[0, 0:1]
    u = _dot(hh.astype(BF16), w_ref[...])
    off = 0
    for j, (ref, n) in enumerate(zip(refs, IN_COLS)):
        if to_cache and j in (K_OUT, V_OUT):
            ref[0, 0] = u[:, off:off + n]
        else:
            ref[...] = u[:, off:off + n]
        off += n


def _inproj(x, mod, mod_row, g, w, tm, layer, caches=None):
    t = x.shape[0]
    ncol = sum(IN_COLS)
    to_cache = caches is not None
    in_specs = [
        pl.BlockSpec((tm, D_MODEL), lambda i: (i, 0)),
        pl.BlockSpec((1, 6, D_MODEL), lambda i: (mod_row(i), 0, 0)),
        pl.BlockSpec((1, D_MODEL), lambda i: (0, 0)),
        pl.BlockSpec((D_MODEL, ncol), lambda i: (0, 0)),
    ]
    out_specs = [pl.BlockSpec((tm, n), lambda i: (i, 0)) for n in IN_COLS]
    out_shape = [jax.ShapeDtypeStruct((t, n), F32) for n in IN_COLS]
    args = [x, mod, g, w]
    aliases = {}
    if to_cache:
        assert caches[0].shape[2] == tm
        cache_spec = pl.BlockSpec((1, 1, tm, GROUP_W), lambda i: (i, layer, 0, 0))
        for j, cch in zip((K_OUT, V_OUT), caches):
            in_specs.append(pl.BlockSpec(memory_space=pl.ANY))
            aliases[len(args)] = j
            args.append(cch)
            out_specs[j] = cache_spec
            out_shape[j] = jax.ShapeDtypeStruct(cch.shape, F32)
    return pl.pallas_call(
        functools.partial(_inproj_kernel, to_cache),
        grid=(t // tm,),
        in_specs=in_specs,
        out_specs=out_specs,
        out_shape=out_shape,
        input_output_aliases=aliases,
        compiler_params=_cparams(1),
        name="inproj_ctx" if to_cache else "inproj",
    )(*args)


def _ssd_kernel(has_ctx, seq_len, xbc_ref, z_ref, dt_ref, cw_ref, cb_ref, dtb_ref, alog_ref, d_ref, ng_ref, *rest):
    if has_ctx:
        h0_ref, y_ref, xc_s, y_s, st_s, upd_s, grow_s, keep_s = rest
        st_ref = None
    else:
        y_ref, st_ref, xc_s, y_s, st_s, upd_s, grow_s, keep_s = rest
    nc = seq_len // CH
    hd = SSD_HEADDIM
    a_row = -jnp.exp(alog_ref[...])
    row = lax.broadcasted_iota(jnp.int32, (CH, CH), 0)
    col = lax.broadcasted_iota(jnp.int32, (CH, CH), 1)
    lane_w = lax.broadcasted_iota(jnp.int32, (CH, GROUP_W), 1)
    lane_n = lax.broadcasted_iota(jnp.int32, (CH, SSD_BC), 1)
    own_block = (lax.broadcasted_iota(jnp.int32, (SSD_BC, GROUP_W), 0) // SSD_STATE
                 == lax.broadcasted_iota(jnp.int32, (SSD_BC, GROUP_W), 1) // (2 * hd))

    def conv_step(c, carry):
        r0 = pl.multiple_of(c * CH, CH)
        xc = _silu(_conv4(_window(xbc_ref, c, nc, seq_len), cw_ref, cb_ref))
        xc_s[pl.ds(r0, CH), :] = xc
        y_s[pl.ds(r0, CH), :] = xc[:, :GROUP_W] * d_ref[...]
        return carry

    lax.fori_loop(0, nc, conv_step, 0)

    st_s[...] = jnp.zeros_like(st_s)
    if has_ctx:
        for d in range(2):
            for h in range(SSD_HEADS):
                g = h // 2
                st_s[d, g * SSD_STATE:(g + 1) * SSD_STATE, h * hd:(h + 1) * hd] = h0_ref[0, 0, d, h].T

    def per_head(v, d):
        lanes = lane_w[:v.shape[0]]
        out = jnp.broadcast_to(v[:, 4 * d + 3:4 * d + 4], (v.shape[0], GROUP_W))
        for h in (2, 1, 0):
            out = jnp.where(lanes < (h + 1) * hd, jnp.broadcast_to(v[:, 4 * d + h:4 * d + h + 1], out.shape), out)
        return out

    def local_step(c, carry):
        r0 = pl.multiple_of(c * CH, CH)
        xc = xc_s[pl.ds(r0, CH), :]
        x = xc[:, :GROUP_W]
        bm = xc[:, GROUP_W:GROUP_W + SSD_BC]
        cm = xc[:, GROUP_W + SSD_BC:]
        dt = _softplus(dt_ref[pl.ds(r0, CH), :] + dtb_ref[...])
        bmb = bm.astype(BF16)
        bm_t = bm.T.astype(BF16)
        scores = [_dot_nt(jnp.where((lane_n // SSD_STATE) == g, cm, 0.0).astype(BF16), bmb) for g in range(2)]
        y_diag = None
        da = dt * a_row
        pre = da
        for k in (1, 2, 4, 8, 16, 32, 64):
            pre = pre + jnp.where(row >= k, pltpu.roll(pre, k, axis=0), 0.0)
        for d in range(2):
            tri = (row >= col) if d == 0 else (row <= col)
            cs = pre if d == 0 else pre[CH - 1:CH, :] - pre + da
            cs_t = cs.T
            tot = cs[CH - 1:CH, :] if d == 0 else cs[0:1, :]
            dt_e, cs_e, tot_e = per_head(dt, d), per_head(cs, d), per_head(tot, d)
            xdt = x * dt_e
            m_parts, r_parts = [], []
            for h in range(SSD_HEADS):
                k = SSD_HEADS * d + h
                decay = jnp.exp(jnp.where(tri, cs[:, k:k + 1] - cs_t[k:k + 1, :], -jnp.inf))
                m_parts.append((scores[h // 2] * decay).astype(BF16))
                r_parts.append(jnp.where((lane_w // hd) == h, xdt, 0.0).astype(BF16))
            yd = _dot(jnp.concatenate(m_parts, axis=1), jnp.concatenate(r_parts, axis=0))
            y_diag = yd if y_diag is None else y_diag + yd
            wgt = xdt * jnp.exp(tot_e - cs_e)
            upd_s[d, c] = jnp.where(own_block, _dot(bm_t, wgt.astype(BF16)), 0.0)
            grow_s[d, pl.ds(r0, CH), :] = jnp.exp(cs_e)
            keep_s[d, pl.ds(c, 1), :] = jnp.exp(tot_e)
        y_s[pl.ds(r0, CH), :] += y_diag
        return carry

    lax.fori_loop(0, nc, local_step, 0, unroll=2)

    def state_dir(d, ci):
        r0 = pl.multiple_of(ci * CH, CH)
        st = st_s[d]
        cmb = xc_s[pl.ds(r0, CH), GROUP_W + SSD_BC:].astype(BF16)
        y_s[pl.ds(r0, CH), :] += _dot(cmb, st.astype(BF16)) * grow_s[d, pl.ds(r0, CH), :]
        st_s[d] = st * keep_s[d, pl.ds(ci, 1), :] + upd_s[d, ci]

    def scan_step(c, carry):
        state_dir(0, c)
        state_dir(1, nc - 1 - c)
        return carry

    lax.fori_loop(0, nc, scan_step, 0, unroll=2)

    def out_step(c, carry):
        r0 = pl.multiple_of(c * CH, CH)
        y = y_s[pl.ds(r0, CH), :] * _silu(z_ref[pl.ds(r0, CH), :])
        y_ref[pl.ds(r0, CH), :] = _rms(y) * ng_ref[...]
        return carry

    lax.fori_loop(0, nc, out_step, 0)
    if not has_ctx:
        for d in range(2):
            for h in range(SSD_HEADS):
                g = h // 2
                st_ref[0, d, h] = st_s[d, g * SSD_STATE:(g + 1) * SSD_STATE, h * hd:(h + 1) * hd].T


def _ssd(has_ctx, nb, seq_len, layer, xbc, z, dt, cw, cb, dtb, alog, dsk, ng, h0=None):
    rows = lambda n: pl.BlockSpec((seq_len, n), lambda b: (b, 0))
    full = lambda a: pl.BlockSpec(a.shape, lambda b: (0,) * a.ndim)
    in_specs = [rows(SSD_CONV_CH), rows(GROUP_W), rows(LANES)] + [full(a) for a in (cw, cb, dtb, alog, dsk, ng)]
    args = [xbc, z, dt, cw, cb, dtb, alog, dsk, ng]
    y_spec = rows(GROUP_W)
    y_shape = jax.ShapeDtypeStruct((nb * seq_len, GROUP_W), F32)
    st_blk = (1, 2, SSD_HEADS, SSD_HEADDIM, SSD_STATE)
    if has_ctx:
        in_specs.append(pl.BlockSpec((1, 1) + st_blk[1:], lambda b: (b, layer, 0, 0, 0, 0)))
        args.append(h0)
        out_specs, out_shape = y_spec, y_shape
    else:
        out_specs = [y_spec, pl.BlockSpec(st_blk, lambda b: (b, 0, 0, 0, 0))]
        out_shape = [y_shape, jax.ShapeDtypeStruct((nb,) + st_blk[1:], F32)]
    return pl.pallas_call(
        functools.partial(_ssd_kernel, has_ctx, seq_len),
        grid=(nb,),
        in_specs=in_specs,
        out_specs=out_specs,
        out_shape=out_shape,
        scratch_shapes=[
            pltpu.VMEM((seq_len, SSD_CONV_CH), F32),
            pltpu.VMEM((seq_len, GROUP_W), F32),
            pltpu.VMEM((2, SSD_BC, GROUP_W), F32),
            pltpu.VMEM((2, seq_len // CH, SSD_BC, GROUP_W), F32),
            pltpu.VMEM((2, seq_len, GROUP_W), F32),
            pltpu.VMEM((2, max(seq_len // CH, SUBLANES), GROUP_W), F32),
        ],
        compiler_params=_cparams(1),
        name="ssd_ctx" if has_ctx else "ssd",
    )(*args)


def _pool_kernel(seq_len, x_ref, w_ref, sc_ref, y_ref):
    nc = seq_len // CH
    wn = CH + 2 * HALO
    lane = lax.broadcasted_iota(jnp.int32, (CH, GROUP_W), 1)
    gw = GROUP_W // len(POOL_WINDOWS)
    half = jnp.where(lane < gw, 1, jnp.where(lane < 2 * gw, 2, jnp.where(lane < 3 * gw, 4, 8)))
    w_blk = w_ref[...].astype(BF16)

    def ahead(v, k):
        return pltpu.roll(v, wn - k, axis=0)

    def step(c, carry):
        r0 = pl.multiple_of(c * CH, CH)
        win = _window(x_ref, c, nc, seq_len)
        p2 = win + ahead(win, 1)
        p4 = p2 + ahead(p2, 2)
        p8 = p4 + ahead(p4, 4)
        p16 = p8 + ahead(p8, 8)
        s2 = ahead(p2, HALO - 1)[:CH]
        s4 = ahead(p4, HALO - 2)[:CH]
        s8 = ahead(p8, HALO - 4)[:CH]
        s16 = p16[:CH]
        tot = jnp.where(lane < gw, s2, jnp.where(lane < 2 * gw, s4, jnp.where(lane < 3 * gw, s8, s16)))
        t = r0 + lax.broadcasted_iota(jnp.int32, (CH, GROUP_W), 0)
        cnt = jnp.minimum(t + half, seq_len) - jnp.maximum(t - half, 0)
        x = win[HALO:HALO + CH]
        diff = tot / cnt.astype(F32) - x
        y_ref[pl.ds(r0, CH), :] = _dot(diff.astype(BF16), w_blk) * sc_ref[...]
        return carry

    lax.fori_loop(0, nc, step, 0)


def _pool(nb, seq_len, x, w_blk, scale):
    return pl.pallas_call(
        functools.partial(_pool_kernel, seq_len),
        grid=(nb,),
        in_specs=[
            pl.BlockSpec((seq_len, GROUP_W), lambda b: (b, 0)),
            pl.BlockSpec((GROUP_W, GROUP_W), lambda b: (0, 0)),
            pl.BlockSpec((1, GROUP_W), lambda b: (0, 0)),
        ],
        out_specs=pl.BlockSpec((seq_len, GROUP_W), lambda b: (b, 0)),
        out_shape=jax.ShapeDtypeStruct((nb * seq_len, GROUP_W), F32),
        compiler_params=_cparams(1),
        name="pool",
    )(x, w_blk, scale)


KEY_BLK = 256


def _rope(x, cos, sin):
    c2 = jnp.concatenate([cos, cos], axis=1)
    s2 = jnp.concatenate([sin, sin], axis=1)
    lane = lax.broadcasted_iota(jnp.int32, x.shape, 1)
    n = x.shape[1]
    partner = jnp.where(lane % 2 == 0, pltpu.roll(x, n - 1, axis=1), pltpu.roll(x, 1, axis=1))
    return x * c2 + partner * s2


def _attn_kernel(has_ctx, seq_len, tq, past, lam_init, q_ref, k_ref, v_ref, lq1, lk1, lq2, lk2, ng_ref, *rest):
    if has_ctx:
        ck_ref, cv_ref, cosq_ref, sinq_ref, cosk_ref, sink_ref, o_ref, kt_s, v_s = rest
    else:
        o_ref, kt_s, v_s = rest
    kb = KEY_BLK

    @pl.when(pl.program_id(1) == 0)
    def _prepare_keys():
        def put(dst0, kk, vv):
            kt_s[:, dst0:dst0 + kb] = kk.T.astype(BF16)
            for h in range(DA_HEADS):
                v_s[h, dst0:dst0 + kb, :] = vv[:, h * DA_VDIM:(h + 1) * DA_VDIM].astype(BF16)

        if has_ctx:
            for j in range(past // kb):
                put(j * kb, ck_ref[0, 0, j * kb:(j + 1) * kb, :], cv_ref[0, 0, j * kb:(j + 1) * kb, :])
        for j in range(seq_len // kb):
            kk = k_ref[0, 0, j * kb:(j + 1) * kb, :] if not has_ctx else k_ref[j * kb:(j + 1) * kb, :]
            vv = v_ref[0, 0, j * kb:(j + 1) * kb, :] if not has_ctx else v_ref[j * kb:(j + 1) * kb, :]
            if has_ctx:
                kk = _rope(kk, cosk_ref[j * kb:(j + 1) * kb, :], sink_ref[j * kb:(j + 1) * kb, :])
            put(past + j * kb, kk, vv)

    q = q_ref[...]
    if has_ctx:
        q = _rope(q, cosq_ref[...], sinq_ref[...])
    q = q * (DA_QKDIM ** -0.5 * math.log2(math.e))
    lam = (jnp.exp(jnp.sum(lq1[...] * lk1[...], axis=-1, keepdims=True))
           - jnp.exp(jnp.sum(lq2[...] * lk2[...], axis=-1, keepdims=True)) + lam_init)
    for h in range(DA_HEADS):
        es, sums = [], []
        for m in range(2):
            lo = h * 2 * DA_QKDIM + m * DA_QKDIM
            s = _dot(q[:, lo:lo + DA_QKDIM].astype(BF16), kt_s[lo:lo + DA_QKDIM, :])
            e = jnp.exp2(s - jnp.max(s, axis=-1, keepdims=True))
            sums.append(jnp.sum(e, axis=-1, keepdims=True))
            es.append(e.astype(BF16))
        o = _dot(jnp.concatenate(es, axis=0), v_s[h])
        acc = o[:tq] / sums[0] - lam * (o[tq:] / sums[1])
        o_ref[:, h * DA_VDIM:(h + 1) * DA_VDIM] = _rms(acc) * ng_ref[...] * (1.0 - lam_init)


def _attn(has_ctx, nb, seq_len, layer, lam_init, q, k, v, lq1, lk1, lq2, lk2, ng, ck=None, cv=None, cos=None, sin=None):
    tq = 256
    nq = seq_len // tq
    past = ck.shape[2] if has_ctx else 0
    keys = seq_len + past
    small = lambda a: pl.BlockSpec(a.shape, lambda b, i: (0,) * a.ndim)
    if has_ctx:
        kv_spec = pl.BlockSpec((seq_len, GROUP_W), lambda b, i: (b, 0))
    else:
        kv_spec = pl.BlockSpec((1, 1, seq_len, GROUP_W), lambda b, i: (b, layer, 0, 0))
    in_specs = [pl.BlockSpec((tq, GROUP_W), lambda b, i: (b * nq + i, 0)), kv_spec, kv_spec]
    in_specs += [small(a) for a in (lq1, lk1, lq2, lk2, ng)]
    args = [q, k, v, lq1, lk1, lq2, lk2, ng]
    if has_ctx:
        in_specs += [
            pl.BlockSpec((1, 1, past, GROUP_W), lambda b, i: (b, layer, 0, 0)),
            pl.BlockSpec((1, 1, past, GROUP_W), lambda b, i: (b, layer, 0, 0)),
            pl.BlockSpec((tq, LANES), lambda b, i: (i, 0)),
            pl.BlockSpec((tq, LANES), lambda b, i: (i, 0)),
            pl.BlockSpec((seq_len, LANES), lambda b, i: (0, 0)),
            pl.BlockSpec((seq_len, LANES), lambda b, i: (0, 0)),
        ]
        args += [ck, cv, cos, sin, cos, sin]
    return pl.pallas_call(
        functools.partial(_attn_kernel, has_ctx, seq_len, tq, past, lam_init),
        grid=(nb, nq),
        in_specs=in_specs,
        out_specs=pl.BlockSpec((tq, GROUP_W), lambda b, i: (b * nq + i, 0)),
        out_shape=jax.ShapeDtypeStruct((nb * seq_len, GROUP_W), F32),
        scratch_shapes=[
            pltpu.VMEM((GROUP_W, keys), BF16),
            pltpu.VMEM((DA_HEADS, keys, DA_VDIM), BF16),
        ],
        compiler_params=_cparams(2),
        name="attn_ctx" if has_ctx else "attn",
    )(*args)


def _rope_tables(seq_len):
    t = np.arange(seq_len)
    rowp = (t // GRID_W).astype(np.float64)
    colp = (t % GRID_W).astype(np.float64)
    n_freq = DA_QKDIM // 4
    inv_freq = ROPE_BASE ** (-np.arange(n_freq, dtype=np.float64) / n_freq)
    ang = np.concatenate([rowp[:, None] * inv_freq, colp[:, None] * inv_freq], axis=-1)
    ang = np.repeat(ang, 2, axis=-1)
    sign = np.where(np.arange(DA_QKDIM) % 2 == 0, -1.0, 1.0)
    cos = np.tile(np.cos(ang), (1, LANES // DA_QKDIM)).astype(np.float32)
    sin = np.tile(np.sin(ang) * sign, (1, LANES // DA_QKDIM)).astype(np.float32)
    return jnp.asarray(cos), jnp.asarray(sin)


def _rglru_kernel(has_ctx, seq_len, x_ref, g_ref, cw_ref, cb_ref, wa_ref, ba_ref, wx_ref, bx_ref, lam_ref, *rest):
    if has_ctx:
        h0_ref, y_ref, a_s, u_s = rest
        st_ref = None
    else:
        y_ref, st_ref, a_s, u_s = rest
    nc = seq_len // CH
    nt = CH // SUBLANES
    sub = lax.broadcasted_iota(jnp.int32, (nt, SUBLANES, GROUP_W), 1)

    def gate_step(c, carry):
        r0 = pl.multiple_of(c * CH, CH)
        xc = _conv4(_window(x_ref, c, nc, seq_len), cw_ref, cb_ref)
        xb = xc.astype(BF16)
        for d in range(2):
            rg = jax.nn.sigmoid(_dot(xb, wa_ref[d].astype(BF16)) + ba_ref[d])
            ig = jax.nn.sigmoid(_dot(xb, wx_ref[d].astype(BF16)) + bx_ref[d])
            log_a = -RG_C * rg * _softplus(-lam_ref[d])
            a = jnp.exp(log_a)
            u = jnp.sqrt(-jnp.tanh(log_a) * (a * a + 1.0)) * (ig * xc)
            a3 = a.reshape(nt, SUBLANES, GROUP_W)
            u3 = u.reshape(nt, SUBLANES, GROUP_W)
            for k in (1, 2, 4):
                if d == 0:
                    ok = sub >= k
                    a_sh, u_sh = pltpu.roll(a3, k, axis=1), pltpu.roll(u3, k, axis=1)
                else:
                    ok = sub < SUBLANES - k
                    a_sh, u_sh = pltpu.roll(a3, SUBLANES - k, axis=1), pltpu.roll(u3, SUBLANES - k, axis=1)
                u3 = u3 + a3 * jnp.where(ok, u_sh, 0.0)
                a3 = a3 * jnp.where(ok, a_sh, 1.0)
            a_s[d, pl.ds(r0, CH), :] = a3.reshape(CH, GROUP_W)
            u_s[d, pl.ds(r0, CH), :] = u3.reshape(CH, GROUP_W)
        return carry

    lax.fori_loop(0, nc, gate_step, 0)

    n_tiles = seq_len // SUBLANES
    if has_ctx:
        hf0, hb0 = h0_ref[0, 0, 0:1, :], h0_ref[0, 0, 1:2, :]
    else:
        hf0 = hb0 = jnp.zeros((1, GROUP_W), F32)

    def carry_step(i, carry):
        hf, hb = carry
        rf = pl.multiple_of(i * SUBLANES, SUBLANES)
        rb = pl.multiple_of((n_tiles - 1 - i) * SUBLANES, SUBLANES)
        tf = u_s[0, pl.ds(rf, SUBLANES), :] + a_s[0, pl.ds(rf, SUBLANES), :] * hf
        tb = u_s[1, pl.ds(rb, SUBLANES), :] + a_s[1, pl.ds(rb, SUBLANES), :] * hb
        u_s[0, pl.ds(rf, SUBLANES), :] = tf
        u_s[1, pl.ds(rb, SUBLANES), :] = tb
        return tf[SUBLANES - 1:SUBLANES, :], tb[0:1, :]

    hf, hb = lax.fori_loop(0, n_tiles, carry_step, (hf0, hb0), unroll=4)
    if not has_ctx:
        st_ref[0, 0:1, :] = hf
        st_ref[0, 1:2, :] = hb

    def out_step(c, carry):
        r0 = pl.multiple_of(c * CH, CH)
        g = g_ref[pl.ds(r0, CH), :]
        gelu = g * (0.5 * (1.0 + jnp.tanh(math.sqrt(2.0 / math.pi) * (g + 0.044715 * (g * g * g)))))
        y_ref[pl.ds(r0, CH), :] = (u_s[0, pl.ds(r0, CH), :] + u_s[1, pl.ds(r0, CH), :]) * gelu
        return carry

    lax.fori_loop(0, nc, out_step, 0)


def _rglru(has_ctx, nb, seq_len, layer, x, g, cw, cb, wa, ba, wx, bx, lam, h0=None):
    rows = pl.BlockSpec((seq_len, GROUP_W), lambda b: (b, 0))
    full = lambda a: pl.BlockSpec(a.shape, lambda b: (0,) * a.ndim)
    in_specs = [rows, rows] + [full(a) for a in (cw, cb, wa, ba, wx, bx, lam)]
    args = [x, g, cw, cb, wa, ba, wx, bx, lam]
    y_shape = jax.ShapeDtypeStruct((nb * seq_len, GROUP_W), F32)
    if has_ctx:
        in_specs.append(pl.BlockSpec((1, 1, 2, GROUP_W), lambda b: (b, layer, 0, 0)))
        args.append(h0)
        out_specs, out_shape = rows, y_shape
    else:
        out_specs = [rows, pl.BlockSpec((1, 2, GROUP_W), lambda b: (b, 0, 0))]
        out_shape = [y_shape, jax.ShapeDtypeStruct((nb, 2, GROUP_W), F32)]
    return pl.pallas_call(
        functools.partial(_rglru_kernel, has_ctx, seq_len),
        grid=(nb,),
        in_specs=in_specs,
        out_specs=out_specs,
        out_shape=out_shape,
        scratch_shapes=[pltpu.VMEM((2, seq_len, GROUP_W), F32), pltpu.VMEM((2, seq_len, GROUP_W), F32)],
        compiler_params=_cparams(1),
        name="rglru_ctx" if has_ctx else "rglru",
    )(*args)


ROUTE_OFF = N_EGROUPS
SORT_TM = 256
RUN_PAD = 16
SLOTS = SORT_TM + N_EGROUPS * RUN_PAD
SLOTS_PAD = 384
RUN_BITS = (16, 32, 64, 128, 256)
FFN_BLK = 512
GS_COLS = D_MODEL + LANES


def _routing_gate(logits):
    lane = lax.broadcasted_iota(jnp.int32, logits.shape, 1)
    lane_f = lane.astype(F32)
    neg = -jnp.inf
    big = float(LANES)
    gl = jnp.where(lane < N_EGROUPS, logits, neg)
    gmax = jnp.max(gl, axis=-1, keepdims=True)
    g_w = 1.0 / jnp.sum(jnp.exp(gl - gmax), axis=-1, keepdims=True)
    g_sel = jnp.min(jnp.where(gl == gmax, lane_f, big), axis=-1, keepdims=True)
    e_lane = lane - ROUTE_OFF
    in_grp = (e_lane >= 0) & (e_lane < N_EXPERTS) & ((e_lane // N_EPG).astype(F32) == g_sel)
    el = jnp.where(in_grp, logits, neg)
    m1 = jnp.max(el, axis=-1, keepdims=True)
    i1 = jnp.min(jnp.where(el == m1, lane_f, big), axis=-1, keepdims=True)
    el2 = jnp.where(lane_f == i1, neg, el)
    m2 = jnp.max(el2, axis=-1, keepdims=True)
    i2 = jnp.min(jnp.where(el2 == m2, lane_f, big), axis=-1, keepdims=True)
    r = jnp.exp(m2 - m1)
    p1 = 1.0 / (1.0 + r)
    p2 = r / (1.0 + r)
    return jnp.where(lane_f == i1, g_w * p1, jnp.where(lane_f == i2, g_w * p2, 0.0)), g_sel


def _route_kernel(x_ref, m_ref, mix0, mix1, mix2, mix3, wout_ref, g2_ref, wr_ref, br_ref,
                  x1_ref, h2_ref, gate_ref, cnt_ref, slot_ref):
    tm = SORT_TM
    mix = jnp.concatenate([mix0[...], mix1[...], mix2[...], mix3[...]], axis=1).astype(BF16)
    x1 = x_ref[...] + m_ref[0, 2:3] * _dot(mix, wout_ref[...])
    x1_ref[...] = x1
    h2 = _rms(x1) * g2_ref[...] * (1.0 + m_ref[0, 4:5]) + m_ref[0, 3:4]
    h2_hi = h2.astype(BF16)
    h2_ref[...] = h2_hi
    h2_lo = (h2 - h2_hi.astype(F32)).astype(BF16)
    hi = _dot(h2_hi, wr_ref[...])
    logits = hi[:, :LANES] + hi[:, LANES:] + _dot(h2_lo, wr_ref[:, :LANES]) + br_ref[...]
    gate, g_sel = _routing_gate(logits)
    gate_ref[...] = gate

    lane = lax.broadcasted_iota(jnp.int32, (tm, LANES), 1).astype(F32)
    onehot = jnp.where(lane == g_sel, 1.0, 0.0)
    cnt_ref[0] = jnp.sum(onehot, axis=0, keepdims=True)
    onehot_t = onehot.T
    earlier = jnp.where(lax.broadcasted_iota(jnp.int32, (tm, tm), 0) < lax.broadcasted_iota(jnp.int32, (tm, tm), 1),
                        1.0, 0.0).astype(BF16)
    rank_t = _dot(onehot_t.astype(BF16), earlier)
    cnt = jnp.sum(onehot_t, axis=1, keepdims=True)
    padded = jnp.ceil(cnt * (1.0 / RUN_PAD)) * RUN_PAD
    grp = lax.broadcasted_iota(jnp.int32, (LANES, 1), 0)
    start = jnp.zeros((LANES, 1), F32)
    for g in range(N_EGROUPS - 1):
        start = start + jnp.where(grp > g, padded[g:g + 1, :], 0.0)
    slot = jnp.sum(onehot_t * (rank_t + start), axis=0, keepdims=True)
    slot_ref[0] = slot


def _route(x, mod, mod_row, mixes, wout, g2, wr, br):
    t = x.shape[0]
    tm = SORT_TM
    nt = t // tm
    tok = lambda n: pl.BlockSpec((tm, n), lambda i: (i, 0))
    const = lambda a: pl.BlockSpec(a.shape, lambda i: (0,) * a.ndim)
    return pl.pallas_call(
        _route_kernel,
        grid=(nt,),
        in_specs=[tok(D_MODEL), pl.BlockSpec((1, 6, D_MODEL), lambda i: (mod_row(i), 0, 0)),
                  tok(GROUP_W), tok(GROUP_W), tok(GROUP_W), tok(GROUP_W),
                  const(wout), const(g2), const(wr), const(br)],
        out_specs=[tok(D_MODEL), tok(D_MODEL), tok(LANES),
                   pl.BlockSpec((1, 1, LANES), lambda i: (i, 0, 0)), pl.BlockSpec((1, 1, tm), lambda i: (i, 0, 0))],
        out_shape=[jax.ShapeDtypeStruct((t, D_MODEL), F32), jax.ShapeDtypeStruct((t, D_MODEL), BF16),
                   jax.ShapeDtypeStruct((t, LANES), F32),
                   jax.ShapeDtypeStruct((nt, 1, LANES), F32), jax.ShapeDtypeStruct((nt, 1, tm), F32)],
        compiler_params=_cparams(1),
        name="route",
    )(x, mod, *mixes, wout, g2, wr, br)


def _for_each_piece(n_rows, fn):
    off = 0
    for bit in RUN_BITS:
        has = (n_rows & bit) != 0
        pl.when(has)(functools.partial(fn, off, bit))
        off = off + jnp.where(has, bit, 0)


def _regroup_kernel(loc0_ref, len_ref, dst0_ref, tail0_ref, tail_len_ref, slack_ref, h2_ref, gate_ref, slot_ref,
                    gs_hbm, srt_s, zero_s, sem, zsem):
    i = pl.program_id(0)
    n = pl.num_programs(0)
    tm = SORT_TM
    start, wait = (lambda cp: cp.start()), (lambda cp: cp.wait())

    def zero_pieces(do):
        def tail_copy(g, off, size):
            return pltpu.make_async_copy(zero_s.at[pl.ds(0, size)],
                                         gs_hbm.at[pl.ds(pl.multiple_of(tail0_ref[g] + off, RUN_PAD), size)], zsem)

        def slack_copy(j):
            size = RUN_BITS[-1]
            return pltpu.make_async_copy(zero_s, gs_hbm.at[pl.ds(pl.multiple_of(slack_ref[0] + j * size, size), size)], zsem)

        for g in range(N_EGROUPS):
            _for_each_piece(tail_len_ref[g], lambda off, size, g=g: do(tail_copy(g, off, size)))
        lax.fori_loop(0, slack_ref[1], lambda j, c: (do(slack_copy(j)), c)[1], 0)

    def run_pieces(tile, do):
        buf = tile % 2
        for g in range(N_EGROUPS):
            r = tile * N_EGROUPS + g

            def piece(off, size, r=r):
                do(pltpu.make_async_copy(
                    srt_s.at[buf, pl.ds(pl.multiple_of(loc0_ref[r] + off, RUN_PAD), size)],
                    gs_hbm.at[pl.ds(pl.multiple_of(dst0_ref[r] + off, RUN_PAD), size)], sem.at[buf]))

            _for_each_piece(len_ref[r], piece)

    @pl.when(i == 0)
    def _zeros():
        zero_s[...] = jnp.zeros_like(zero_s)
        zero_pieces(start)

    @pl.when(i >= 2)
    def _reuse():
        run_pieces(i - 2, wait)

    perm = lax.broadcasted_iota(jnp.int32, (SLOTS, tm), 0).astype(F32) == slot_ref[0]
    perm = jnp.where(perm, 1.0, 0.0).astype(BF16)
    srt_h = _dot(perm, h2_ref[...])
    gate = gate_ref[...]
    g_hi = gate.astype(BF16)
    g_mid = (gate - g_hi.astype(F32)).astype(BF16)
    g_lo = (gate - g_hi.astype(F32) - g_mid.astype(F32)).astype(BF16)
    parts = _dot(perm, jnp.concatenate([g_hi, g_mid], axis=1))
    srt_g = parts[:, :LANES] + parts[:, LANES:] + _dot(perm, g_lo)
    srt_s[i % 2] = jnp.concatenate([srt_h, srt_g], axis=1)
    run_pieces(i, start)

    @pl.when(i == n - 1)
    def _drain():
        pl.when(i >= 1)(lambda: run_pieces(i - 1, wait))
        run_pieces(i, wait)
        zero_pieces(wait)


def _regroup(h2, gate, slot, loc0, length, dst0, tail0, tail_len, slack, n_rows):
    t = h2.shape[0]
    tm = SORT_TM
    return pl.pallas_call(
        _regroup_kernel,
        grid_spec=pltpu.PrefetchScalarGridSpec(
            num_scalar_prefetch=6,
            grid=(t // tm,),
            in_specs=[pl.BlockSpec((tm, D_MODEL), lambda i, *_: (i, 0)),
                      pl.BlockSpec((tm, LANES), lambda i, *_: (i, 0)),
                      pl.BlockSpec((1, 1, tm), lambda i, *_: (i, 0, 0))],
            out_specs=pl.BlockSpec(memory_space=pl.ANY),
            scratch_shapes=[pltpu.VMEM((2, SLOTS, GS_COLS), F32), pltpu.VMEM((RUN_BITS[-1], GS_COLS), F32),
                            pltpu.SemaphoreType.DMA((2,)), pltpu.SemaphoreType.DMA(())],
        ),
        out_shape=jax.ShapeDtypeStruct((n_rows, GS_COLS), F32),
        compiler_params=_cparams(1),
        name="regroup",
    )(loc0, length, dst0, tail0, tail_len, slack, h2, gate, slot)


def _ffn_kernel(blk_grp_ref, n_valid_ref, gs_ref, wg_ref, wu_ref, wd_ref, ys_ref):
    b = pl.program_id(0)

    @pl.when(b >= n_valid_ref[0])
    def _unused():
        ys_ref[...] = jnp.zeros_like(ys_ref)

    @pl.when(b < n_valid_ref[0])
    def _block():
        grp = blk_grp_ref[b]
        xs = gs_ref[:, :D_MODEL].astype(BF16)
        gates = gs_ref[:, D_MODEL:]
        lane = lax.broadcasted_iota(jnp.int32, gates.shape, 1)
        hid = []
        for j in range(N_EPG):
            gcol = jnp.sum(jnp.where(lane == grp * N_EPG + j + ROUTE_OFF, gates, 0.0), axis=-1, keepdims=True)
            hj = _silu(_dot(xs, wg_ref[j])) * _dot(xs, wu_ref[j])
            hid.append((hj * gcol).astype(BF16))
        ys_ref[...] = _dot(jnp.concatenate(hid, axis=1), wd_ref[...]).astype(BF16)


def _group_ffn(gs, blk_grp, n_valid, wg, wu, wd):
    nblk = gs.shape[0] // FFN_BLK
    live = lambda b, bg, nv: jnp.minimum(b, nv[0] - 1)
    return pl.pallas_call(
        _ffn_kernel,
        grid_spec=pltpu.PrefetchScalarGridSpec(
            num_scalar_prefetch=2,
            grid=(nblk,),
            in_specs=[
                pl.BlockSpec((FFN_BLK, GS_COLS), lambda b, bg, nv: (live(b, bg, nv), 0)),
                pl.BlockSpec((N_EPG, D_MODEL, EXPERT_FF), lambda b, bg, nv: (bg[b], 0, 0)),
                pl.BlockSpec((N_EPG, D_MODEL, EXPERT_FF), lambda b, bg, nv: (bg[b], 0, 0)),
                pl.BlockSpec((N_EPG * EXPERT_FF, D_MODEL), lambda b, bg, nv: (bg[b], 0)),
            ],
            out_specs=pl.BlockSpec((FFN_BLK, D_MODEL), lambda b, bg, nv: (b, 0)),
        ),
        out_shape=jax.ShapeDtypeStruct((nblk * FFN_BLK, D_MODEL), BF16),
        compiler_params=_cparams(1),
        name="group_ffn",
    )(blk_grp, n_valid, gs, wg, wu, wd)


def _combine_kernel(final, loc0_ref, len_ref, dst0_ref, x1_ref, m_ref, slot_ref, gfin_ref, ys_hbm, o_ref, run_s, sem):
    i = pl.program_id(0)
    tm = SORT_TM

    def fetch(tile, do):
        buf = tile % 2
        for g in range(N_EGROUPS):
            r = tile * N_EGROUPS + g

            def piece(off, size, r=r):
                do(pltpu.make_async_copy(
                    ys_hbm.at[pl.ds(pl.multiple_of(dst0_ref[r] + off, RUN_PAD), size)],
                    run_s.at[buf, pl.ds(pl.multiple_of(loc0_ref[r] + off, RUN_PAD), size)], sem.at[buf]))

            _for_each_piece(len_ref[r], piece)

    @pl.when(i == 0)
    def _first():
        fetch(i, lambda cp: cp.start())

    @pl.when(i + 1 < pl.num_programs(0))
    def _prefetch():
        fetch(i + 1, lambda cp: cp.start())

    fetch(i, lambda cp: cp.wait())

    used = loc0_ref[i * N_EGROUPS + N_EGROUPS - 1] + len_ref[i * N_EGROUPS + N_EGROUPS - 1]
    rows = lax.broadcasted_iota(jnp.int32, (SLOTS_PAD, D_MODEL), 0)
    y_run = jnp.where(rows < used, run_s[i % 2], jnp.zeros((), BF16))
    slot_b = jnp.broadcast_to(slot_ref[0], (LANES, tm)).T
    lane = lax.broadcasted_iota(jnp.int32, (tm, LANES), 1).astype(F32)
    inv = jnp.concatenate([jnp.where(slot_b == lane + float(c), 1.0, 0.0) for c in range(0, SLOTS_PAD, LANES)],
                          axis=1).astype(BF16)
    x2 = x1_ref[...] + m_ref[0, 5:6] * _dot(inv, y_run)
    o_ref[...] = _rms(x2) * gfin_ref[...] if final else x2


def _combine(final, x1, mod, mod_row, slot, gfin, ys, loc0, length, dst0):
    t = x1.shape[0]
    tm = SORT_TM
    return pl.pallas_call(
        functools.partial(_combine_kernel, final),
        grid_spec=pltpu.PrefetchScalarGridSpec(
            num_scalar_prefetch=3,
            grid=(t // tm,),
            in_specs=[
                pl.BlockSpec((tm, D_MODEL), lambda i, *_: (i, 0)),
                pl.BlockSpec((1, 6, D_MODEL), lambda i, *_: (mod_row(i), 0, 0)),
                pl.BlockSpec((1, 1, tm), lambda i, *_: (i, 0, 0)),
                pl.BlockSpec((1, D_MODEL), lambda i, *_: (0, 0)),
                pl.BlockSpec(memory_space=pl.ANY),
            ],
            out_specs=pl.BlockSpec((tm, D_MODEL), lambda i, *_: (i, 0)),
            scratch_shapes=[pltpu.VMEM((2, SLOTS_PAD, D_MODEL), BF16), pltpu.SemaphoreType.DMA((2,))],
        ),
        out_shape=jax.ShapeDtypeStruct((t, D_MODEL), F32),
        compiler_params=_cparams(1),
        name="combine",
    )(loc0, length, dst0, x1, mod, slot, gfin, ys)


def _outproj_moe(final, x, mod, mod_row, mixes, wout, g2, wr, br, wg, wu, wd, gfin):
    t = x.shape[0]
    nt = t // SORT_TM
    x1, h2, gate, cnt, slot = _route(x, mod, mod_row, mixes, wout, g2, wr, br)
    cnt = cnt[:, 0, :N_EGROUPS].astype(jnp.int32)
    length = (cnt + RUN_PAD - 1) // RUN_PAD * RUN_PAD
    loc0 = jnp.cumsum(length, axis=1) - length
    g_rows = jnp.sum(length, axis=0)
    g_blocks = (g_rows + FFN_BLK - 1) // FFN_BLK
    g_base = (jnp.cumsum(g_blocks) - g_blocks) * FFN_BLK
    dst0 = g_base[None, :] + jnp.cumsum(length, axis=0) - length
    n_blocks = (t + nt * N_EGROUPS * (RUN_PAD - 1) + FFN_BLK - 1) // FFN_BLK + N_EGROUPS
    blk_grp = jnp.minimum(jnp.sum(jnp.arange(n_blocks)[:, None] >= jnp.cumsum(g_blocks)[None, :], axis=1),
                          N_EGROUPS - 1).astype(jnp.int32)
    n_valid = jnp.sum(g_blocks).astype(jnp.int32)[None]
    flat = lambda a: a.reshape(-1).astype(jnp.int32)
    slack = jnp.stack([n_valid[0] * FFN_BLK, (n_blocks - n_valid[0]) * (FFN_BLK // RUN_BITS[-1])]).astype(jnp.int32)
    gs = _regroup(h2, gate, slot, flat(loc0), flat(length), flat(dst0), flat(g_base + g_rows),
                  flat(g_blocks * FFN_BLK - g_rows), slack, n_blocks * FFN_BLK)
    ys = _group_ffn(gs, blk_grp, n_valid, wg, wu, wd)
    return _combine(final, x1, mod, mod_row, slot, gfin, ys, flat(loc0), flat(length), flat(dst0))


def _block_diag(w):
    n, k, _ = w.shape
    eye = jnp.eye(n, dtype=w.dtype)
    return (eye[:, None, :, None] * w[:, :, None, :]).reshape(n * k, n * k)


def _pad_lanes(v, n=LANES):
    return jnp.pad(v, ((0, 0), (0, n - v.shape[-1])))


def kernel(x_prompt, x_sample, c, cache_k, cache_v, state_ssd, state_rglru, c_ctx, w_mod, b_mod, norm1_g, norm2_g, w_in, w_out, ssd_conv_w, ssd_conv_b, ssd_dt_bias, ssd_a_log, ssd_d, ssd_norm_g, pool_w, pool_scale, da_lam_q1, da_lam_k1, da_lam_q2, da_lam_k2, da_norm_g, rg_conv_w, rg_conv_b, rg_wa, rg_ba, rg_wx, rg_bx, rg_lambda, moe_w_group, moe_b_group, moe_w_expert, moe_b_expert, moe_w_gate, moe_w_up, moe_w_down, final_norm_g):
    nbp, lp, _ = x_prompt.shape
    nbs, ls, _ = x_sample.shape
    past = cache_k.shape[2]
    tm_in = 256
    assert nbs + 1 <= MOD_ROWS and lp % CH == 0 and ls % CH == 0
    assert lp == tm_in and ls % tm_in == 0 and lp % SORT_TM == 0 and ls % SORT_TM == 0
    assert lp % KEY_BLK == 0 and ls % KEY_BLK == 0 and past % KEY_BLK == 0

    cond = jnp.concatenate([c_ctx[None, :], c, jnp.zeros((MOD_ROWS - 1 - nbs, D_MODEL), F32)], axis=0)
    mod = _modulation(cond, w_mod, b_mod).reshape(DEPTH * MOD_ROWS, 6, D_MODEL)

    w_in_r = jnp.concatenate([w_in[:, :, :DT_LO], w_in[:, :, DT_HI:], w_in[:, :, DT_LO:DT_HI],
                              jnp.zeros((DEPTH, D_MODEL, LANES - (DT_HI - DT_LO)), F32)], axis=-1).astype(BF16)
    w_out_b = w_out.astype(BF16)
    w_gate_b = moe_w_gate.astype(BF16)
    w_up_b = moe_w_up.astype(BF16)
    w_down_b = moe_w_down.astype(BF16).reshape(DEPTH, N_EXPERTS * EXPERT_FF, D_MODEL)
    w_route = jnp.concatenate([moe_w_group, moe_w_expert,
                               jnp.zeros((DEPTH, D_MODEL, LANES - N_EGROUPS - N_EXPERTS), F32)], axis=-1)
    w_route_hi = w_route.astype(BF16)
    w_route = jnp.concatenate([w_route_hi, (w_route - w_route_hi.astype(F32)).astype(BF16)], axis=-1)
    b_route = _pad_lanes(jnp.concatenate([moe_b_group, moe_b_expert], axis=-1))
    dtb = _pad_lanes(ssd_dt_bias.reshape(DEPTH, 2 * SSD_HEADS))
    alog = _pad_lanes(ssd_a_log.reshape(DEPTH, 2 * SSD_HEADS))
    d_skip = jnp.repeat(ssd_d, SSD_HEADDIM, axis=-1)
    cos, sin = _rope_tables(ls)
    ck = cache_k.reshape(nbs, DEPTH, past, GROUP_W)
    cv = cache_v.reshape(nbs, DEPTH, past, GROUP_W)
    g_fin = final_norm_g[None, :]

    xp = x_prompt.reshape(nbp * lp, D_MODEL)
    xs = x_sample.reshape(nbs * ls, D_MODEL)
    new_k = jnp.zeros((nbp, DEPTH, lp, GROUP_W), F32)
    new_v = jnp.zeros((nbp, DEPTH, lp, GROUP_W), F32)
    ssd_out, rg_out = [], []
    for l in range(DEPTH):
        row1 = lambda a: a[l][None, :]
        final = l == DEPTH - 1
        lam_init = 0.8 - 0.6 * math.exp(-0.3 * l)
        ssd_w = (ssd_conv_w[l], row1(ssd_conv_b), row1(dtb), row1(alog), row1(d_skip), row1(ssd_norm_g))
        pool_wb = _block_diag(pool_w[l])
        att_w = (row1(da_lam_q1), row1(da_lam_k1), row1(da_lam_q2), row1(da_lam_k2), row1(da_norm_g))
        rg_w = (rg_conv_w[l], row1(rg_conv_b),
                jnp.stack([_block_diag(rg_wa[l, 0]), _block_diag(rg_wa[l, 1])]), rg_ba[l][:, None, :],
                jnp.stack([_block_diag(rg_wx[l, 0]), _block_diag(rg_wx[l, 1])]), rg_bx[l][:, None, :],
                rg_lambda[l][:, None, :])
        moe_w = (w_out_b[l], row1(norm2_g), w_route[l], row1(b_route), w_gate_b[l], w_up_b[l], w_down_b[l], g_fin)
        ctx_row = lambda i, l=l: l * MOD_ROWS
        lat_row = lambda tm: (lambda i, l=l: l * MOD_ROWS + 1 + i // (ls // tm))
        assert tm_in == SORT_TM

        xbc, z, xpool, q, new_k, new_v, xr, gr, dt = _inproj(
            xp, mod, ctx_row, row1(norm1_g), w_in_r[l], tm_in, l, caches=(new_k, new_v))
        ya, st_ssd = _ssd(False, nbp, lp, l, xbc, z, dt, *ssd_w)
        yb = _pool(nbp, lp, xpool, pool_wb, row1(pool_scale))
        yc = _attn(False, nbp, lp, l, lam_init, q, new_k, new_v, *att_w)
        yd, st_rg = _rglru(False, nbp, lp, l, xr, gr, *rg_w)
        xp = _outproj_moe(final, xp, mod, ctx_row, (ya, yb, yc, yd), *moe_w)
        ssd_out.append(st_ssd)
        rg_out.append(st_rg)

        xbc, z, xpool, q, k, v, xr, gr, dt = _inproj(xs, mod, lat_row(tm_in), row1(norm1_g), w_in_r[l], tm_in, l)
        ya = _ssd(True, nbs, ls, l, xbc, z, dt, *ssd_w, h0=state_ssd)
        yb = _pool(nbs, ls, xpool, pool_wb, row1(pool_scale))
        yc = _attn(True, nbs, ls, l, lam_init, q, k, v, *att_w, ck=ck, cv=cv, cos=cos, sin=sin)
        yd = _rglru(True, nbs, ls, l, xr, gr, *rg_w, h0=state_rglru)
        xs = _outproj_moe(final, xs, mod, lat_row(SORT_TM), (ya, yb, yc, yd), *moe_w)

    return (xp.reshape(nbp, lp, D_MODEL), xs.reshape(nbs, ls, D_MODEL),
            new_k.reshape(nbp, DEPTH, lp, DA_HEADS, 2 * DA_QKDIM), new_v.reshape(nbp, DEPTH, lp, DA_HEADS, DA_VDIM),
            jnp.stack(ssd_out, axis=1), jnp.stack(rg_out, axis=1))
```

```python
import functools
import math

import numpy as np
import jax
import jax.numpy as jnp
from jax import lax
from jax.experimental import pallas as pl
from jax.experimental.pallas import tpu as pltpu

F32 = jnp.float32
BF16 = jnp.bfloat16
HIGHEST = lax.Precision.HIGHEST

D_MODEL = 1024
DEPTH = 4
GRID_W = 64
GROUP_W = 256
EPS = 1e-6
SSD_HEADDIM = 64
SSD_HEADS = 4
SSD_STATE = 64
SSD_BC = 128
SSD_CONV_CH = 512
POOL_WINDOWS = (2, 4, 8, 16)
DA_HEADS = 4
DA_VDIM = 64
DA_QKDIM = 32
ROPE_BASE = 10000.0
RG_C = 8.0
N_EGROUPS = 4
N_EPG = 4
N_EXPERTS = 16
EXPERT_FF = 256
DT_LO, DT_HI = 768, 776

LANES = 128
SUBLANES = 8
CH = 128
HALO = SUBLANES
MOD_ROWS = 16
VMEM_LIMIT = 56 * 1024 * 1024


def _cparams(n_axes):
    return pltpu.CompilerParams(dimension_semantics=("arbitrary",) * n_axes, vmem_limit_bytes=VMEM_LIMIT)


def _silu(x):
    return x * jax.nn.sigmoid(x)


def _softplus(x):
    return jnp.maximum(x, 0.0) + jnp.log1p(jnp.exp(-jnp.abs(x)))


def _dot(a, b, **kw):
    return jnp.dot(a, b, preferred_element_type=F32, **kw)


def _dot_nt(a, b):
    return lax.dot_general(a, b, (((1,), (1,)), ((), ())), preferred_element_type=F32)


def _rms(x):
    return x * lax.rsqrt(jnp.mean(x * x, axis=-1, keepdims=True) + EPS)


def _window(ref, c, n_steps, seq_len):
    r0 = pl.multiple_of(c * CH, CH)
    main = ref[pl.ds(r0, CH), :]
    lo = pl.multiple_of(jnp.maximum(r0 - HALO, 0), HALO)
    hi = pl.multiple_of(jnp.minimum(r0 + CH, seq_len - HALO), HALO)
    prev = jnp.where(c > 0, ref[pl.ds(lo, HALO), :], 0.0)
    nxt = jnp.where(c < n_steps - 1, ref[pl.ds(hi, HALO), :], 0.0)
    return jnp.concatenate([prev, main, nxt], axis=0)


def _conv4(win, w_ref, b_ref):
    acc = b_ref[...]
    for k in range(4):
        acc = acc + w_ref[k:k + 1, :] * win[HALO - 1 + k:HALO - 1 + k + CH, :]
    return acc


def _mod_kernel(cond_ref, w_ref, b_ref, o_ref):
    cnd = cond_ref[...]
    o_ref[0] = _dot(_silu(cnd), w_ref[0], precision=HIGHEST) + b_ref[0]


def _modulation(cond, w_mod, b_mod):
    nb = 6
    return pl.pallas_call(
        _mod_kernel,
        grid=(DEPTH, nb),
        in_specs=[
            pl.BlockSpec((MOD_ROWS, D_MODEL), lambda l, j: (0, 0)),
            pl.BlockSpec((1, D_MODEL, D_MODEL), lambda l, j: (l, 0, j)),
            pl.BlockSpec((1, 1, D_MODEL), lambda l, j: (l, 0, j)),
        ],
        out_specs=pl.BlockSpec((1, MOD_ROWS, D_MODEL), lambda l, j: (l, 0, j)),
        out_shape=jax.ShapeDtypeStruct((DEPTH, MOD_ROWS, nb * D_MODEL), F32),
        compiler_params=_cparams(2),
        name="modulation",
    )(cond, w_mod, b_mod.reshape(DEPTH, 1, nb * D_MODEL))


IN_COLS = (512, 256, 256, 256, 256, 256, 256, 256, LANES)
K_OUT, V_OUT = 4, 5


def _inproj_kernel(to_cache, x_ref, m_ref, g_ref, w_ref, *refs):
    if to_cache:
        refs = refs[2:]
    hh = _rms(x_ref[...]) * g_ref[...] * (1.0 + m_ref[0, 1:2]) + m_ref[0, 0:1]
    u = _dot(hh.astype(BF16), w_ref[...])
    off = 0
    for j, (ref, n) in enumerate(zip(refs, IN_COLS)):
        if to_cache and j in (K_OUT, V_OUT):
            ref[:, 0] = u[:, off:off + n].reshape(ref.shape[0], ref.shape[2], n)
        else:
            ref[...] = u[:, off:off + n]
        off += n


def _inproj(x, mod, mod_row, g, w, tm, layer, caches=None):
    t = x.shape[0]
    ncol = sum(IN_COLS)
    to_cache = caches is not None
    in_specs = [
        pl.BlockSpec((tm, D_MODEL), lambda i: (i, 0)),
        pl.BlockSpec((1, 6, D_MODEL), lambda i: (mod_row(i), 0, 0)),
        pl.BlockSpec((1, D_MODEL), lambda i: (0, 0)),
        pl.BlockSpec((D_MODEL, ncol), lambda i: (0, 0)),
    ]
    out_specs = [pl.BlockSpec((tm, n), lambda i: (i, 0)) for n in IN_COLS]
    out_shape = [jax.ShapeDtypeStruct((t, n), F32) for n in IN_COLS]
    args = [x, mod, g, w]
    aliases = {}
    if to_cache:
        seq = caches[0].shape[2]
        assert tm % seq == 0
        cache_spec = pl.BlockSpec((tm // seq, 1, seq, GROUP_W), lambda i: (i, layer, 0, 0))
        for j, cch in zip((K_OUT, V_OUT), caches):
            in_specs.append(pl.BlockSpec(memory_space=pl.ANY))
            aliases[len(args)] = j
            args.append(cch)
            out_specs[j] = cache_spec
            out_shape[j] = jax.ShapeDtypeStruct(cch.shape, F32)
    return pl.pallas_call(
        functools.partial(_inproj_kernel, to_cache),
        grid=(t // tm,),
        in_specs=in_specs,
        out_specs=out_specs,
        out_shape=out_shape,
        input_output_aliases=aliases,
        compiler_params=_cparams(1),
        name="inproj_ctx" if to_cache else "inproj",
    )(*args)


def _ssd_kernel(has_ctx, seq_len, xbc_ref, z_ref, dt_ref, cw_ref, cb_ref, dtb_ref, alog_ref, d_ref, ng_ref, *rest):
    if has_ctx:
        h0_ref, y_ref, xc_s, y_s, st_s, upd_s, grow_s, keep_s = rest
        st_ref = None
    else:
        y_ref, st_ref, xc_s, y_s, st_s, upd_s, grow_s, keep_s = rest
    nc = seq_len // CH
    hd = SSD_HEADDIM
    a_row = -jnp.exp(alog_ref[...])
    row = lax.broadcasted_iota(jnp.int32, (CH, CH), 0)
    col = lax.broadcasted_iota(jnp.int32, (CH, CH), 1)
    lane_w = lax.broadcasted_iota(jnp.int32, (CH, GROUP_W), 1)
    lane_n = lax.broadcasted_iota(jnp.int32, (CH, SSD_BC), 1)
    own_block = (lax.broadcasted_iota(jnp.int32, (SSD_BC, GROUP_W), 0) // SSD_STATE
                 == lax.broadcasted_iota(jnp.int32, (SSD_BC, GROUP_W), 1) // (2 * hd))

    def conv_step(c, carry):
        r0 = pl.multiple_of(c * CH, CH)
        xc = _silu(_conv4(_window(xbc_ref, c, nc, seq_len), cw_ref, cb_ref))
        xc_s[pl.ds(r0, CH), :] = xc
        y_s[pl.ds(r0, CH), :] = xc[:, :GROUP_W] * d_ref[...]
        return carry

    lax.fori_loop(0, nc, conv_step, 0)

    st_s[...] = jnp.zeros_like(st_s)
    if has_ctx:
        for d in range(2):
            for h in range(SSD_HEADS):
                g = h // 2
                st_s[d, g * SSD_STATE:(g + 1) * SSD_STATE, h * hd:(h + 1) * hd] = h0_ref[0, 0, d, h].T

    def per_head(v, d):
        lanes = lane_w[:v.shape[0]]
        out = jnp.broadcast_to(v[:, 4 * d + 3:4 * d + 4], (v.shape[0], GROUP_W))
        for h in (2, 1, 0):
            out = jnp.where(lanes < (h + 1) * hd, jnp.broadcast_to(v[:, 4 * d + h:4 * d + h + 1], out.shape), out)
        return out

    def local_step(c, carry):
        r0 = pl.multiple_of(c * CH, CH)
        xc = xc_s[pl.ds(r0, CH), :]
        x = xc[:, :GROUP_W]
        bm = xc[:, GROUP_W:GROUP_W + SSD_BC]
        cm = xc[:, GROUP_W + SSD_BC:]
        dt = _softplus(dt_ref[pl.ds(r0, CH), :] + dtb_ref[...])
        bmb = bm.astype(BF16)
        bm_t = bm.T.astype(BF16)
        scores = [_dot_nt(jnp.where((lane_n // SSD_STATE) == g, cm, 0.0).astype(BF16), bmb) for g in range(2)]
        y_diag = None
        da = dt * a_row
        pre = da
        for k in (1, 2, 4, 8, 16, 32, 64):
            pre = pre + jnp.where(row >= k, pltpu.roll(pre, k, axis=0), 0.0)
        for d in range(2):
            tri = (row >= col) if d == 0 else (row <= col)
            cs = pre if d == 0 else pre[CH - 1:CH, :] - pre + da
            cs_t = cs.T
            tot = cs[CH - 1:CH, :] if d == 0 else cs[0:1, :]
            dt_e, cs_e, tot_e = per_head(dt, d), per_head(cs, d), per_head(tot, d)
            xdt = x * dt_e
            m_parts, r_parts = [], []
            for h in range(SSD_HEADS):
                k = SSD_HEADS * d + h
                decay = jnp.exp(jnp.where(tri, cs[:, k:k + 1] - cs_t[k:k + 1, :], -jnp.inf))
                m_parts.append((scores[h // 2] * decay).astype(BF16))
                r_parts.append(jnp.where((lane_w // hd) == h, xdt, 0.0).astype(BF16))
            yd = _dot(jnp.concatenate(m_parts, axis=1), jnp.concatenate(r_parts, axis=0))
            y_diag = yd if y_diag is None else y_diag + yd
            wgt = xdt * jnp.exp(tot_e - cs_e)
            upd_s[d, c] = jnp.where(own_block, _dot(bm_t, wgt.astype(BF16)), 0.0)
            grow_s[d, pl.ds(r0, CH), :] = jnp.exp(cs_e)
            keep_s[d, pl.ds(c, 1), :] = jnp.exp(tot_e)
        y_s[pl.ds(r0, CH), :] += y_diag
        return carry

    lax.fori_loop(0, nc, local_step, 0, unroll=2)

    def state_dir(d, ci):
        r0 = pl.multiple_of(ci * CH, CH)
        st = st_s[d]
        cmb = xc_s[pl.ds(r0, CH), GROUP_W + SSD_BC:].astype(BF16)
        y_s[pl.ds(r0, CH), :] += _dot(cmb, st.astype(BF16)) * grow_s[d, pl.ds(r0, CH), :]
        st_s[d] = st * keep_s[d, pl.ds(ci, 1), :] + upd_s[d, ci]

    def scan_step(c, carry):
        state_dir(0, c)
        state_dir(1, nc - 1 - c)
        return carry

    lax.fori_loop(0, nc, scan_step, 0, unroll=2)

    def out_step(c, carry):
        r0 = pl.multiple_of(c * CH, CH)
        y = y_s[pl.ds(r0, CH), :] * _silu(z_ref[pl.ds(r0, CH), :])
        y_ref[pl.ds(r0, CH), :] = _rms(y) * ng_ref[...]
        return carry

    lax.fori_loop(0, nc, out_step, 0)
    if not has_ctx:
        for d in range(2):
            for h in range(SSD_HEADS):
                g = h // 2
                st_ref[0, d, h] = st_s[d, g * SSD_STATE:(g + 1) * SSD_STATE, h * hd:(h + 1) * hd].T


def _ssd(has_ctx, nb, seq_len, layer, xbc, z, dt, cw, cb, dtb, alog, dsk, ng, h0=None):
    rows = lambda n: pl.BlockSpec((seq_len, n), lambda b: (b, 0))
    full = lambda a: pl.BlockSpec(a.shape, lambda b: (0,) * a.ndim)
    in_specs = [rows(SSD_CONV_CH), rows(GROUP_W), rows(LANES)] + [full(a) for a in (cw, cb, dtb, alog, dsk, ng)]
    args = [xbc, z, dt, cw, cb, dtb, alog, dsk, ng]
    y_spec = rows(GROUP_W)
    y_shape = jax.ShapeDtypeStruct((nb * seq_len, GROUP_W), F32)
    st_blk = (1, 2, SSD_HEADS, SSD_HEADDIM, SSD_STATE)
    if has_ctx:
        in_specs.append(pl.BlockSpec((1, 1) + st_blk[1:], lambda b: (b, layer, 0, 0, 0, 0)))
        args.append(h0)
        out_specs, out_shape = y_spec, y_shape
    else:
        out_specs = [y_spec, pl.BlockSpec(st_blk, lambda b: (b, 0, 0, 0, 0))]
        out_shape = [y_shape, jax.ShapeDtypeStruct((nb,) + st_blk[1:], F32)]
    return pl.pallas_call(
        functools.partial(_ssd_kernel, has_ctx, seq_len),
        grid=(nb,),
        in_specs=in_specs,
        out_specs=out_specs,
        out_shape=out_shape,
        scratch_shapes=[
            pltpu.VMEM((seq_len, SSD_CONV_CH), F32),
            pltpu.VMEM((seq_len, GROUP_W), F32),
            pltpu.VMEM((2, SSD_BC, GROUP_W), F32),
            pltpu.VMEM((2, seq_len // CH, SSD_BC, GROUP_W), F32),
            pltpu.VMEM((2, seq_len, GROUP_W), F32),
            pltpu.VMEM((2, max(seq_len // CH, SUBLANES), GROUP_W), F32),
        ],
        compiler_params=_cparams(1),
        name="ssd_ctx" if has_ctx else "ssd",
    )(*args)


def _pool_kernel(seq_len, x_ref, w_ref, sc_ref, y_ref):
    nc = seq_len // CH
    wn = CH + 2 * HALO
    lane = lax.broadcasted_iota(jnp.int32, (CH, GROUP_W), 1)
    gw = GROUP_W // len(POOL_WINDOWS)
    half = jnp.where(lane < gw, 1, jnp.where(lane < 2 * gw, 2, jnp.where(lane < 3 * gw, 4, 8)))
    w_blk = w_ref[...].astype(BF16)

    def ahead(v, k):
        return pltpu.roll(v, wn - k, axis=0)

    def step(c, carry):
        r0 = pl.multiple_of(c * CH, CH)
        win = _window(x_ref, c, nc, seq_len)
        p2 = win + ahead(win, 1)
        p4 = p2 + ahead(p2, 2)
        p8 = p4 + ahead(p4, 4)
        p16 = p8 + ahead(p8, 8)
        s2 = ahead(p2, HALO - 1)[:CH]
        s4 = ahead(p4, HALO - 2)[:CH]
        s8 = ahead(p8, HALO - 4)[:CH]
        s16 = p16[:CH]
        tot = jnp.where(lane < gw, s2, jnp.where(lane < 2 * gw, s4, jnp.where(lane < 3 * gw, s8, s16)))
        t = r0 + lax.broadcasted_iota(jnp.int32, (CH, GROUP_W), 0)
        cnt = jnp.minimum(t + half, seq_len) - jnp.maximum(t - half, 0)
        x = win[HALO:HALO + CH]
        diff = tot / cnt.astype(F32) - x
        y_ref[pl.ds(r0, CH), :] = _dot(diff.astype(BF16), w_blk) * sc_ref[...]
        return carry

    lax.fori_loop(0, nc, step, 0)


def _pool(nb, seq_len, x, w_blk, scale):
    return pl.pallas_call(
        functools.partial(_pool_kernel, seq_len),
        grid=(nb,),
        in_specs=[
            pl.BlockSpec((seq_len, GROUP_W), lambda b: (b, 0)),
            pl.BlockSpec((GROUP_W, GROUP_W), lambda b: (0, 0)),
            pl.BlockSpec((1, GROUP_W), lambda b: (0, 0)),
        ],
        out_specs=pl.BlockSpec((seq_len, GROUP_W), lambda b: (b, 0)),
        out_shape=jax.ShapeDtypeStruct((nb * seq_len, GROUP_W), F32),
        compiler_params=_cparams(1),
        name="pool",
    )(x, w_blk, scale)


KEY_BLK = 256


def _rope(x, cos, sin):
    c2 = jnp.concatenate([cos, cos], axis=1)
    s2 = jnp.concatenate([sin, sin], axis=1)
    lane = lax.broadcasted_iota(jnp.int32, x.shape, 1)
    n = x.shape[1]
    partner = jnp.where(lane % 2 == 0, pltpu.roll(x, n - 1, axis=1), pltpu.roll(x, 1, axis=1))
    return x * c2 + partner * s2


def _attn_kernel(has_ctx, seq_len, tq, past, lam_init, q_ref, k_ref, v_ref, lq1, lk1, lq2, lk2, ng_ref, *rest):
    if has_ctx:
        ck_ref, cv_ref, cosq_ref, sinq_ref, cosk_ref, sink_ref, o_ref, kt_s, v_s = rest
    else:
        o_ref, kt_s, v_s = rest
    kb = KEY_BLK

    @pl.when(pl.program_id(1) == 0)
    def _prepare_keys():
        def put(dst0, kk, vv):
            kt_s[:, dst0:dst0 + kb] = kk.T.astype(BF16)
            for h in range(DA_HEADS):
                v_s[h, dst0:dst0 + kb, :] = vv[:, h * DA_VDIM:(h + 1) * DA_VDIM].astype(BF16)

        if has_ctx:
            for j in range(past // kb):
                put(j * kb, ck_ref[0, 0, j * kb:(j + 1) * kb, :], cv_ref[0, 0, j * kb:(j + 1) * kb, :])
        for j in range(seq_len // kb):
            kk = k_ref[0, 0, j * kb:(j + 1) * kb, :] if not has_ctx else k_ref[j * kb:(j + 1) * kb, :]
            vv = v_ref[0, 0, j * kb:(j + 1) * kb, :] if not has_ctx else v_ref[j * kb:(j + 1) * kb, :]
            if has_ctx:
                kk = _rope(kk, cosk_ref[j * kb:(j + 1) * kb, :], sink_ref[j * kb:(j + 1) * kb, :])
            put(past + j * kb, kk, vv)

    q = q_ref[...]
    if has_ctx:
        q = _rope(q, cosq_ref[...], sinq_ref[...])
    q = q * (DA_QKDIM ** -0.5 * math.log2(math.e))
    lam = (jnp.exp(jnp.sum(lq1[...] * lk1[...], axis=-1, keepdims=True))
           - jnp.exp(jnp.sum(lq2[...] * lk2[...], axis=-1, keepdims=True)) + lam_init)
    for h in range(DA_HEADS):
        es, sums = [], []
        for m in range(2):
            lo = h * 2 * DA_QKDIM + m * DA_QKDIM
            s = _dot(q[:, lo:lo + DA_QKDIM].astype(BF16), kt_s[lo:lo + DA_QKDIM, :])
            e = jnp.exp2(s - jnp.max(s, axis=-1, keepdims=True))
            sums.append(jnp.sum(e, axis=-1, keepdims=True))
            es.append(e.astype(BF16))
        o = _dot(jnp.concatenate(es, axis=0), v_s[h])
        acc = o[:tq] / sums[0] - lam * (o[tq:] / sums[1])
        o_ref[:, h * DA_VDIM:(h + 1) * DA_VDIM] = _rms(acc) * ng_ref[...] * (1.0 - lam_init)


def _attn(has_ctx, nb, seq_len, layer, lam_init, q, k, v, lq1, lk1, lq2, lk2, ng, ck=None, cv=None, cos=None, sin=None):
    tq = 256
    nq = seq_len // tq
    past = ck.shape[2] if has_ctx else 0
    keys = seq_len + past
    small = lambda a: pl.BlockSpec(a.shape, lambda b, i: (0,) * a.ndim)
    if has_ctx:
        kv_spec = pl.BlockSpec((seq_len, GROUP_W), lambda b, i: (b, 0))
    else:
        kv_spec = pl.BlockSpec((1, 1, seq_len, GROUP_W), lambda b, i: (b, layer, 0, 0))
    in_specs = [pl.BlockSpec((tq, GROUP_W), lambda b, i: (b * nq + i, 0)), kv_spec, kv_spec]
    in_specs += [small(a) for a in (lq1, lk1, lq2, lk2, ng)]
    args = [q, k, v, lq1, lk1, lq2, lk2, ng]
    if has_ctx:
        in_specs += [
            pl.BlockSpec((1, 1, past, GROUP_W), lambda b, i: (b, layer, 0, 0)),
            pl.BlockSpec((1, 1, past, GROUP_W), lambda b, i: (b, layer, 0, 0)),
            pl.BlockSpec((tq, LANES), lambda b, i: (i, 0)),
            pl.BlockSpec((tq, LANES), lambda b, i: (i, 0)),
            pl.BlockSpec((seq_len, LANES), lambda b, i: (0, 0)),
            pl.BlockSpec((seq_len, LANES), lambda b, i: (0, 0)),
        ]
        args += [ck, cv, cos, sin, cos, sin]
    return pl.pallas_call(
        functools.partial(_attn_kernel, has_ctx, seq_len, tq, past, lam_init),
        grid=(nb, nq),
        in_specs=in_specs,
        out_specs=pl.BlockSpec((tq, GROUP_W), lambda b, i: (b * nq + i, 0)),
        out_shape=jax.ShapeDtypeStruct((nb * seq_len, GROUP_W), F32),
        scratch_shapes=[
            pltpu.VMEM((GROUP_W, keys), BF16),
            pltpu.VMEM((DA_HEADS, keys, DA_VDIM), BF16),
        ],
        compiler_params=_cparams(2),
        name="attn_ctx" if has_ctx else "attn",
    )(*args)


def _rope_tables(seq_len):
    t = np.arange(seq_len)
    rowp = (t // GRID_W).astype(np.float64)
    colp = (t % GRID_W).astype(np.float64)
    n_freq = DA_QKDIM // 4
    inv_freq = ROPE_BASE ** (-np.arange(n_freq, dtype=np.float64) / n_freq)
    ang = np.concatenate([rowp[:, None] * inv_freq, colp[:, None] * inv_freq], axis=-1)
    ang = np.repeat(ang, 2, axis=-1)
    sign = np.where(np.arange(DA_QKDIM) % 2 == 0, -1.0, 1.0)
    cos = np.tile(np.cos(ang), (1, LANES // DA_QKDIM)).astype(np.float32)
    sin = np.tile(np.sin(ang) * sign, (1, LANES // DA_QKDIM)).astype(np.float32)
    return jnp.asarray(cos), jnp.asarray(sin)


def _rglru_kernel(has_ctx, seq_len, x_ref, g_ref, cw_ref, cb_ref, wa_ref, ba_ref, wx_ref, bx_ref, lam_ref, *rest):
    if has_ctx:
        h0_ref, y_ref, a_s, u_s = rest
        st_ref = None
    else:
        y_ref, st_ref, a_s, u_s = rest
    nc = seq_len // CH
    nt = CH // SUBLANES
    sub = lax.broadcasted_iota(jnp.int32, (nt, SUBLANES, GROUP_W), 1)

    def gate_step(c, carry):
        r0 = pl.multiple_of(c * CH, CH)
        xc = _conv4(_window(x_ref, c, nc, seq_len), cw_ref, cb_ref)
        xb = xc.astype(BF16)
        for d in range(2):
            rg = jax.nn.sigmoid(_dot(xb, wa_ref[d].astype(BF16)) + ba_ref[d])
            ig = jax.nn.sigmoid(_dot(xb, wx_ref[d].astype(BF16)) + bx_ref[d])
            log_a = -RG_C * rg * _softplus(-lam_ref[d])
            a = jnp.exp(log_a)
            u = jnp.sqrt(-jnp.tanh(log_a) * (a * a + 1.0)) * (ig * xc)
            a3 = a.reshape(nt, SUBLANES, GROUP_W)
            u3 = u.reshape(nt, SUBLANES, GROUP_W)
            for k in (1, 2, 4):
                if d == 0:
                    ok = sub >= k
                    a_sh, u_sh = pltpu.roll(a3, k, axis=1), pltpu.roll(u3, k, axis=1)
                else:
                    ok = sub < SUBLANES - k
                    a_sh, u_sh = pltpu.roll(a3, SUBLANES - k, axis=1), pltpu.roll(u3, SUBLANES - k, axis=1)
                u3 = u3 + a3 * jnp.where(ok, u_sh, 0.0)
                a3 = a3 * jnp.where(ok, a_sh, 1.0)
            a_s[d, pl.ds(r0, CH), :] = a3.reshape(CH, GROUP_W)
            u_s[d, pl.ds(r0, CH), :] = u3.reshape(CH, GROUP_W)
        return carry

    lax.fori_loop(0, nc, gate_step, 0)

    n_tiles = seq_len // SUBLANES
    if has_ctx:
        hf0, hb0 = h0_ref[0, 0, 0:1, :], h0_ref[0, 0, 1:2, :]
    else:
        hf0 = hb0 = jnp.zeros((1, GROUP_W), F32)

    def carry_step(i, carry):
        hf, hb = carry
        rf = pl.multiple_of(i * SUBLANES, SUBLANES)
        rb = pl.multiple_of((n_tiles - 1 - i) * SUBLANES, SUBLANES)
        tf = u_s[0, pl.ds(rf, SUBLANES), :] + a_s[0, pl.ds(rf, SUBLANES), :] * hf
        tb = u_s[1, pl.ds(rb, SUBLANES), :] + a_s[1, pl.ds(rb, SUBLANES), :] * hb
        u_s[0, pl.ds(rf, SUBLANES), :] = tf
        u_s[1, pl.ds(rb, SUBLANES), :] = tb
        return tf[SUBLANES - 1:SUBLANES, :], tb[0:1, :]

    hf, hb = lax.fori_loop(0, n_tiles, carry_step, (hf0, hb0), unroll=4)
    if not has_ctx:
        st_ref[0, 0:1, :] = hf
        st_ref[0, 1:2, :] = hb

    def out_step(c, carry):
        r0 = pl.multiple_of(c * CH, CH)
        g = g_ref[pl.ds(r0, CH), :]
        gelu = g * (0.5 * (1.0 + jnp.tanh(math.sqrt(2.0 / math.pi) * (g + 0.044715 * (g * g * g)))))
        y_ref[pl.ds(r0, CH), :] = (u_s[0, pl.ds(r0, CH), :] + u_s[1, pl.ds(r0, CH), :]) * gelu
        return carry

    lax.fori_loop(0, nc, out_step, 0)


def _rglru(has_ctx, nb, seq_len, layer, x, g, cw, cb, wa, ba, wx, bx, lam, h0=None):
    rows = pl.BlockSpec((seq_len, GROUP_W), lambda b: (b, 0))
    full = lambda a: pl.BlockSpec(a.shape, lambda b: (0,) * a.ndim)
    in_specs = [rows, rows] + [full(a) for a in (cw, cb, wa, ba, wx, bx, lam)]
    args = [x, g, cw, cb, wa, ba, wx, bx, lam]
    y_shape = jax.ShapeDtypeStruct((nb * seq_len, GROUP_W), F32)
    if has_ctx:
        in_specs.append(pl.BlockSpec((1, 1, 2, GROUP_W), lambda b: (b, layer, 0, 0)))
        args.append(h0)
        out_specs, out_shape = rows, y_shape
    else:
        out_specs = [rows, pl.BlockSpec((1, 2, GROUP_W), lambda b: (b, 0, 0))]
        out_shape = [y_shape, jax.ShapeDtypeStruct((nb, 2, GROUP_W), F32)]
    return pl.pallas_call(
        functools.partial(_rglru_kernel, has_ctx, seq_len),
        grid=(nb,),
        in_specs=in_specs,
        out_specs=out_specs,
        out_shape=out_shape,
        scratch_shapes=[pltpu.VMEM((2, seq_len, GROUP_W), F32), pltpu.VMEM((2, seq_len, GROUP_W), F32)],
        compiler_params=_cparams(1),
        name="rglru_ctx" if has_ctx else "rglru",
    )(*args)


ROUTE_OFF = N_EGROUPS
SORT_TM = 256
RUN_PAD = 16
SLOTS = SORT_TM + N_EGROUPS * RUN_PAD
SLOTS_PAD = 384
RUN_BITS = (16, 32, 64, 128, 256)
FFN_BLK = 512
GS_COLS = D_MODEL + LANES


def _routing_gate(logits):
    lane = lax.broadcasted_iota(jnp.int32, logits.shape, 1)
    lane_f = lane.astype(F32)
    neg = -jnp.inf
    big = float(LANES)
    gl = jnp.where(lane < N_EGROUPS, logits, neg)
    gmax = jnp.max(gl, axis=-1, keepdims=True)
    g_w = 1.0 / jnp.sum(jnp.exp(gl - gmax), axis=-1, keepdims=True)
    g_sel = jnp.min(jnp.where(gl == gmax, lane_f, big), axis=-1, keepdims=True)
    e_lane = lane - ROUTE_OFF
    in_grp = (e_lane >= 0) & (e_lane < N_EXPERTS) & ((e_lane // N_EPG).astype(F32) == g_sel)
    el = jnp.where(in_grp, logits, neg)
    m1 = jnp.max(el, axis=-1, keepdims=True)
    i1 = jnp.min(jnp.where(el == m1, lane_f, big), axis=-1, keepdims=True)
    el2 = jnp.where(lane_f == i1, neg, el)
    m2 = jnp.max(el2, axis=-1, keepdims=True)
    i2 = jnp.min(jnp.where(el2 == m2, lane_f, big), axis=-1, keepdims=True)
    r = jnp.exp(m2 - m1)
    p1 = 1.0 / (1.0 + r)
    p2 = r / (1.0 + r)
    return jnp.where(lane_f == i1, g_w * p1, jnp.where(lane_f == i2, g_w * p2, 0.0)), g_sel


def _route_kernel(x_ref, m_ref, mix0, mix1, mix2, mix3, wout_ref, g2_ref, wr_ref, br_ref,
                  x1_ref, h2_ref, gate_ref, cnt_ref, slot_ref):
    tm = SORT_TM
    mix = jnp.concatenate([mix0[...], mix1[...], mix2[...], mix3[...]], axis=1).astype(BF16)
    x1 = x_ref[...] + m_ref[0, 2:3] * _dot(mix, wout_ref[...])
    x1_ref[...] = x1
    h2 = _rms(x1) * g2_ref[...] * (1.0 + m_ref[0, 4:5]) + m_ref[0, 3:4]
    h2_hi = h2.astype(BF16)
    h2_ref[...] = h2_hi
    h2_lo = (h2 - h2_hi.astype(F32)).astype(BF16)
    hi = _dot(h2_hi, wr_ref[...])
    logits = hi[:, :LANES] + hi[:, LANES:] + _dot(h2_lo, wr_ref[:, :LANES]) + br_ref[...]
    gate, g_sel = _routing_gate(logits)
    gate_ref[...] = gate

    lane = lax.broadcasted_iota(jnp.int32, (tm, LANES), 1).astype(F32)
    onehot = jnp.where(lane == g_sel, 1.0, 0.0)
    cnt_ref[0] = jnp.sum(onehot, axis=0, keepdims=True)
    onehot_t = onehot.T
    earlier = jnp.where(lax.broadcasted_iota(jnp.int32, (tm, tm), 0) < lax.broadcasted_iota(jnp.int32, (tm, tm), 1),
                        1.0, 0.0).astype(BF16)
    rank_t = _dot(onehot_t.astype(BF16), earlier)
    cnt = jnp.sum(onehot_t, axis=1, keepdims=True)
    padded = jnp.ceil(cnt * (1.0 / RUN_PAD)) * RUN_PAD
    grp = lax.broadcasted_iota(jnp.int32, (LANES, 1), 0)
    start = jnp.zeros((LANES, 1), F32)
    for g in range(N_EGROUPS - 1):
        start = start + jnp.where(grp > g, padded[g:g + 1, :], 0.0)
    slot = jnp.sum(onehot_t * (rank_t + start), axis=0, keepdims=True)
    slot_ref[0] = slot


def _route(x, mod, mod_row, mixes, wout, g2, wr, br):
    t = x.shape[0]
    tm = SORT_TM
    nt = t // tm
    tok = lambda n: pl.BlockSpec((tm, n), lambda i: (i, 0))
    const = lambda a: pl.BlockSpec(a.shape, lambda i: (0,) * a.ndim)
    return pl.pallas_call(
        _route_kernel,
        grid=(nt,),
        in_specs=[tok(D_MODEL), pl.BlockSpec((1, 6, D_MODEL), lambda i: (mod_row(i), 0, 0)),
                  tok(GROUP_W), tok(GROUP_W), tok(GROUP_W), tok(GROUP_W),
                  const(wout), const(g2), const(wr), const(br)],
        out_specs=[tok(D_MODEL), tok(D_MODEL), tok(LANES),
                   pl.BlockSpec((1, 1, LANES), lambda i: (i, 0, 0)), pl.BlockSpec((1, 1, tm), lambda i: (i, 0, 0))],
        out_shape=[jax.ShapeDtypeStruct((t, D_MODEL), F32), jax.ShapeDtypeStruct((t, D_MODEL), BF16),
                   jax.ShapeDtypeStruct((t, LANES), F32),
                   jax.ShapeDtypeStruct((nt, 1, LANES), F32), jax.ShapeDtypeStruct((nt, 1, tm), F32)],
        compiler_params=_cparams(1),
        name="route",
    )(x, mod, *mixes, wout, g2, wr, br)


def _for_each_piece(n_rows, fn):
    off = 0
    for bit in RUN_BITS:
        has = (n_rows & bit) != 0
        pl.when(has)(functools.partial(fn, off, bit))
        off = off + jnp.where(has, bit, 0)


def _regroup_kernel(loc0_ref, len_ref, dst0_ref, tail0_ref, tail_len_ref, slack_ref, h2_ref, gate_ref, slot_ref,
                    gs_hbm, srt_s, zero_s, sem, zsem):
    i = pl.program_id(0)
    n = pl.num_programs(0)
    tm = SORT_TM
    start, wait = (lambda cp: cp.start()), (lambda cp: cp.wait())

    def zero_pieces(do):
        def tail_copy(g, off, size):
            return pltpu.make_async_copy(zero_s.at[pl.ds(0, size)],
                                         gs_hbm.at[pl.ds(pl.multiple_of(tail0_ref[g] + off, RUN_PAD), size)], zsem)

        def slack_copy(j):
            size = RUN_BITS[-1]
            return pltpu.make_async_copy(zero_s, gs_hbm.at[pl.ds(pl.multiple_of(slack_ref[0] + j * size, size), size)], zsem)

        for g in range(N_EGROUPS):
            _for_each_piece(tail_len_ref[g], lambda off, size, g=g: do(tail_copy(g, off, size)))
        lax.fori_loop(0, slack_ref[1], lambda j, c: (do(slack_copy(j)), c)[1], 0)

    def run_pieces(tile, do):
        buf = tile % 2
        for g in range(N_EGROUPS):
            r = tile * N_EGROUPS + g

            def piece(off, size, r=r):
                do(pltpu.make_async_copy(
                    srt_s.at[buf, pl.ds(pl.multiple_of(loc0_ref[r] + off, RUN_PAD), size)],
                    gs_hbm.at[pl.ds(pl.multiple_of(dst0_ref[r] + off, RUN_PAD), size)], sem.at[buf]))

            _for_each_piece(len_ref[r], piece)

    @pl.when(i == 0)
    def _zeros():
        zero_s[...] = jnp.zeros_like(zero_s)
        zero_pieces(start)

    @pl.when(i >= 2)
    def _reuse():
        run_pieces(i - 2, wait)

    perm = lax.broadcasted_iota(jnp.int32, (SLOTS, tm), 0).astype(F32) == slot_ref[0]
    perm = jnp.where(perm, 1.0, 0.0).astype(BF16)
    srt_h = _dot(perm, h2_ref[...])
    gate = gate_ref[...]
    g_hi = gate.astype(BF16)
    g_mid = (gate - g_hi.astype(F32)).astype(BF16)
    g_lo = (gate - g_hi.astype(F32) - g_mid.astype(F32)).astype(BF16)
    parts = _dot(perm, jnp.concatenate([g_hi, g_mid], axis=1))
    srt_g = parts[:, :LANES] + parts[:, LANES:] + _dot(perm, g_lo)
    srt_s[i % 2] = jnp.concatenate([srt_h, srt_g], axis=1)
    run_pieces(i, start)

    @pl.when(i == n - 1)
    def _drain():
        pl.when(i >= 1)(lambda: run_pieces(i - 1, wait))
        run_pieces(i, wait)
        zero_pieces(wait)


def _regroup(h2, gate, slot, loc0, length, dst0, tail0, tail_len, slack, n_rows):
    t = h2.shape[0]
    tm = SORT_TM
    return pl.pallas_call(
        _regroup_kernel,
        grid_spec=pltpu.PrefetchScalarGridSpec(
            num_scalar_prefetch=6,
            grid=(t // tm,),
            in_specs=[pl.BlockSpec((tm, D_MODEL), lambda i, *_: (i, 0)),
                      pl.BlockSpec((tm, LANES), lambda i, *_: (i, 0)),
                      pl.BlockSpec((1, 1, tm), lambda i, *_: (i, 0, 0))],
            out_specs=pl.BlockSpec(memory_space=pl.ANY),
            scratch_shapes=[pltpu.VMEM((2, SLOTS, GS_COLS), F32), pltpu.VMEM((RUN_BITS[-1], GS_COLS), F32),
                            pltpu.SemaphoreType.DMA((2,)), pltpu.SemaphoreType.DMA(())],
        ),
        out_shape=jax.ShapeDtypeStruct((n_rows, GS_COLS), F32),
        compiler_params=_cparams(1),
        name="regroup",
    )(loc0, length, dst0, tail0, tail_len, slack, h2, gate, slot)


def _ffn_kernel(blk_grp_ref, n_valid_ref, gs_ref, wg_ref, wu_ref, wd_ref, ys_ref):
    b = pl.program_id(0)

    @pl.when(b >= n_valid_ref[0])
    def _unused():
        ys_ref[...] = jnp.zeros_like(ys_ref)

    @pl.when(b < n_valid_ref[0])
    def _block():
        grp = blk_grp_ref[b]
        xs = gs_ref[:, :D_MODEL].astype(BF16)
        gates = gs_ref[:, D_MODEL:]
        lane = lax.broadcasted_iota(jnp.int32, gates.shape, 1)
        hid = []
        for j in range(N_EPG):
            gcol = jnp.sum(jnp.where(lane == grp * N_EPG + j + ROUTE_OFF, gates, 0.0), axis=-1, keepdims=True)
            hj = _silu(_dot(xs, wg_ref[j])) * _dot(xs, wu_ref[j])
            hid.append((hj * gcol).astype(BF16))
        ys_ref[...] = _dot(jnp.concatenate(hid, axis=1), wd_ref[...]).astype(BF16)


def _group_ffn(gs, blk_grp, n_valid, wg, wu, wd):
    nblk = gs.shape[0] // FFN_BLK
    live = lambda b, bg, nv: jnp.minimum(b, nv[0] - 1)
    return pl.pallas_call(
        _ffn_kernel,
        grid_spec=pltpu.PrefetchScalarGridSpec(
            num_scalar_prefetch=2,
            grid=(nblk,),
            in_specs=[
                pl.BlockSpec((FFN_BLK, GS_COLS), lambda b, bg, nv: (live(b, bg, nv), 0)),
                pl.BlockSpec((N_EPG, D_MODEL, EXPERT_FF), lambda b, bg, nv: (bg[b], 0, 0)),
                pl.BlockSpec((N_EPG, D_MODEL, EXPERT_FF), lambda b, bg, nv: (bg[b], 0, 0)),
                pl.BlockSpec((N_EPG * EXPERT_FF, D_MODEL), lambda b, bg, nv: (bg[b], 0)),
            ],
            out_specs=pl.BlockSpec((FFN_BLK, D_MODEL), lambda b, bg, nv: (b, 0)),
        ),
        out_shape=jax.ShapeDtypeStruct((nblk * FFN_BLK, D_MODEL), BF16),
        compiler_params=_cparams(1),
        name="group_ffn",
    )(blk_grp, n_valid, gs, wg, wu, wd)


def _combine_kernel(final, loc0_ref, len_ref, dst0_ref, x1_ref, m_ref, slot_ref, gfin_ref, ys_hbm, o_ref, run_s, sem):
    i = pl.program_id(0)
    tm = SORT_TM

    def fetch(tile, do):
        buf = tile % 2
        for g in range(N_EGROUPS):
            r = tile * N_EGROUPS + g

            def piece(off, size, r=r):
                do(pltpu.make_async_copy(
                    ys_hbm.at[pl.ds(pl.multiple_of(dst0_ref[r] + off, RUN_PAD), size)],
                    run_s.at[buf, pl.ds(pl.multiple_of(loc0_ref[r] + off, RUN_PAD), size)], sem.at[buf]))

            _for_each_piece(len_ref[r], piece)

    @pl.when(i == 0)
    def _first():
        fetch(i, lambda cp: cp.start())

    @pl.when(i + 1 < pl.num_programs(0))
    def _prefetch():
        fetch(i + 1, lambda cp: cp.start())

    fetch(i, lambda cp: cp.wait())

    used = loc0_ref[i * N_EGROUPS + N_EGROUPS - 1] + len_ref[i * N_EGROUPS + N_EGROUPS - 1]
    rows = lax.broadcasted_iota(jnp.int32, (SLOTS_PAD, D_MODEL), 0)
    y_run = jnp.where(rows < used, run_s[i % 2], jnp.zeros((), BF16))
    slot_b = jnp.broadcast_to(slot_ref[0], (LANES, tm)).T
    lane = lax.broadcasted_iota(jnp.int32, (tm, LANES), 1).astype(F32)
    inv = jnp.concatenate([jnp.where(slot_b == lane + float(c), 1.0, 0.0) for c in range(0, SLOTS_PAD, LANES)],
                          axis=1).astype(BF16)
    x2 = x1_ref[...] + m_ref[0, 5:6] * _dot(inv, y_run)
    o_ref[...] = _rms(x2) * gfin_ref[...] if final else x2


def _combine(final, x1, mod, mod_row, slot, gfin, ys, loc0, length, dst0):
    t = x1.shape[0]
    tm = SORT_TM
    return pl.pallas_call(
        functools.partial(_combine_kernel, final),
        grid_spec=pltpu.PrefetchScalarGridSpec(
            num_scalar_prefetch=3,
            grid=(t // tm,),
            in_specs=[
                pl.BlockSpec((tm, D_MODEL), lambda i, *_: (i, 0)),
                pl.BlockSpec((1, 6, D_MODEL), lambda i, *_: (mod_row(i), 0, 0)),
                pl.BlockSpec((1, 1, tm), lambda i, *_: (i, 0, 0)),
                pl.BlockSpec((1, D_MODEL), lambda i, *_: (0, 0)),
                pl.BlockSpec(memory_space=pl.ANY),
            ],
            out_specs=pl.BlockSpec((tm, D_MODEL), lambda i, *_: (i, 0)),
            scratch_shapes=[pltpu.VMEM((2, SLOTS_PAD, D_MODEL), BF16), pltpu.SemaphoreType.DMA((2,))],
        ),
        out_shape=jax.ShapeDtypeStruct((t, D_MODEL), F32),
        compiler_params=_cparams(1),
        name="combine",
    )(loc0, length, dst0, x1, mod, slot, gfin, ys)


def _outproj_moe(final, x, mod, mod_row, mixes, wout, g2, wr, br, wg, wu, wd, gfin):
    t = x.shape[0]
    nt = t // SORT_TM
    x1, h2, gate, cnt, slot = _route(x, mod, mod_row, mixes, wout, g2, wr, br)
    cnt = cnt[:, 0, :N_EGROUPS].astype(jnp.int32)
    length = (cnt + RUN_PAD - 1) // RUN_PAD * RUN_PAD
    loc0 = jnp.cumsum(length, axis=1) - length
    g_rows = jnp.sum(length, axis=0)
    g_blocks = (g_rows + FFN_BLK - 1) // FFN_BLK
    g_base = (jnp.cumsum(g_blocks) - g_blocks) * FFN_BLK
    dst0 = g_base[None, :] + jnp.cumsum(length, axis=0) - length
    n_blocks = (t + nt * N_EGROUPS * (RUN_PAD - 1) + FFN_BLK - 1) // FFN_BLK + N_EGROUPS
    blk_grp = jnp.minimum(jnp.sum(jnp.arange(n_blocks)[:, None] >= jnp.cumsum(g_blocks)[None, :], axis=1),
                          N_EGROUPS - 1).astype(jnp.int32)
    n_valid = jnp.sum(g_blocks).astype(jnp.int32)[None]
    flat = lambda a: a.reshape(-1).astype(jnp.int32)
    slack = jnp.stack([n_valid[0] * FFN_BLK, (n_blocks - n_valid[0]) * (FFN_BLK // RUN_BITS[-1])]).astype(jnp.int32)
    gs = _regroup(h2, gate, slot, flat(loc0), flat(length), flat(dst0), flat(g_base + g_rows),
                  flat(g_blocks * FFN_BLK - g_rows), slack, n_blocks * FFN_BLK)
    ys = _group_ffn(gs, blk_grp, n_valid, wg, wu, wd)
    return _combine(final, x1, mod, mod_row, slot, gfin, ys, flat(loc0), flat(length), flat(dst0))


def _block_diag(w):
    n, k, _ = w.shape
    eye = jnp.eye(n, dtype=w.dtype)
    return (eye[:, None, :, None] * w[:, :, None, :]).reshape(n * k, n * k)


def _pad_lanes(v, n=LANES):
    return jnp.pad(v, ((0, 0), (0, n - v.shape[-1])))


def kernel(x_prompt, x_sample, c, cache_k, cache_v, state_ssd, state_rglru, c_ctx, w_mod, b_mod, norm1_g, norm2_g, w_in, w_out, ssd_conv_w, ssd_conv_b, ssd_dt_bias, ssd_a_log, ssd_d, ssd_norm_g, pool_w, pool_scale, da_lam_q1, da_lam_k1, da_lam_q2, da_lam_k2, da_norm_g, rg_conv_w, rg_conv_b, rg_wa, rg_ba, rg_wx, rg_bx, rg_lambda, moe_w_group, moe_b_group, moe_w_expert, moe_b_expert, moe_w_gate, moe_w_up, moe_w_down, final_norm_g):
    nbp, lp, _ = x_prompt.shape
    nbs, ls, _ = x_sample.shape
    past = cache_k.shape[2]
    tm_in = 512
    assert nbs + 1 <= MOD_ROWS and lp % CH == 0 and ls % CH == 0
    assert tm_in % lp == 0 and (nbp * lp) % tm_in == 0 and ls % tm_in == 0 and lp % SORT_TM == 0 and ls % SORT_TM == 0
    assert lp % KEY_BLK == 0 and ls % KEY_BLK == 0 and past % KEY_BLK == 0

    cond = jnp.concatenate([c_ctx[None, :], c, jnp.zeros((MOD_ROWS - 1 - nbs, D_MODEL), F32)], axis=0)
    mod = _modulation(cond, w_mod, b_mod).reshape(DEPTH * MOD_ROWS, 6, D_MODEL)

    w_in_r = jnp.concatenate([w_in[:, :, :DT_LO], w_in[:, :, DT_HI:], w_in[:, :, DT_LO:DT_HI],
                              jnp.zeros((DEPTH, D_MODEL, LANES - (DT_HI - DT_LO)), F32)], axis=-1).astype(BF16)
    w_out_b = w_out.astype(BF16)
    w_gate_b = moe_w_gate.astype(BF16)
    w_up_b = moe_w_up.astype(BF16)
    w_down_b = moe_w_down.astype(BF16).reshape(DEPTH, N_EXPERTS * EXPERT_FF, D_MODEL)
    w_route = jnp.concatenate([moe_w_group, moe_w_expert,
                               jnp.zeros((DEPTH, D_MODEL, LANES - N_EGROUPS - N_EXPERTS), F32)], axis=-1)
    w_route_hi = w_route.astype(BF16)
    w_route = jnp.concatenate([w_route_hi, (w_route - w_route_hi.astype(F32)).astype(BF16)], axis=-1)
    b_route = _pad_lanes(jnp.concatenate([moe_b_group, moe_b_expert], axis=-1))
    dtb = _pad_lanes(ssd_dt_bias.reshape(DEPTH, 2 * SSD_HEADS))
    alog = _pad_lanes(ssd_a_log.reshape(DEPTH, 2 * SSD_HEADS))
    d_skip = jnp.repeat(ssd_d, SSD_HEADDIM, axis=-1)
    cos, sin = _rope_tables(ls)
    ck = cache_k.reshape(nbs, DEPTH, past, GROUP_W)
    cv = cache_v.reshape(nbs, DEPTH, past, GROUP_W)
    g_fin = final_norm_g[None, :]

    xp = x_prompt.reshape(nbp * lp, D_MODEL)
    xs = x_sample.reshape(nbs * ls, D_MODEL)
    new_k = jnp.zeros((nbp, DEPTH, lp, GROUP_W), F32)
    new_v = jnp.zeros((nbp, DEPTH, lp, GROUP_W), F32)
    ssd_out, rg_out = [], []
    for l in range(DEPTH):
        row1 = lambda a: a[l][None, :]
        final = l == DEPTH - 1
        lam_init = 0.8 - 0.6 * math.exp(-0.3 * l)
        ssd_w = (ssd_conv_w[l], row1(ssd_conv_b), row1(dtb), row1(alog), row1(d_skip), row1(ssd_norm_g))
        pool_wb = _block_diag(pool_w[l])
        att_w = (row1(da_lam_q1), row1(da_lam_k1), row1(da_lam_q2), row1(da_lam_k2), row1(da_norm_g))
        rg_w = (rg_conv_w[l], row1(rg_conv_b),
                jnp.stack([_block_diag(rg_wa[l, 0]), _block_diag(rg_wa[l, 1])]), rg_ba[l][:, None, :],
                jnp.stack([_block_diag(rg_wx[l, 0]), _block_diag(rg_wx[l, 1])]), rg_bx[l][:, None, :],
                rg_lambda[l][:, None, :])
        moe_w = (w_out_b[l], row1(norm2_g), w_route[l], row1(b_route), w_gate_b[l], w_up_b[l], w_down_b[l], g_fin)
        ctx_row = lambda i, l=l: l * MOD_ROWS
        lat_row = lambda tm: (lambda i, l=l: l * MOD_ROWS + 1 + i // (ls // tm))

        xbc, z, xpool, q, new_k, new_v, xr, gr, dt = _inproj(
            xp, mod, ctx_row, row1(norm1_g), w_in_r[l], tm_in, l, caches=(new_k, new_v))
        ya, st_ssd = _ssd(False, nbp, lp, l, xbc, z, dt, *ssd_w)
        yb = _pool(nbp, lp, xpool, pool_wb, row1(pool_scale))
        yc = _attn(False, nbp, lp, l, lam_init, q, new_k, new_v, *att_w)
        yd, st_rg = _rglru(False, nbp, lp, l, xr, gr, *rg_w)
        xp = _outproj_moe(final, xp, mod, ctx_row, (ya, yb, yc, yd), *moe_w)
        ssd_out.append(st_ssd)
        rg_out.append(st_rg)

        xbc, z, xpool, q, k, v, xr, gr, dt = _inproj(xs, mod, lat_row(tm_in), row1(norm1_g), w_in_r[l], tm_in, l)
        ya = _ssd(True, nbs, ls, l, xbc, z, dt, *ssd_w, h0=state_ssd)
        yb = _pool(nbs, ls, xpool, pool_wb, row1(pool_scale))
        yc = _attn(True, nbs, ls, l, lam_init, q, k, v, *att_w, ck=ck, cv=cv, cos=cos, sin=sin)
        yd = _rglru(True, nbs, ls, l, xr, gr, *rg_w, h0=state_rglru)
        xs = _outproj_moe(final, xs, mod, lat_row(SORT_TM), (ya, yb, yc, yd), *moe_w)

    return (xp.reshape(nbp, lp, D_MODEL), xs.reshape(nbs, ls, D_MODEL),
            new_k.reshape(nbp, DEPTH, lp, DA_HEADS, 2 * DA_QKDIM), new_v.reshape(nbp, DEPTH, lp, DA_HEADS, DA_VDIM),
            jnp.stack(ssd_out, axis=1), jnp.stack(rg_out, axis=1))
```

```python
import functools
import math

import numpy as np
import jax
import jax.numpy as jnp
from jax import lax
from jax.experimental import pallas as pl
from jax.experimental.pallas import tpu as pltpu

F32 = jnp.float32
BF16 = jnp.bfloat16
HIGHEST = lax.Precision.HIGHEST

D_MODEL = 1024
DEPTH = 4
GRID_W = 64
GROUP_W = 256
EPS = 1e-6
SSD_HEADDIM = 64
SSD_HEADS = 4
SSD_STATE = 64
SSD_BC = 128
SSD_CONV_CH = 512
POOL_WINDOWS = (2, 4, 8, 16)
DA_HEADS = 4
DA_VDIM = 64
DA_QKDIM = 32
ROPE_BASE = 10000.0
RG_C = 8.0
N_EGROUPS = 4
N_EPG = 4
N_EXPERTS = 16
EXPERT_FF = 256
DT_LO, DT_HI = 768, 776

LANES = 128
SUBLANES = 8
CH = 128
HALO = SUBLANES
MOD_ROWS = 16
VMEM_LIMIT = 56 * 1024 * 1024


def _cparams(n_axes):
    return pltpu.CompilerParams(dimension_semantics=("arbitrary",) * n_axes, vmem_limit_bytes=VMEM_LIMIT)


def _silu(x):
    return x * jax.nn.sigmoid(x)


def _softplus(x):
    return jnp.maximum(x, 0.0) + jnp.log1p(jnp.exp(-jnp.abs(x)))


def _dot(a, b, **kw):
    return jnp.dot(a, b, preferred_element_type=F32, **kw)


def _dot_nt(a, b):
    return lax.dot_general(a, b, (((1,), (1,)), ((), ())), preferred_element_type=F32)


def _rms(x):
    return x * lax.rsqrt(jnp.mean(x * x, axis=-1, keepdims=True) + EPS)


def _window(ref, c, n_steps, seq_len):
    r0 = pl.multiple_of(c * CH, CH)
    main = ref[pl.ds(r0, CH), :]
    lo = pl.multiple_of(jnp.maximum(r0 - HALO, 0), HALO)
    hi = pl.multiple_of(jnp.minimum(r0 + CH, seq_len - HALO), HALO)
    prev = jnp.where(c > 0, ref[pl.ds(lo, HALO), :], 0.0)
    nxt = jnp.where(c < n_steps - 1, ref[pl.ds(hi, HALO), :], 0.0)
    return jnp.concatenate([prev, main, nxt], axis=0)


def _conv4(win, w_ref, b_ref):
    acc = b_ref[...]
    for k in range(4):
        acc = acc + w_ref[k:k + 1, :] * win[HALO - 1 + k:HALO - 1 + k + CH, :]
    return acc


def _mod_kernel(cond_ref, w_ref, b_ref, o_ref):
    cnd = cond_ref[...]
    o_ref[0] = _dot(_silu(cnd), w_ref[0], precision=HIGHEST) + b_ref[0]


def _modulation(cond, w_mod, b_mod):
    nb = 6
    return pl.pallas_call(
        _mod_kernel,
        grid=(DEPTH, nb),
        in_specs=[
            pl.BlockSpec((MOD_ROWS, D_MODEL), lambda l, j: (0, 0)),
            pl.BlockSpec((1, D_MODEL, D_MODEL), lambda l, j: (l, 0, j)),
            pl.BlockSpec((1, 1, D_MODEL), lambda l, j: (l, 0, j)),
        ],
        out_specs=pl.BlockSpec((1, MOD_ROWS, D_MODEL), lambda l, j: (l, 0, j)),
        out_shape=jax.ShapeDtypeStruct((DEPTH, MOD_ROWS, nb * D_MODEL), F32),
        compiler_params=_cparams(2),
        name="modulation",
    )(cond, w_mod, b_mod.reshape(DEPTH, 1, nb * D_MODEL))


IN_COLS = (512, 256, 256, 256, 256, 256, 256, 256, LANES)
K_OUT, V_OUT = 4, 5


def _inproj_kernel(to_cache, x_ref, m_ref, g_ref, w_ref, *refs):
    if to_cache:
        refs = refs[2:]
    hh = _rms(x_ref[...]) * g_ref[...] * (1.0 + m_ref[0, 1:2]) + m_ref[0, 0:1]
    u = _dot(hh.astype(BF16), w_ref[...])
    off = 0
    for j, (ref, n) in enumerate(zip(refs, IN_COLS)):
        if to_cache and j in (K_OUT, V_OUT):
            ref[:, 0] = u[:, off:off + n].reshape(ref.shape[0], ref.shape[2], n)
        else:
            ref[...] = u[:, off:off + n]
        off += n


def _inproj(x, mod, mod_row, g, w, tm, layer, caches=None):
    t = x.shape[0]
    ncol = sum(IN_COLS)
    to_cache = caches is not None
    in_specs = [
        pl.BlockSpec((tm, D_MODEL), lambda i: (i, 0)),
        pl.BlockSpec((1, 6, D_MODEL), lambda i: (mod_row(i), 0, 0)),
        pl.BlockSpec((1, D_MODEL), lambda i: (0, 0)),
        pl.BlockSpec((D_MODEL, ncol), lambda i: (0, 0)),
    ]
    out_specs = [pl.BlockSpec((tm, n), lambda i: (i, 0)) for n in IN_COLS]
    out_shape = [jax.ShapeDtypeStruct((t, n), F32) for n in IN_COLS]
    args = [x, mod, g, w]
    aliases = {}
    if to_cache:
        seq = caches[0].shape[2]
        assert tm % seq == 0
        cache_spec = pl.BlockSpec((tm // seq, 1, seq, GROUP_W), lambda i: (i, layer, 0, 0))
        for j, cch in zip((K_OUT, V_OUT), caches):
            in_specs.append(pl.BlockSpec(memory_space=pl.ANY))
            aliases[len(args)] = j
            args.append(cch)
            out_specs[j] = cache_spec
            out_shape[j] = jax.ShapeDtypeStruct(cch.shape, F32)
    return pl.pallas_call(
        functools.partial(_inproj_kernel, to_cache),
        grid=(t // tm,),
        in_specs=in_specs,
        out_specs=out_specs,
        out_shape=out_shape,
        input_output_aliases=aliases,
        compiler_params=_cparams(1),
        name="inproj_ctx" if to_cache else "inproj",
    )(*args)


def _ssd_kernel(has_ctx, seq_len, xbc_ref, z_ref, dt_ref, cw_ref, cb_ref, dtb_ref, alog_ref, d_ref, ng_ref, *rest):
    if has_ctx:
        h0_ref, y_ref, xc_s, y_s, st_s, upd_s, grow_s, keep_s = rest
        st_ref = None
    else:
        y_ref, st_ref, xc_s, y_s, st_s, upd_s, grow_s, keep_s = rest
    nc = seq_len // CH
    hd = SSD_HEADDIM
    a_row = -jnp.exp(alog_ref[...])
    row = lax.broadcasted_iota(jnp.int32, (CH, CH), 0)
    col = lax.broadcasted_iota(jnp.int32, (CH, CH), 1)
    lane_w = lax.broadcasted_iota(jnp.int32, (CH, GROUP_W), 1)
    lane_n = lax.broadcasted_iota(jnp.int32, (CH, SSD_BC), 1)
    own_block = (lax.broadcasted_iota(jnp.int32, (SSD_BC, GROUP_W), 0) // SSD_STATE
                 == lax.broadcasted_iota(jnp.int32, (SSD_BC, GROUP_W), 1) // (2 * hd))

    def conv_step(c, carry):
        r0 = pl.multiple_of(c * CH, CH)
        xc = _silu(_conv4(_window(xbc_ref, c, nc, seq_len), cw_ref, cb_ref))
        xc_s[pl.ds(r0, CH), :] = xc
        y_s[pl.ds(r0, CH), :] = xc[:, :GROUP_W] * d_ref[...]
        return carry

    lax.fori_loop(0, nc, conv_step, 0)

    st_s[...] = jnp.zeros_like(st_s)
    if has_ctx:
        for d in range(2):
            for h in range(SSD_HEADS):
                g = h // 2
                st_s[d, g * SSD_STATE:(g + 1) * SSD_STATE, h * hd:(h + 1) * hd] = h0_ref[0, 0, d, h].T

    def per_head(v, d):
        lanes = lane_w[:v.shape[0]]
        out = jnp.broadcast_to(v[:, 4 * d + 3:4 * d + 4], (v.shape[0], GROUP_W))
        for h in (2, 1, 0):
            out = jnp.where(lanes < (h + 1) * hd, jnp.broadcast_to(v[:, 4 * d + h:4 * d + h + 1], out.shape), out)
        return out

    def local_step(c, carry):
        r0 = pl.multiple_of(c * CH, CH)
        xc = xc_s[pl.ds(r0, CH), :]
        x = xc[:, :GROUP_W]
        bm = xc[:, GROUP_W:GROUP_W + SSD_BC]
        cm = xc[:, GROUP_W + SSD_BC:]
        dt = _softplus(dt_ref[pl.ds(r0, CH), :] + dtb_ref[...])
        bmb = bm.astype(BF16)
        bm_t = bm.T.astype(BF16)
        scores = [_dot_nt(jnp.where((lane_n // SSD_STATE) == g, cm, 0.0).astype(BF16), bmb) for g in range(2)]
        y_diag = None
        da = dt * a_row
        pre = da
        for k in (1, 2, 4, 8, 16, 32, 64):
            pre = pre + jnp.where(row >= k, pltpu.roll(pre, k, axis=0), 0.0)
        for d in range(2):
            tri = (row >= col) if d == 0 else (row <= col)
            cs = pre if d == 0 else pre[CH - 1:CH, :] - pre + da
            cs_t = cs.T
            tot = cs[CH - 1:CH, :] if d == 0 else cs[0:1, :]
            dt_e, cs_e, tot_e = per_head(dt, d), per_head(cs, d), per_head(tot, d)
            xdt = x * dt_e
            m_parts, r_parts = [], []
            for h in range(SSD_HEADS):
                k = SSD_HEADS * d + h
                decay = jnp.exp(jnp.where(tri, cs[:, k:k + 1] - cs_t[k:k + 1, :], -jnp.inf))
                m_parts.append((scores[h // 2] * decay).astype(BF16))
                r_parts.append(jnp.where((lane_w // hd) == h, xdt, 0.0).astype(BF16))
            yd = _dot(jnp.concatenate(m_parts, axis=1), jnp.concatenate(r_parts, axis=0))
            y_diag = yd if y_diag is None else y_diag + yd
            wgt = xdt * jnp.exp(tot_e - cs_e)
            upd_s[d, c] = jnp.where(own_block, _dot(bm_t, wgt.astype(BF16)), 0.0)
            grow_s[d, pl.ds(r0, CH), :] = jnp.exp(cs_e)
            keep_s[d, pl.ds(c, 1), :] = jnp.exp(tot_e)
        y_s[pl.ds(r0, CH), :] += y_diag
        return carry

    lax.fori_loop(0, nc, local_step, 0, unroll=2)

    def state_dir(d, ci):
        r0 = pl.multiple_of(ci * CH, CH)
        st = st_s[d]
        cmb = xc_s[pl.ds(r0, CH), GROUP_W + SSD_BC:].astype(BF16)
        y_s[pl.ds(r0, CH), :] += _dot(cmb, st.astype(BF16)) * grow_s[d, pl.ds(r0, CH), :]
        st_s[d] = st * keep_s[d, pl.ds(ci, 1), :] + upd_s[d, ci]

    def scan_step(c, carry):
        state_dir(0, c)
        state_dir(1, nc - 1 - c)
        return carry

    lax.fori_loop(0, nc, scan_step, 0, unroll=2)

    def out_step(c, carry):
        r0 = pl.multiple_of(c * CH, CH)
        y = y_s[pl.ds(r0, CH), :] * _silu(z_ref[pl.ds(r0, CH), :])
        y_ref[pl.ds(r0, CH), :] = _rms(y) * ng_ref[...]
        return carry

    lax.fori_loop(0, nc, out_step, 0)
    if not has_ctx:
        for d in range(2):
            for h in range(SSD_HEADS):
                g = h // 2
                st_ref[0, d, h] = st_s[d, g * SSD_STATE:(g + 1) * SSD_STATE, h * hd:(h + 1) * hd].T


def _ssd(has_ctx, nb, seq_len, layer, xbc, z, dt, cw, cb, dtb, alog, dsk, ng, h0=None):
    rows = lambda n: pl.BlockSpec((seq_len, n), lambda b: (b, 0))
    full = lambda a: pl.BlockSpec(a.shape, lambda b: (0,) * a.ndim)
    in_specs = [rows(SSD_CONV_CH), rows(GROUP_W), rows(LANES)] + [full(a) for a in (cw, cb, dtb, alog, dsk, ng)]
    args = [xbc, z, dt, cw, cb, dtb, alog, dsk, ng]
    y_spec = rows(GROUP_W)
    y_shape = jax.ShapeDtypeStruct((nb * seq_len, GROUP_W), F32)
    st_blk = (1, 2, SSD_HEADS, SSD_HEADDIM, SSD_STATE)
    if has_ctx:
        in_specs.append(pl.BlockSpec((1, 1) + st_blk[1:], lambda b: (b, layer, 0, 0, 0, 0)))
        args.append(h0)
        out_specs, out_shape = y_spec, y_shape
    else:
        out_specs = [y_spec, pl.BlockSpec(st_blk, lambda b: (b, 0, 0, 0, 0))]
        out_shape = [y_shape, jax.ShapeDtypeStruct((nb,) + st_blk[1:], F32)]
    return pl.pallas_call(
        functools.partial(_ssd_kernel, has_ctx, seq_len),
        grid=(nb,),
        in_specs=in_specs,
        out_specs=out_specs,
        out_shape=out_shape,
        scratch_shapes=[
            pltpu.VMEM((seq_len, SSD_CONV_CH), F32),
            pltpu.VMEM((seq_len, GROUP_W), F32),
            pltpu.VMEM((2, SSD_BC, GROUP_W), F32),
            pltpu.VMEM((2, seq_len // CH, SSD_BC, GROUP_W), F32),
            pltpu.VMEM((2, seq_len, GROUP_W), F32),
            pltpu.VMEM((2, max(seq_len // CH, SUBLANES), GROUP_W), F32),
        ],
        compiler_params=_cparams(1),
        name="ssd_ctx" if has_ctx else "ssd",
    )(*args)


def _pool_kernel(seq_len, x_ref, w_ref, sc_ref, y_ref):
    nc = seq_len // CH
    wn = CH + 2 * HALO
    lane = lax.broadcasted_iota(jnp.int32, (CH, GROUP_W), 1)
    gw = GROUP_W // len(POOL_WINDOWS)
    half = jnp.where(lane < gw, 1, jnp.where(lane < 2 * gw, 2, jnp.where(lane < 3 * gw, 4, 8)))
    w_blk = w_ref[...].astype(BF16)

    def ahead(v, k):
        return pltpu.roll(v, wn - k, axis=0)

    def step(c, carry):
        r0 = pl.multiple_of(c * CH, CH)
        win = _window(x_ref, c, nc, seq_len)
        p2 = win + ahead(win, 1)
        p4 = p2 + ahead(p2, 2)
        p8 = p4 + ahead(p4, 4)
        p16 = p8 + ahead(p8, 8)
        s2 = ahead(p2, HALO - 1)[:CH]
        s4 = ahead(p4, HALO - 2)[:CH]
        s8 = ahead(p8, HALO - 4)[:CH]
        s16 = p16[:CH]
        tot = jnp.where(lane < gw, s2, jnp.where(lane < 2 * gw, s4, jnp.where(lane < 3 * gw, s8, s16)))
        t = r0 + lax.broadcasted_iota(jnp.int32, (CH, GROUP_W), 0)
        cnt = jnp.minimum(t + half, seq_len) - jnp.maximum(t - half, 0)
        x = win[HALO:HALO + CH]
        diff = tot / cnt.astype(F32) - x
        y_ref[pl.ds(r0, CH), :] = _dot(diff.astype(BF16), w_blk) * sc_ref[...]
        return carry

    lax.fori_loop(0, nc, step, 0)


def _pool(nb, seq_len, x, w_blk, scale):
    return pl.pallas_call(
        functools.partial(_pool_kernel, seq_len),
        grid=(nb,),
        in_specs=[
            pl.BlockSpec((seq_len, GROUP_W), lambda b: (b, 0)),
            pl.BlockSpec((GROUP_W, GROUP_W), lambda b: (0, 0)),
            pl.BlockSpec((1, GROUP_W), lambda b: (0, 0)),
        ],
        out_specs=pl.BlockSpec((seq_len, GROUP_W), lambda b: (b, 0)),
        out_shape=jax.ShapeDtypeStruct((nb * seq_len, GROUP_W), F32),
        compiler_params=_cparams(1),
        name="pool",
    )(x, w_blk, scale)


KEY_BLK = 256


def _rope(x, cos, sin):
    c2 = jnp.concatenate([cos, cos], axis=1)
    s2 = jnp.concatenate([sin, sin], axis=1)
    lane = lax.broadcasted_iota(jnp.int32, x.shape, 1)
    n = x.shape[1]
    partner = jnp.where(lane % 2 == 0, pltpu.roll(x, n - 1, axis=1), pltpu.roll(x, 1, axis=1))
    return x * c2 + partner * s2


def _attn_kernel(has_ctx, seq_len, tq, past, lam_init, q_ref, k_ref, v_ref, lq1, lk1, lq2, lk2, ng_ref, *rest):
    if has_ctx:
        ck_ref, cv_ref, cosq_ref, sinq_ref, cosk_ref, sink_ref, o_ref, kt_s, v_s = rest
    else:
        o_ref, kt_s, v_s = rest
    kb = KEY_BLK

    @pl.when(pl.program_id(1) == 0)
    def _prepare_keys():
        def put(dst0, kk, vv):
            kt_s[:, dst0:dst0 + kb] = kk.T.astype(BF16)
            for h in range(DA_HEADS):
                v_s[h, dst0:dst0 + kb, :] = vv[:, h * DA_VDIM:(h + 1) * DA_VDIM].astype(BF16)

        if has_ctx:
            for j in range(past // kb):
                put(j * kb, ck_ref[0, 0, j * kb:(j + 1) * kb, :], cv_ref[0, 0, j * kb:(j + 1) * kb, :])
        for j in range(seq_len // kb):
            kk = k_ref[0, 0, j * kb:(j + 1) * kb, :] if not has_ctx else k_ref[j * kb:(j + 1) * kb, :]
            vv = v_ref[0, 0, j * kb:(j + 1) * kb, :] if not has_ctx else v_ref[j * kb:(j + 1) * kb, :]
            if has_ctx:
                kk = _rope(kk, cosk_ref[j * kb:(j + 1) * kb, :], sink_ref[j * kb:(j + 1) * kb, :])
            put(past + j * kb, kk, vv)

    q = q_ref[...]
    if has_ctx:
        q = _rope(q, cosq_ref[...], sinq_ref[...])
    q = q * (DA_QKDIM ** -0.5 * math.log2(math.e))
    lam = (jnp.exp(jnp.sum(lq1[...] * lk1[...], axis=-1, keepdims=True))
           - jnp.exp(jnp.sum(lq2[...] * lk2[...], axis=-1, keepdims=True)) + lam_init)
    for h in range(DA_HEADS):
        es, sums = [], []
        for m in range(2):
            lo = h * 2 * DA_QKDIM + m * DA_QKDIM
            s = _dot(q[:, lo:lo + DA_QKDIM].astype(BF16), kt_s[lo:lo + DA_QKDIM, :])
            e = jnp.exp2(s - jnp.max(s, axis=-1, keepdims=True))
            sums.append(jnp.sum(e, axis=-1, keepdims=True))
            es.append(e.astype(BF16))
        o = _dot(jnp.concatenate(es, axis=0), v_s[h])
        acc = o[:tq] / sums[0] - lam * (o[tq:] / sums[1])
        o_ref[:, h * DA_VDIM:(h + 1) * DA_VDIM] = _rms(acc) * ng_ref[...] * (1.0 - lam_init)


def _attn(has_ctx, nb, seq_len, layer, lam_init, q, k, v, lq1, lk1, lq2, lk2, ng, ck=None, cv=None, cos=None, sin=None):
    tq = 256
    nq = seq_len // tq
    past = ck.shape[2] if has_ctx else 0
    keys = seq_len + past
    small = lambda a: pl.BlockSpec(a.shape, lambda b, i: (0,) * a.ndim)
    if has_ctx:
        kv_spec = pl.BlockSpec((seq_len, GROUP_W), lambda b, i: (b, 0))
    else:
        kv_spec = pl.BlockSpec((1, 1, seq_len, GROUP_W), lambda b, i: (b, layer, 0, 0))
    in_specs = [pl.BlockSpec((tq, GROUP_W), lambda b, i: (b * nq + i, 0)), kv_spec, kv_spec]
    in_specs += [small(a) for a in (lq1, lk1, lq2, lk2, ng)]
    args = [q, k, v, lq1, lk1, lq2, lk2, ng]
    if has_ctx:
        in_specs += [
            pl.BlockSpec((1, 1, past, GROUP_W), lambda b, i: (b, layer, 0, 0)),
            pl.BlockSpec((1, 1, past, GROUP_W), lambda b, i: (b, layer, 0, 0)),
            pl.BlockSpec((tq, LANES), lambda b, i: (i, 0)),
            pl.BlockSpec((tq, LANES), lambda b, i: (i, 0)),
            pl.BlockSpec((seq_len, LANES), lambda b, i: (0, 0)),
            pl.BlockSpec((seq_len, LANES), lambda b, i: (0, 0)),
        ]
        args += [ck, cv, cos, sin, cos, sin]
    return pl.pallas_call(
        functools.partial(_attn_kernel, has_ctx, seq_len, tq, past, lam_init),
        grid=(nb, nq),
        in_specs=in_specs,
        out_specs=pl.BlockSpec((tq, GROUP_W), lambda b, i: (b * nq + i, 0)),
        out_shape=jax.ShapeDtypeStruct((nb * seq_len, GROUP_W), F32),
        scratch_shapes=[
            pltpu.VMEM((GROUP_W, keys), BF16),
            pltpu.VMEM((DA_HEADS, keys, DA_VDIM), BF16),
        ],
        compiler_params=_cparams(2),
        name="attn_ctx" if has_ctx else "attn",
    )(*args)


def _rope_tables(seq_len):
    t = np.arange(seq_len)
    rowp = (t // GRID_W).astype(np.float64)
    colp = (t % GRID_W).astype(np.float64)
    n_freq = DA_QKDIM // 4
    inv_freq = ROPE_BASE ** (-np.arange(n_freq, dtype=np.float64) / n_freq)
    ang = np.concatenate([rowp[:, None] * inv_freq, colp[:, None] * inv_freq], axis=-1)
    ang = np.repeat(ang, 2, axis=-1)
    sign = np.where(np.arange(DA_QKDIM) % 2 == 0, -1.0, 1.0)
    cos = np.tile(np.cos(ang), (1, LANES // DA_QKDIM)).astype(np.float32)
    sin = np.tile(np.sin(ang) * sign, (1, LANES // DA_QKDIM)).astype(np.float32)
    return jnp.asarray(cos), jnp.asarray(sin)


def _rglru_kernel(has_ctx, seq_len, x_ref, g_ref, cw_ref, cb_ref, wa_ref, ba_ref, wx_ref, bx_ref, lam_ref, *rest):
    if has_ctx:
        h0_ref, y_ref, a_s, u_s = rest
        st_ref = None
    else:
        y_ref, st_ref, a_s, u_s = rest
    nc = seq_len // CH
    nt = CH // SUBLANES
    sub = lax.broadcasted_iota(jnp.int32, (nt, SUBLANES, GROUP_W), 1)

    def gate_step(c, carry):
        r0 = pl.multiple_of(c * CH, CH)
        xc = _conv4(_window(x_ref, c, nc, seq_len), cw_ref, cb_ref)
        xb = xc.astype(BF16)
        for d in range(2):
            rg = jax.nn.sigmoid(_dot(xb, wa_ref[d].astype(BF16)) + ba_ref[d])
            ig = jax.nn.sigmoid(_dot(xb, wx_ref[d].astype(BF16)) + bx_ref[d])
            log_a = -RG_C * rg * _softplus(-lam_ref[d])
            a = jnp.exp(log_a)
            u = jnp.sqrt(-jnp.tanh(log_a) * (a * a + 1.0)) * (ig * xc)
            a3 = a.reshape(nt, SUBLANES, GROUP_W)
            u3 = u.reshape(nt, SUBLANES, GROUP_W)
            for k in (1, 2, 4):
                if d == 0:
                    ok = sub >= k
                    a_sh, u_sh = pltpu.roll(a3, k, axis=1), pltpu.roll(u3, k, axis=1)
                else:
                    ok = sub < SUBLANES - k
                    a_sh, u_sh = pltpu.roll(a3, SUBLANES - k, axis=1), pltpu.roll(u3, SUBLANES - k, axis=1)
                u3 = u3 + a3 * jnp.where(ok, u_sh, 0.0)
                a3 = a3 * jnp.where(ok, a_sh, 1.0)
            a_s[d, pl.ds(r0, CH), :] = a3.reshape(CH, GROUP_W)
            u_s[d, pl.ds(r0, CH), :] = u3.reshape(CH, GROUP_W)
        return carry

    lax.fori_loop(0, nc, gate_step, 0)

    n_tiles = seq_len // SUBLANES
    if has_ctx:
        hf0, hb0 = h0_ref[0, 0, 0:1, :], h0_ref[0, 0, 1:2, :]
    else:
        hf0 = hb0 = jnp.zeros((1, GROUP_W), F32)

    def carry_step(i, carry):
        hf, hb = carry
        rf = pl.multiple_of(i * SUBLANES, SUBLANES)
        rb = pl.multiple_of((n_tiles - 1 - i) * SUBLANES, SUBLANES)
        tf = u_s[0, pl.ds(rf, SUBLANES), :] + a_s[0, pl.ds(rf, SUBLANES), :] * hf
        tb = u_s[1, pl.ds(rb, SUBLANES), :] + a_s[1, pl.ds(rb, SUBLANES), :] * hb
        u_s[0, pl.ds(rf, SUBLANES), :] = tf
        u_s[1, pl.ds(rb, SUBLANES), :] = tb
        return tf[SUBLANES - 1:SUBLANES, :], tb[0:1, :]

    hf, hb = lax.fori_loop(0, n_tiles, carry_step, (hf0, hb0), unroll=4)
    if not has_ctx:
        st_ref[0, 0:1, :] = hf
        st_ref[0, 1:2, :] = hb

    def out_step(c, carry):
        r0 = pl.multiple_of(c * CH, CH)
        g = g_ref[pl.ds(r0, CH), :]
        gelu = g * (0.5 * (1.0 + jnp.tanh(math.sqrt(2.0 / math.pi) * (g + 0.044715 * (g * g * g)))))
        y_ref[pl.ds(r0, CH), :] = (u_s[0, pl.ds(r0, CH), :] + u_s[1, pl.ds(r0, CH), :]) * gelu
        return carry

    lax.fori_loop(0, nc, out_step, 0)


def _rglru(has_ctx, nb, seq_len, layer, x, g, cw, cb, wa, ba, wx, bx, lam, h0=None):
    rows = pl.BlockSpec((seq_len, GROUP_W), lambda b: (b, 0))
    full = lambda a: pl.BlockSpec(a.shape, lambda b: (0,) * a.ndim)
    in_specs = [rows, rows] + [full(a) for a in (cw, cb, wa, ba, wx, bx, lam)]
    args = [x, g, cw, cb, wa, ba, wx, bx, lam]
    y_shape = jax.ShapeDtypeStruct((nb * seq_len, GROUP_W), F32)
    if has_ctx:
        in_specs.append(pl.BlockSpec((1, 1, 2, GROUP_W), lambda b: (b, layer, 0, 0)))
        args.append(h0)
        out_specs, out_shape = rows, y_shape
    else:
        out_specs = [rows, pl.BlockSpec((1, 2, GROUP_W), lambda b: (b, 0, 0))]
        out_shape = [y_shape, jax.ShapeDtypeStruct((nb, 2, GROUP_W), F32)]
    return pl.pallas_call(
        functools.partial(_rglru_kernel, has_ctx, seq_len),
        grid=(nb,),
        in_specs=in_specs,
        out_specs=out_specs,
        out_shape=out_shape,
        scratch_shapes=[pltpu.VMEM((2, seq_len, GROUP_W), F32), pltpu.VMEM((2, seq_len, GROUP_W), F32)],
        compiler_params=_cparams(1),
        name="rglru_ctx" if has_ctx else "rglru",
    )(*args)


ROUTE_OFF = N_EGROUPS
SORT_TM = 256
RUN_PAD = 16
SLOTS = SORT_TM + N_EGROUPS * RUN_PAD
SLOTS_PAD = 384
RUN_BITS = (16, 32, 64, 128, 256)
FFN_BLK = 512
GS_COLS = D_MODEL + LANES


def _routing_gate(logits):
    lane = lax.broadcasted_iota(jnp.int32, logits.shape, 1)
    lane_f = lane.astype(F32)
    neg = -jnp.inf
    big = float(LANES)
    gl = jnp.where(lane < N_EGROUPS, logits, neg)
    gmax = jnp.max(gl, axis=-1, keepdims=True)
    g_w = 1.0 / jnp.sum(jnp.exp(gl - gmax), axis=-1, keepdims=True)
    g_sel = jnp.min(jnp.where(gl == gmax, lane_f, big), axis=-1, keepdims=True)
    e_lane = lane - ROUTE_OFF
    in_grp = (e_lane >= 0) & (e_lane < N_EXPERTS) & ((e_lane // N_EPG).astype(F32) == g_sel)
    el = jnp.where(in_grp, logits, neg)
    m1 = jnp.max(el, axis=-1, keepdims=True)
    i1 = jnp.min(jnp.where(el == m1, lane_f, big), axis=-1, keepdims=True)
    el2 = jnp.where(lane_f == i1, neg, el)
    m2 = jnp.max(el2, axis=-1, keepdims=True)
    i2 = jnp.min(jnp.where(el2 == m2, lane_f, big), axis=-1, keepdims=True)
    r = jnp.exp(m2 - m1)
    p1 = 1.0 / (1.0 + r)
    p2 = r / (1.0 + r)
    return jnp.where(lane_f == i1, g_w * p1, jnp.where(lane_f == i2, g_w * p2, 0.0)), g_sel


def _route_kernel(x_ref, m_ref, mix0, mix1, mix2, mix3, wout_ref, g2_ref, wr_ref, br_ref,
                  x1_ref, h2_ref, gate_ref, cnt_ref, slot_ref):
    tm = SORT_TM
    mix = jnp.concatenate([mix0[...], mix1[...], mix2[...], mix3[...]], axis=1).astype(BF16)
    x1 = x_ref[...] + m_ref[0, 2:3] * _dot(mix, wout_ref[...])
    x1_ref[...] = x1
    h2 = _rms(x1) * g2_ref[...] * (1.0 + m_ref[0, 4:5]) + m_ref[0, 3:4]
    h2_hi = h2.astype(BF16)
    h2_ref[...] = h2_hi
    h2_lo = (h2 - h2_hi.astype(F32)).astype(BF16)
    hi = _dot(h2_hi, wr_ref[...])
    logits = hi[:, :LANES] + hi[:, LANES:] + _dot(h2_lo, wr_ref[:, :LANES]) + br_ref[...]
    gate, g_sel = _routing_gate(logits)
    gate_ref[...] = gate

    lane = lax.broadcasted_iota(jnp.int32, (tm, LANES), 1).astype(F32)
    onehot = jnp.where(lane == g_sel, 1.0, 0.0)
    cnt_ref[0] = jnp.sum(onehot, axis=0, keepdims=True)
    onehot_t = onehot.T
    earlier = jnp.where(lax.broadcasted_iota(jnp.int32, (tm, tm), 0) < lax.broadcasted_iota(jnp.int32, (tm, tm), 1),
                        1.0, 0.0).astype(BF16)
    rank_t = _dot(onehot_t.astype(BF16), earlier)
    cnt = jnp.sum(onehot_t, axis=1, keepdims=True)
    padded = jnp.ceil(cnt * (1.0 / RUN_PAD)) * RUN_PAD
    grp = lax.broadcasted_iota(jnp.int32, (LANES, 1), 0)
    start = jnp.zeros((LANES, 1), F32)
    for g in range(N_EGROUPS - 1):
        start = start + jnp.where(grp > g, padded[g:g + 1, :], 0.0)
    slot = jnp.sum(onehot_t * (rank_t + start), axis=0, keepdims=True)
    slot_ref[0] = slot


def _route(x, mod, mod_row, mixes, wout, g2, wr, br):
    t = x.shape[0]
    tm = SORT_TM
    nt = t // tm
    tok = lambda n: pl.BlockSpec((tm, n), lambda i: (i, 0))
    const = lambda a: pl.BlockSpec(a.shape, lambda i: (0,) * a.ndim)
    return pl.pallas_call(
        _route_kernel,
        grid=(nt,),
        in_specs=[tok(D_MODEL), pl.BlockSpec((1, 6, D_MODEL), lambda i: (mod_row(i), 0, 0)),
                  tok(GROUP_W), tok(GROUP_W), tok(GROUP_W), tok(GROUP_W),
                  const(wout), const(g2), const(wr), const(br)],
        out_specs=[tok(D_MODEL), tok(D_MODEL), tok(LANES),
                   pl.BlockSpec((1, 1, LANES), lambda i: (i, 0, 0)), pl.BlockSpec((1, 1, tm), lambda i: (i, 0, 0))],
        out_shape=[jax.ShapeDtypeStruct((t, D_MODEL), F32), jax.ShapeDtypeStruct((t, D_MODEL), BF16),
                   jax.ShapeDtypeStruct((t, LANES), F32),
                   jax.ShapeDtypeStruct((nt, 1, LANES), F32), jax.ShapeDtypeStruct((nt, 1, tm), F32)],
        compiler_params=_cparams(1),
        name="route",
    )(x, mod, *mixes, wout, g2, wr, br)


def _for_each_piece(n_rows, fn):
    off = 0
    for bit in RUN_BITS:
        has = (n_rows & bit) != 0
        pl.when(has)(functools.partial(fn, off, bit))
        off = off + jnp.where(has, bit, 0)


def _regroup_kernel(loc0_ref, len_ref, dst0_ref, tail0_ref, tail_len_ref, slack_ref, h2_ref, gate_ref, slot_ref,
                    gs_hbm, srt_s, zero_s, sem, zsem):
    i = pl.program_id(0)
    n = pl.num_programs(0)
    tm = SORT_TM
    start, wait = (lambda cp: cp.start()), (lambda cp: cp.wait())

    def zero_pieces(do):
        def tail_copy(g, off, size):
            return pltpu.make_async_copy(zero_s.at[pl.ds(0, size)],
                                         gs_hbm.at[pl.ds(pl.multiple_of(tail0_ref[g] + off, RUN_PAD), size)], zsem)

        def slack_copy(j):
            size = RUN_BITS[-1]
            return pltpu.make_async_copy(zero_s, gs_hbm.at[pl.ds(pl.multiple_of(slack_ref[0] + j * size, size), size)], zsem)

        for g in range(N_EGROUPS):
            _for_each_piece(tail_len_ref[g], lambda off, size, g=g: do(tail_copy(g, off, size)))
        lax.fori_loop(0, slack_ref[1], lambda j, c: (do(slack_copy(j)), c)[1], 0)

    def run_pieces(tile, do):
        buf = tile % 2
        for g in range(N_EGROUPS):
            r = tile * N_EGROUPS + g

            def piece(off, size, r=r):
                do(pltpu.make_async_copy(
                    srt_s.at[buf, pl.ds(pl.multiple_of(loc0_ref[r] + off, RUN_PAD), size)],
                    gs_hbm.at[pl.ds(pl.multiple_of(dst0_ref[r] + off, RUN_PAD), size)], sem.at[buf]))

            _for_each_piece(len_ref[r], piece)

    @pl.when(i == 0)
    def _zeros():
        zero_s[...] = jnp.zeros_like(zero_s)
        zero_pieces(start)

    @pl.when(i >= 2)
    def _reuse():
        run_pieces(i - 2, wait)

    perm = lax.broadcasted_iota(jnp.int32, (SLOTS, tm), 0).astype(F32) == slot_ref[0]
    perm = jnp.where(perm, 1.0, 0.0).astype(BF16)
    srt_h = _dot(perm, h2_ref[...])
    gate = gate_ref[...]
    g_hi = gate.astype(BF16)
    g_mid = (gate - g_hi.astype(F32)).astype(BF16)
    g_lo = (gate - g_hi.astype(F32) - g_mid.astype(F32)).astype(BF16)
    parts = _dot(perm, jnp.concatenate([g_hi, g_mid], axis=1))
    srt_g = parts[:, :LANES] + parts[:, LANES:] + _dot(perm, g_lo)
    srt_s[i % 2] = jnp.concatenate([srt_h, srt_g], axis=1)
    run_pieces(i, start)

    @pl.when(i == n - 1)
    def _drain():
        pl.when(i >= 1)(lambda: run_pieces(i - 1, wait))
        run_pieces(i, wait)
        zero_pieces(wait)


def _regroup(h2, gate, slot, loc0, length, dst0, tail0, tail_len, slack, n_rows):
    t = h2.shape[0]
    tm = SORT_TM
    return pl.pallas_call(
        _regroup_kernel,
        grid_spec=pltpu.PrefetchScalarGridSpec(
            num_scalar_prefetch=6,
            grid=(t // tm,),
            in_specs=[pl.BlockSpec((tm, D_MODEL), lambda i, *_: (i, 0)),
                      pl.BlockSpec((tm, LANES), lambda i, *_: (i, 0)),
                      pl.BlockSpec((1, 1, tm), lambda i, *_: (i, 0, 0))],
            out_specs=pl.BlockSpec(memory_space=pl.ANY),
            scratch_shapes=[pltpu.VMEM((2, SLOTS, GS_COLS), F32), pltpu.VMEM((RUN_BITS[-1], GS_COLS), F32),
                            pltpu.SemaphoreType.DMA((2,)), pltpu.SemaphoreType.DMA(())],
        ),
        out_shape=jax.ShapeDtypeStruct((n_rows, GS_COLS), F32),
        compiler_params=_cparams(1),
        name="regroup",
    )(loc0, length, dst0, tail0, tail_len, slack, h2, gate, slot)


def _ffn_kernel(blk_grp_ref, n_valid_ref, gs_ref, wg_ref, wu_ref, wd_ref, ys_ref):
    b = pl.program_id(0)

    @pl.when(b >= n_valid_ref[0])
    def _unused():
        ys_ref[...] = jnp.zeros_like(ys_ref)

    @pl.when(b < n_valid_ref[0])
    def _block():
        grp = blk_grp_ref[b]
        xs = gs_ref[:, :D_MODEL].astype(BF16)
        gates = gs_ref[:, D_MODEL:]
        lane = lax.broadcasted_iota(jnp.int32, gates.shape, 1)
        hid = []
        for j in range(N_EPG):
            gcol = jnp.sum(jnp.where(lane == grp * N_EPG + j + ROUTE_OFF, gates, 0.0), axis=-1, keepdims=True)
            hj = _silu(_dot(xs, wg_ref[j])) * _dot(xs, wu_ref[j])
            hid.append((hj * gcol).astype(BF16))
        ys_ref[...] = _dot(jnp.concatenate(hid, axis=1), wd_ref[...]).astype(BF16)


def _group_ffn(gs, blk_grp, n_valid, wg, wu, wd):
    nblk = gs.shape[0] // FFN_BLK
    live = lambda b, bg, nv: jnp.minimum(b, nv[0] - 1)
    return pl.pallas_call(
        _ffn_kernel,
        grid_spec=pltpu.PrefetchScalarGridSpec(
            num_scalar_prefetch=2,
            grid=(nblk,),
            in_specs=[
                pl.BlockSpec((FFN_BLK, GS_COLS), lambda b, bg, nv: (live(b, bg, nv), 0)),
                pl.BlockSpec((N_EPG, D_MODEL, EXPERT_FF), lambda b, bg, nv: (bg[b], 0, 0)),
                pl.BlockSpec((N_EPG, D_MODEL, EXPERT_FF), lambda b, bg, nv: (bg[b], 0, 0)),
                pl.BlockSpec((N_EPG * EXPERT_FF, D_MODEL), lambda b, bg, nv: (bg[b], 0)),
            ],
            out_specs=pl.BlockSpec((FFN_BLK, D_MODEL), lambda b, bg, nv: (b, 0)),
        ),
        out_shape=jax.ShapeDtypeStruct((nblk * FFN_BLK, D_MODEL), BF16),
        compiler_params=_cparams(1),
        name="group_ffn",
    )(blk_grp, n_valid, gs, wg, wu, wd)


def _combine_kernel(final, loc0_ref, len_ref, dst0_ref, x1_ref, m_ref, slot_ref, gfin_ref, ys_hbm, o_ref, run_s, sem):
    i = pl.program_id(0)
    tm = SORT_TM

    def fetch(tile, do):
        buf = tile % 2
        for g in range(N_EGROUPS):
            r = tile * N_EGROUPS + g

            def piece(off, size, r=r):
                do(pltpu.make_async_copy(
                    ys_hbm.at[pl.ds(pl.multiple_of(dst0_ref[r] + off, RUN_PAD), size)],
                    run_s.at[buf, pl.ds(pl.multiple_of(loc0_ref[r] + off, RUN_PAD), size)], sem.at[buf]))

            _for_each_piece(len_ref[r], piece)

    @pl.when(i == 0)
    def _first():
        fetch(i, lambda cp: cp.start())

    @pl.when(i + 1 < pl.num_programs(0))
    def _prefetch():
        fetch(i + 1, lambda cp: cp.start())

    fetch(i, lambda cp: cp.wait())

    used = loc0_ref[i * N_EGROUPS + N_EGROUPS - 1] + len_ref[i * N_EGROUPS + N_EGROUPS - 1]
    rows = lax.broadcasted_iota(jnp.int32, (SLOTS_PAD, D_MODEL), 0)
    y_run = jnp.where(rows < used, run_s[i % 2], jnp.zeros((), BF16))
    slot_b = jnp.broadcast_to(slot_ref[0], (LANES, tm)).T
    lane = lax.broadcasted_iota(jnp.int32, (tm, LANES), 1).astype(F32)
    inv = jnp.concatenate([jnp.where(slot_b == lane + float(c), 1.0, 0.0) for c in range(0, SLOTS_PAD, LANES)],
                          axis=1).astype(BF16)
    x2 = x1_ref[...] + m_ref[0, 5:6] * _dot(inv, y_run)
    o_ref[...] = _rms(x2) * gfin_ref[...] if final else x2


def _combine(final, x1, mod, mod_row, slot, gfin, ys, loc0, length, dst0):
    t = x1.shape[0]
    tm = SORT_TM
    return pl.pallas_call(
        functools.partial(_combine_kernel, final),
        grid_spec=pltpu.PrefetchScalarGridSpec(
            num_scalar_prefetch=3,
            grid=(t // tm,),
            in_specs=[
                pl.BlockSpec((tm, D_MODEL), lambda i, *_: (i, 0)),
                pl.BlockSpec((1, 6, D_MODEL), lambda i, *_: (mod_row(i), 0, 0)),
                pl.BlockSpec((1, 1, tm), lambda i, *_: (i, 0, 0)),
                pl.BlockSpec((1, D_MODEL), lambda i, *_: (0, 0)),
                pl.BlockSpec(memory_space=pl.ANY),
            ],
            out_specs=pl.BlockSpec((tm, D_MODEL), lambda i, *_: (i, 0)),
            scratch_shapes=[pltpu.VMEM((2, SLOTS_PAD, D_MODEL), BF16), pltpu.SemaphoreType.DMA((2,))],
        ),
        out_shape=jax.ShapeDtypeStruct((t, D_MODEL), F32),
        compiler_params=_cparams(1),
        name="combine",
    )(loc0, length, dst0, x1, mod, slot, gfin, ys)


def _outproj_moe(final, x, mod, mod_row, mixes, wout, g2, wr, br, wg, wu, wd, gfin):
    t = x.shape[0]
    nt = t // SORT_TM
    x1, h2, gate, cnt, slot = _route(x, mod, mod_row, mixes, wout, g2, wr, br)
    cnt = cnt[:, 0, :N_EGROUPS].astype(jnp.int32)
    length = (cnt + RUN_PAD - 1) // RUN_PAD * RUN_PAD
    loc0 = jnp.cumsum(length, axis=1) - length
    g_rows = jnp.sum(length, axis=0)
    g_blocks = (g_rows + FFN_BLK - 1) // FFN_BLK
    g_base = (jnp.cumsum(g_blocks) - g_blocks) * FFN_BLK
    dst0 = g_base[None, :] + jnp.cumsum(length, axis=0) - length
    n_blocks = (t + nt * N_EGROUPS * (RUN_PAD - 1) + FFN_BLK - 1) // FFN_BLK + N_EGROUPS
    blk_grp = jnp.minimum(jnp.sum(jnp.arange(n_blocks)[:, None] >= jnp.cumsum(g_blocks)[None, :], axis=1),
                          N_EGROUPS - 1).astype(jnp.int32)
    n_valid = jnp.sum(g_blocks).astype(jnp.int32)[None]
    flat = lambda a: a.reshape(-1).astype(jnp.int32)
    slack = jnp.stack([n_valid[0] * FFN_BLK, (n_blocks - n_valid[0]) * (FFN_BLK // RUN_BITS[-1])]).astype(jnp.int32)
    gs = _regroup(h2, gate, slot, flat(loc0), flat(length), flat(dst0), flat(g_base + g_rows),
                  flat(g_blocks * FFN_BLK - g_rows), slack, n_blocks * FFN_BLK)
    ys = _group_ffn(gs, blk_grp, n_valid, wg, wu, wd)
    return _combine(final, x1, mod, mod_row, slot, gfin, ys, flat(loc0), flat(length), flat(dst0))


def _block_diag(w):
    n, k, _ = w.shape
    eye = jnp.eye(n, dtype=w.dtype)
    return (eye[:, None, :, None] * w[:, :, None, :]).reshape(n * k, n * k)


def _pad_lanes(v, n=LANES):
    return jnp.pad(v, ((0, 0), (0, n - v.shape[-1])))


def kernel(x_prompt, x_sample, c, cache_k, cache_v, state_ssd, state_rglru, c_ctx, w_mod, b_mod, norm1_g, norm2_g, w_in, w_out, ssd_conv_w, ssd_conv_b, ssd_dt_bias, ssd_a_log, ssd_d, ssd_norm_g, pool_w, pool_scale, da_lam_q1, da_lam_k1, da_lam_q2, da_lam_k2, da_norm_g, rg_conv_w, rg_conv_b, rg_wa, rg_ba, rg_wx, rg_bx, rg_lambda, moe_w_group, moe_b_group, moe_w_expert, moe_b_expert, moe_w_gate, moe_w_up, moe_w_down, final_norm_g):
    nbp, lp, _ = x_prompt.shape
    nbs, ls, _ = x_sample.shape
    past = cache_k.shape[2]
    tm_in = 1024
    assert nbs + 1 <= MOD_ROWS and lp % CH == 0 and ls % CH == 0
    assert tm_in % lp == 0 and (nbp * lp) % tm_in == 0 and ls % tm_in == 0 and lp % SORT_TM == 0 and ls % SORT_TM == 0
    assert lp % KEY_BLK == 0 and ls % KEY_BLK == 0 and past % KEY_BLK == 0

    cond = jnp.concatenate([c_ctx[None, :], c, jnp.zeros((MOD_ROWS - 1 - nbs, D_MODEL), F32)], axis=0)
    mod = _modulation(cond, w_mod, b_mod).reshape(DEPTH * MOD_ROWS, 6, D_MODEL)

    w_in_r = jnp.concatenate([w_in[:, :, :DT_LO], w_in[:, :, DT_HI:], w_in[:, :, DT_LO:DT_HI],
                              jnp.zeros((DEPTH, D_MODEL, LANES - (DT_HI - DT_LO)), F32)], axis=-1).astype(BF16)
    w_out_b = w_out.astype(BF16)
    w_gate_b = moe_w_gate.astype(BF16)
    w_up_b = moe_w_up.astype(BF16)
    w_down_b = moe_w_down.astype(BF16).reshape(DEPTH, N_EXPERTS * EXPERT_FF, D_MODEL)
    w_route = jnp.concatenate([moe_w_group, moe_w_expert,
                               jnp.zeros((DEPTH, D_MODEL, LANES - N_EGROUPS - N_EXPERTS), F32)], axis=-1)
    w_route_hi = w_route.astype(BF16)
    w_route = jnp.concatenate([w_route_hi, (w_route - w_route_hi.astype(F32)).astype(BF16)], axis=-1)
    b_route = _pad_lanes(jnp.concatenate([moe_b_group, moe_b_expert], axis=-1))
    dtb = _pad_lanes(ssd_dt_bias.reshape(DEPTH, 2 * SSD_HEADS))
    alog = _pad_lanes(ssd_a_log.reshape(DEPTH, 2 * SSD_HEADS))
    d_skip = jnp.repeat(ssd_d, SSD_HEADDIM, axis=-1)
    cos, sin = _rope_tables(ls)
    ck = cache_k.reshape(nbs, DEPTH, past, GROUP_W)
    cv = cache_v.reshape(nbs, DEPTH, past, GROUP_W)
    g_fin = final_norm_g[None, :]

    xp = x_prompt.reshape(nbp * lp, D_MODEL)
    xs = x_sample.reshape(nbs * ls, D_MODEL)
    new_k = jnp.zeros((nbp, DEPTH, lp, GROUP_W), F32)
    new_v = jnp.zeros((nbp, DEPTH, lp, GROUP_W), F32)
    ssd_out, rg_out = [], []
    for l in range(DEPTH):
        row1 = lambda a: a[l][None, :]
        final = l == DEPTH - 1
        lam_init = 0.8 - 0.6 * math.exp(-0.3 * l)
        ssd_w = (ssd_conv_w[l], row1(ssd_conv_b), row1(dtb), row1(alog), row1(d_skip), row1(ssd_norm_g))
        pool_wb = _block_diag(pool_w[l])
        att_w = (row1(da_lam_q1), row1(da_lam_k1), row1(da_lam_q2), row1(da_lam_k2), row1(da_norm_g))
        rg_w = (rg_conv_w[l], row1(rg_conv_b),
                jnp.stack([_block_diag(rg_wa[l, 0]), _block_diag(rg_wa[l, 1])]), rg_ba[l][:, None, :],
                jnp.stack([_block_diag(rg_wx[l, 0]), _block_diag(rg_wx[l, 1])]), rg_bx[l][:, None, :],
                rg_lambda[l][:, None, :])
        moe_w = (w_out_b[l], row1(norm2_g), w_route[l], row1(b_route), w_gate_b[l], w_up_b[l], w_down_b[l], g_fin)
        ctx_row = lambda i, l=l: l * MOD_ROWS
        lat_row = lambda tm: (lambda i, l=l: l * MOD_ROWS + 1 + i // (ls // tm))

        xbc, z, xpool, q, new_k, new_v, xr, gr, dt = _inproj(
            xp, mod, ctx_row, row1(norm1_g), w_in_r[l], tm_in, l, caches=(new_k, new_v))
        ya, st_ssd = _ssd(False, nbp, lp, l, xbc, z, dt, *ssd_w)
        yb = _pool(nbp, lp, xpool, pool_wb, row1(pool_scale))
        yc = _attn(False, nbp, lp, l, lam_init, q, new_k, new_v, *att_w)
        yd, st_rg = _rglru(False, nbp, lp, l, xr, gr, *rg_w)
        xp = _outproj_moe(final, xp, mod, ctx_row, (ya, yb, yc, yd), *moe_w)
        ssd_out.append(st_ssd)
        rg_out.append(st_rg)

        xbc, z, xpool, q, k, v, xr, gr, dt = _inproj(xs, mod, lat_row(tm_in), row1(norm1_g), w_in_r[l], tm_in, l)
        ya = _ssd(True, nbs, ls, l, xbc, z, dt, *ssd_w, h0=state_ssd)
        yb = _pool(nbs, ls, xpool, pool_wb, row1(pool_scale))
        yc = _attn(True, nbs, ls, l, lam_init, q, k, v, *att_w, ck=ck, cv=cv, cos=cos, sin=sin)
        yd = _rglru(True, nbs, ls, l, xr, gr, *rg_w, h0=state_rglru)
        xs = _outproj_moe(final, xs, mod, lat_row(SORT_TM), (ya, yb, yc, yd), *moe_w)

    return (xp.reshape(nbp, lp, D_MODEL), xs.reshape(nbs, ls, D_MODEL),
            new_k.reshape(nbp, DEPTH, lp, DA_HEADS, 2 * DA_QKDIM), new_v.reshape(nbp, DEPTH, lp, DA_HEADS, DA_VDIM),
            jnp.stack(ssd_out, axis=1), jnp.stack(rg_out, axis=1))
```

```python
import functools
import math

import numpy as np
import jax
import jax.numpy as jnp
from jax import lax
from jax.experimental import pallas as pl
from jax.experimental.pallas import tpu as pltpu

F32 = jnp.float32
BF16 = jnp.bfloat16
HIGHEST = lax.Precision.HIGHEST

D_MODEL = 1024
DEPTH = 4
GRID_W = 64
GROUP_W = 256
EPS = 1e-6
SSD_HEADDIM = 64
SSD_HEADS = 4
SSD_STATE = 64
SSD_BC = 128
SSD_CONV_CH = 512
POOL_WINDOWS = (2, 4, 8, 16)
DA_HEADS = 4
DA_VDIM = 64
DA_QKDIM = 32
ROPE_BASE = 10000.0
RG_C = 8.0
N_EGROUPS = 4
N_EPG = 4
N_EXPERTS = 16
EXPERT_FF = 256
DT_LO, DT_HI = 768, 776

LANES = 128
SUBLANES = 8
CH = 128
HALO = SUBLANES
MOD_ROWS = 16
VMEM_LIMIT = 56 * 1024 * 1024


def _cparams(n_axes):
    return pltpu.CompilerParams(dimension_semantics=("arbitrary",) * n_axes, vmem_limit_bytes=VMEM_LIMIT)


def _silu(x):
    return x * jax.nn.sigmoid(x)


def _softplus(x):
    return jnp.maximum(x, 0.0) + jnp.log1p(jnp.exp(-jnp.abs(x)))


def _dot(a, b, **kw):
    return jnp.dot(a, b, preferred_element_type=F32, **kw)


def _dot_nt(a, b):
    return lax.dot_general(a, b, (((1,), (1,)), ((), ())), preferred_element_type=F32)


def _rms(x):
    return x * lax.rsqrt(jnp.mean(x * x, axis=-1, keepdims=True) + EPS)


def _window(ref, c, n_steps, seq_len):
    r0 = pl.multiple_of(c * CH, CH)
    main = ref[pl.ds(r0, CH), :]
    lo = pl.multiple_of(jnp.maximum(r0 - HALO, 0), HALO)
    hi = pl.multiple_of(jnp.minimum(r0 + CH, seq_len - HALO), HALO)
    prev = jnp.where(c > 0, ref[pl.ds(lo, HALO), :], 0.0)
    nxt = jnp.where(c < n_steps - 1, ref[pl.ds(hi, HALO), :], 0.0)
    return jnp.concatenate([prev, main, nxt], axis=0)


def _conv4(win, w_ref, b_ref):
    acc = b_ref[...]
    for k in range(4):
        acc = acc + w_ref[k:k + 1, :] * win[HALO - 1 + k:HALO - 1 + k + CH, :]
    return acc


def _mod_kernel(cond_ref, w_ref, b_ref, o_ref):
    cnd = cond_ref[...]
    o_ref[0] = _dot(_silu(cnd), w_ref[0], precision=HIGHEST) + b_ref[0]


def _modulation(cond, w_mod, b_mod):
    nb = 6
    return pl.pallas_call(
        _mod_kernel,
        grid=(DEPTH, nb),
        in_specs=[
            pl.BlockSpec((MOD_ROWS, D_MODEL), lambda l, j: (0, 0)),
            pl.BlockSpec((1, D_MODEL, D_MODEL), lambda l, j: (l, 0, j)),
            pl.BlockSpec((1, 1, D_MODEL), lambda l, j: (l, 0, j)),
        ],
        out_specs=pl.BlockSpec((1, MOD_ROWS, D_MODEL), lambda l, j: (l, 0, j)),
        out_shape=jax.ShapeDtypeStruct((DEPTH, MOD_ROWS, nb * D_MODEL), F32),
        compiler_params=_cparams(2),
        name="modulation",
    )(cond, w_mod, b_mod.reshape(DEPTH, 1, nb * D_MODEL))


IN_COLS = (512, 256, 256, 256, 256, 256, 256, 256, LANES)
K_OUT, V_OUT = 4, 5


def _inproj_kernel(to_cache, x_ref, m_ref, g_ref, w_ref, *refs):
    if to_cache:
        refs = refs[2:]
    hh = _rms(x_ref[...]) * g_ref[...] * (1.0 + m_ref[0, 1:2]) + m_ref[0, 0:1]
    u = _dot(hh.astype(BF16), w_ref[...])
    off = 0
    for j, (ref, n) in enumerate(zip(refs, IN_COLS)):
        if to_cache and j in (K_OUT, V_OUT):
            ref[:, 0] = u[:, off:off + n].reshape(ref.shape[0], ref.shape[2], n)
        else:
            ref[...] = u[:, off:off + n]
        off += n


def _inproj(x, mod, mod_row, g, w, tm, layer, caches=None):
    t = x.shape[0]
    ncol = sum(IN_COLS)
    to_cache = caches is not None
    in_specs = [
        pl.BlockSpec((tm, D_MODEL), lambda i: (i, 0)),
        pl.BlockSpec((1, 6, D_MODEL), lambda i: (mod_row(i), 0, 0)),
        pl.BlockSpec((1, D_MODEL), lambda i: (0, 0)),
        pl.BlockSpec((D_MODEL, ncol), lambda i: (0, 0)),
    ]
    out_specs = [pl.BlockSpec((tm, n), lambda i: (i, 0)) for n in IN_COLS]
    out_shape = [jax.ShapeDtypeStruct((t, n), F32) for n in IN_COLS]
    args = [x, mod, g, w]
    aliases = {}
    if to_cache:
        seq = caches[0].shape[2]
        assert tm % seq == 0
        cache_spec = pl.BlockSpec((tm // seq, 1, seq, GROUP_W), lambda i: (i, layer, 0, 0))
        for j, cch in zip((K_OUT, V_OUT), caches):
            in_specs.append(pl.BlockSpec(memory_space=pl.ANY))
            aliases[len(args)] = j
            args.append(cch)
            out_specs[j] = cache_spec
            out_shape[j] = jax.ShapeDtypeStruct(cch.shape, F32)
    return pl.pallas_call(
        functools.partial(_inproj_kernel, to_cache),
        grid=(t // tm,),
        in_specs=in_specs,
        out_specs=out_specs,
        out_shape=out_shape,
        input_output_aliases=aliases,
        compiler_params=_cparams(1),
        name="inproj_ctx" if to_cache else "inproj",
    )(*args)


def _ssd_kernel(has_ctx, seq_len, xbc_ref, z_ref, dt_ref, cw_ref, cb_ref, dtb_ref, alog_ref, d_ref, ng_ref, *rest):
    if has_ctx:
        h0_ref, y_ref, xc_s, y_s, st_s, upd_s, grow_s, keep_s = rest
        st_ref = None
    else:
        y_ref, st_ref, xc_s, y_s, st_s, upd_s, grow_s, keep_s = rest
    nc = seq_len // CH
    hd = SSD_HEADDIM
    a_row = -jnp.exp(alog_ref[...])
    row = lax.broadcasted_iota(jnp.int32, (CH, CH), 0)
    col = lax.broadcasted_iota(jnp.int32, (CH, CH), 1)
    lane_w = lax.broadcasted_iota(jnp.int32, (CH, GROUP_W), 1)
    lane_n = lax.broadcasted_iota(jnp.int32, (CH, SSD_BC), 1)
    own_block = (lax.broadcasted_iota(jnp.int32, (SSD_BC, GROUP_W), 0) // SSD_STATE
                 == lax.broadcasted_iota(jnp.int32, (SSD_BC, GROUP_W), 1) // (2 * hd))

    def conv_step(c, carry):
        r0 = pl.multiple_of(c * CH, CH)
        xc = _silu(_conv4(_window(xbc_ref, c, nc, seq_len), cw_ref, cb_ref))
        xc_s[pl.ds(r0, CH), :] = xc
        y_s[pl.ds(r0, CH), :] = xc[:, :GROUP_W] * d_ref[...]
        return carry

    lax.fori_loop(0, nc, conv_step, 0)

    st_s[...] = jnp.zeros_like(st_s)
    if has_ctx:
        for d in range(2):
            for h in range(SSD_HEADS):
                g = h // 2
                st_s[d, g * SSD_STATE:(g + 1) * SSD_STATE, h * hd:(h + 1) * hd] = h0_ref[0, 0, d, h].T

    def per_head(v, d):
        lanes = lane_w[:v.shape[0]]
        out = jnp.broadcast_to(v[:, 4 * d + 3:4 * d + 4], (v.shape[0], GROUP_W))
        for h in (2, 1, 0):
            out = jnp.where(lanes < (h + 1) * hd, jnp.broadcast_to(v[:, 4 * d + h:4 * d + h + 1], out.shape), out)
        return out

    def local_step(c, carry):
        r0 = pl.multiple_of(c * CH, CH)
        xc = xc_s[pl.ds(r0, CH), :]
        x = xc[:, :GROUP_W]
        bm = xc[:, GROUP_W:GROUP_W + SSD_BC]
        cm = xc[:, GROUP_W + SSD_BC:]
        dt = _softplus(dt_ref[pl.ds(r0, CH), :] + dtb_ref[...])
        bmb = bm.astype(BF16)
        bm_t = bm.T.astype(BF16)
        scores = [_dot_nt(jnp.where((lane_n // SSD_STATE) == g, cm, 0.0).astype(BF16), bmb) for g in range(2)]
        y_diag = None
        da = dt * a_row
        pre = da
        for k in (1, 2, 4, 8, 16, 32, 64):
            pre = pre + jnp.where(row >= k, pltpu.roll(pre, k, axis=0), 0.0)
        for d in range(2):
            tri = (row >= col) if d == 0 else (row <= col)
            cs = pre if d == 0 else pre[CH - 1:CH, :] - pre + da
            cs_t = cs.T
            tot = cs[CH - 1:CH, :] if d == 0 else cs[0:1, :]
            dt_e, cs_e, tot_e = per_head(dt, d), per_head(cs, d), per_head(tot, d)
            xdt = x * dt_e
            m_parts, r_parts = [], []
            for h in range(SSD_HEADS):
                k = SSD_HEADS * d + h
                decay = jnp.exp(jnp.where(tri, cs[:, k:k + 1] - cs_t[k:k + 1, :], -jnp.inf))
                m_parts.append((scores[h // 2] * decay).astype(BF16))
                r_parts.append(jnp.where((lane_w // hd) == h, xdt, 0.0).astype(BF16))
            yd = _dot(jnp.concatenate(m_parts, axis=1), jnp.concatenate(r_parts, axis=0))
            y_diag = yd if y_diag is None else y_diag + yd
            wgt = xdt * jnp.exp(tot_e - cs_e)
            upd_s[d, c] = jnp.where(own_block, _dot(bm_t, wgt.astype(BF16)), 0.0)
            grow_s[d, pl.ds(r0, CH), :] = jnp.exp(cs_e)
            keep_s[d, pl.ds(c, 1), :] = jnp.exp(tot_e)
        y_s[pl.ds(r0, CH), :] += y_diag
        return carry

    lax.fori_loop(0, nc, local_step, 0, unroll=2)

    def state_dir(d, ci):
        r0 = pl.multiple_of(ci * CH, CH)
        st = st_s[d]
        cmb = xc_s[pl.ds(r0, CH), GROUP_W + SSD_BC:].astype(BF16)
        y_s[pl.ds(r0, CH), :] += _dot(cmb, st.astype(BF16)) * grow_s[d, pl.ds(r0, CH), :]
        st_s[d] = st * keep_s[d, pl.ds(ci, 1), :] + upd_s[d, ci]

    def scan_step(c, carry):
        state_dir(0, c)
        state_dir(1, nc - 1 - c)
        return carry

    lax.fori_loop(0, nc, scan_step, 0, unroll=2)

    def out_step(c, carry):
        r0 = pl.multiple_of(c * CH, CH)
        y = y_s[pl.ds(r0, CH), :] * _silu(z_ref[pl.ds(r0, CH), :])
        y_ref[pl.ds(r0, CH), :] = _rms(y) * ng_ref[...]
        return carry

    lax.fori_loop(0, nc, out_step, 0)
    if not has_ctx:
        for d in range(2):
            for h in range(SSD_HEADS):
                g = h // 2
                st_ref[0, d, h] = st_s[d, g * SSD_STATE:(g + 1) * SSD_STATE, h * hd:(h + 1) * hd].T


def _ssd(has_ctx, nb, seq_len, layer, xbc, z, dt, cw, cb, dtb, alog, dsk, ng, h0=None):
    rows = lambda n: pl.BlockSpec((seq_len, n), lambda b: (b, 0))
    full = lambda a: pl.BlockSpec(a.shape, lambda b: (0,) * a.ndim)
    in_specs = [rows(SSD_CONV_CH), rows(GROUP_W), rows(LANES)] + [full(a) for a in (cw, cb, dtb, alog, dsk, ng)]
    args = [xbc, z, dt, cw, cb, dtb, alog, dsk, ng]
    y_spec = rows(GROUP_W)
    y_shape = jax.ShapeDtypeStruct((nb * seq_len, GROUP_W), F32)
    st_blk = (1, 2, SSD_HEADS, SSD_HEADDIM, SSD_STATE)
    if has_ctx:
        in_specs.append(pl.BlockSpec((1, 1) + st_blk[1:], lambda b: (b, layer, 0, 0, 0, 0)))
        args.append(h0)
        out_specs, out_shape = y_spec, y_shape
    else:
        out_specs = [y_spec, pl.BlockSpec(st_blk, lambda b: (b, 0, 0, 0, 0))]
        out_shape = [y_shape, jax.ShapeDtypeStruct((nb,) + st_blk[1:], F32)]
    return pl.pallas_call(
        functools.partial(_ssd_kernel, has_ctx, seq_len),
        grid=(nb,),
        in_specs=in_specs,
        out_specs=out_specs,
        out_shape=out_shape,
        scratch_shapes=[
            pltpu.VMEM((seq_len, SSD_CONV_CH), F32),
            pltpu.VMEM((seq_len, GROUP_W), F32),
            pltpu.VMEM((2, SSD_BC, GROUP_W), F32),
            pltpu.VMEM((2, seq_len // CH, SSD_BC, GROUP_W), F32),
            pltpu.VMEM((2, seq_len, GROUP_W), F32),
            pltpu.VMEM((2, max(seq_len // CH, SUBLANES), GROUP_W), F32),
        ],
        compiler_params=_cparams(1),
        name="ssd_ctx" if has_ctx else "ssd",
    )(*args)


def _pool_kernel(seq_len, x_ref, w_ref, sc_ref, y_ref):
    nc = seq_len // CH
    wn = CH + 2 * HALO
    lane = lax.broadcasted_iota(jnp.int32, (CH, GROUP_W), 1)
    gw = GROUP_W // len(POOL_WINDOWS)
    half = jnp.where(lane < gw, 1, jnp.where(lane < 2 * gw, 2, jnp.where(lane < 3 * gw, 4, 8)))
    w_blk = w_ref[...].astype(BF16)

    def ahead(v, k):
        return pltpu.roll(v, wn - k, axis=0)

    def step(c, carry):
        r0 = pl.multiple_of(c * CH, CH)
        win = _window(x_ref, c, nc, seq_len)
        p2 = win + ahead(win, 1)
        p4 = p2 + ahead(p2, 2)
        p8 = p4 + ahead(p4, 4)
        p16 = p8 + ahead(p8, 8)
        s2 = ahead(p2, HALO - 1)[:CH]
        s4 = ahead(p4, HALO - 2)[:CH]
        s8 = ahead(p8, HALO - 4)[:CH]
        s16 = p16[:CH]
        tot = jnp.where(lane < gw, s2, jnp.where(lane < 2 * gw, s4, jnp.where(lane < 3 * gw, s8, s16)))
        t = r0 + lax.broadcasted_iota(jnp.int32, (CH, GROUP_W), 0)
        cnt = jnp.minimum(t + half, seq_len) - jnp.maximum(t - half, 0)
        x = win[HALO:HALO + CH]
        diff = tot / cnt.astype(F32) - x
        y_ref[pl.ds(r0, CH), :] = _dot(diff.astype(BF16), w_blk) * sc_ref[...]
        return carry

    lax.fori_loop(0, nc, step, 0)


def _pool(nb, seq_len, x, w_blk, scale):
    return pl.pallas_call(
        functools.partial(_pool_kernel, seq_len),
        grid=(nb,),
        in_specs=[
            pl.BlockSpec((seq_len, GROUP_W), lambda b: (b, 0)),
            pl.BlockSpec((GROUP_W, GROUP_W), lambda b: (0, 0)),
            pl.BlockSpec((1, GROUP_W), lambda b: (0, 0)),
        ],
        out_specs=pl.BlockSpec((seq_len, GROUP_W), lambda b: (b, 0)),
        out_shape=jax.ShapeDtypeStruct((nb * seq_len, GROUP_W), F32),
        compiler_params=_cparams(1),
        name="pool",
    )(x, w_blk, scale)


KEY_BLK = 256


def _rope(x, cos, sin):
    c2 = jnp.concatenate([cos, cos], axis=1)
    s2 = jnp.concatenate([sin, sin], axis=1)
    lane = lax.broadcasted_iota(jnp.int32, x.shape, 1)
    n = x.shape[1]
    partner = jnp.where(lane % 2 == 0, pltpu.roll(x, n - 1, axis=1), pltpu.roll(x, 1, axis=1))
    return x * c2 + partner * s2


def _attn_kernel(has_ctx, seq_len, tq, past, lam_init, q_ref, k_ref, v_ref, lq1, lk1, lq2, lk2, ng_ref, *rest):
    if has_ctx:
        ck_ref, cv_ref, cosq_ref, sinq_ref, cosk_ref, sink_ref, o_ref, kt_s, v_s = rest
    else:
        o_ref, kt_s, v_s = rest
    kb = KEY_BLK

    @pl.when(pl.program_id(1) == 0)
    def _prepare_keys():
        def put(dst0, kk, vv):
            kt_s[:, dst0:dst0 + kb] = kk.T.astype(BF16)
            for h in range(DA_HEADS):
                v_s[h, dst0:dst0 + kb, :] = vv[:, h * DA_VDIM:(h + 1) * DA_VDIM].astype(BF16)

        if has_ctx:
            for j in range(past // kb):
                put(j * kb, ck_ref[0, 0, j * kb:(j + 1) * kb, :], cv_ref[0, 0, j * kb:(j + 1) * kb, :])
        for j in range(seq_len // kb):
            kk = k_ref[0, 0, j * kb:(j + 1) * kb, :] if not has_ctx else k_ref[j * kb:(j + 1) * kb, :]
            vv = v_ref[0, 0, j * kb:(j + 1) * kb, :] if not has_ctx else v_ref[j * kb:(j + 1) * kb, :]
            if has_ctx:
                kk = _rope(kk, cosk_ref[j * kb:(j + 1) * kb, :], sink_ref[j * kb:(j + 1) * kb, :])
            put(past + j * kb, kk, vv)

    q = q_ref[...]
    if has_ctx:
        q = _rope(q, cosq_ref[...], sinq_ref[...])
    q = q * (DA_QKDIM ** -0.5 * math.log2(math.e))
    lam = (jnp.exp(jnp.sum(lq1[...] * lk1[...], axis=-1, keepdims=True))
           - jnp.exp(jnp.sum(lq2[...] * lk2[...], axis=-1, keepdims=True)) + lam_init)
    for h in range(DA_HEADS):
        es, sums = [], []
        for m in range(2):
            lo = h * 2 * DA_QKDIM + m * DA_QKDIM
            s = _dot(q[:, lo:lo + DA_QKDIM].astype(BF16), kt_s[lo:lo + DA_QKDIM, :])
            e = jnp.exp2(s - jnp.max(s, axis=-1, keepdims=True))
            sums.append(jnp.sum(e, axis=-1, keepdims=True))
            es.append(e.astype(BF16))
        o = _dot(jnp.concatenate(es, axis=0), v_s[h])
        acc = o[:tq] / sums[0] - lam * (o[tq:] / sums[1])
        o_ref[:, h * DA_VDIM:(h + 1) * DA_VDIM] = _rms(acc) * ng_ref[...] * (1.0 - lam_init)


def _attn(has_ctx, nb, seq_len, layer, lam_init, q, k, v, lq1, lk1, lq2, lk2, ng, ck=None, cv=None, cos=None, sin=None):
    tq = 256
    nq = seq_len // tq
    past = ck.shape[2] if has_ctx else 0
    keys = seq_len + past
    small = lambda a: pl.BlockSpec(a.shape, lambda b, i: (0,) * a.ndim)
    if has_ctx:
        kv_spec = pl.BlockSpec((seq_len, GROUP_W), lambda b, i: (b, 0))
    else:
        kv_spec = pl.BlockSpec((1, 1, seq_len, GROUP_W), lambda b, i: (b, layer, 0, 0))
    in_specs = [pl.BlockSpec((tq, GROUP_W), lambda b, i: (b * nq + i, 0)), kv_spec, kv_spec]
    in_specs += [small(a) for a in (lq1, lk1, lq2, lk2, ng)]
    args = [q, k, v, lq1, lk1, lq2, lk2, ng]
    if has_ctx:
        in_specs += [
            pl.BlockSpec((1, 1, past, GROUP_W), lambda b, i: (b, layer, 0, 0)),
            pl.BlockSpec((1, 1, past, GROUP_W), lambda b, i: (b, layer, 0, 0)),
            pl.BlockSpec((tq, LANES), lambda b, i: (i, 0)),
            pl.BlockSpec((tq, LANES), lambda b, i: (i, 0)),
            pl.BlockSpec((seq_len, LANES), lambda b, i: (0, 0)),
            pl.BlockSpec((seq_len, LANES), lambda b, i: (0, 0)),
        ]
        args += [ck, cv, cos, sin, cos, sin]
    return pl.pallas_call(
        functools.partial(_attn_kernel, has_ctx, seq_len, tq, past, lam_init),
        grid=(nb, nq),
        in_specs=in_specs,
        out_specs=pl.BlockSpec((tq, GROUP_W), lambda b, i: (b * nq + i, 0)),
        out_shape=jax.ShapeDtypeStruct((nb * seq_len, GROUP_W), F32),
        scratch_shapes=[
            pltpu.VMEM((GROUP_W, keys), BF16),
            pltpu.VMEM((DA_HEADS, keys, DA_VDIM), BF16),
        ],
        compiler_params=_cparams(2),
        name="attn_ctx" if has_ctx else "attn",
    )(*args)


def _rope_tables(seq_len):
    t = np.arange(seq_len)
    rowp = (t // GRID_W).astype(np.float64)
    colp = (t % GRID_W).astype(np.float64)
    n_freq = DA_QKDIM // 4
    inv_freq = ROPE_BASE ** (-np.arange(n_freq, dtype=np.float64) / n_freq)
    ang = np.concatenate([rowp[:, None] * inv_freq, colp[:, None] * inv_freq], axis=-1)
    ang = np.repeat(ang, 2, axis=-1)
    sign = np.where(np.arange(DA_QKDIM) % 2 == 0, -1.0, 1.0)
    cos = np.tile(np.cos(ang), (1, LANES // DA_QKDIM)).astype(np.float32)
    sin = np.tile(np.sin(ang) * sign, (1, LANES // DA_QKDIM)).astype(np.float32)
    return jnp.asarray(cos), jnp.asarray(sin)


def _rglru_kernel(has_ctx, seq_len, x_ref, g_ref, cw_ref, cb_ref, wa_ref, ba_ref, wx_ref, bx_ref, lam_ref, *rest):
    if has_ctx:
        h0_ref, y_ref, a_s, u_s = rest
        st_ref = None
    else:
        y_ref, st_ref, a_s, u_s = rest
    nc = seq_len // CH
    nt = CH // SUBLANES
    sub = lax.broadcasted_iota(jnp.int32, (nt, SUBLANES, GROUP_W), 1)

    def gate_step(c, carry):
        r0 = pl.multiple_of(c * CH, CH)
        xc = _conv4(_window(x_ref, c, nc, seq_len), cw_ref, cb_ref)
        xb = xc.astype(BF16)
        for d in range(2):
            rg = jax.nn.sigmoid(_dot(xb, wa_ref[d].astype(BF16)) + ba_ref[d])
            ig = jax.nn.sigmoid(_dot(xb, wx_ref[d].astype(BF16)) + bx_ref[d])
            log_a = -RG_C * rg * _softplus(-lam_ref[d])
            a = jnp.exp(log_a)
            u = jnp.sqrt(-jnp.tanh(log_a) * (a * a + 1.0)) * (ig * xc)
            a3 = a.reshape(nt, SUBLANES, GROUP_W)
            u3 = u.reshape(nt, SUBLANES, GROUP_W)
            for k in (1, 2, 4):
                if d == 0:
                    ok = sub >= k
                    a_sh, u_sh = pltpu.roll(a3, k, axis=1), pltpu.roll(u3, k, axis=1)
                else:
                    ok = sub < SUBLANES - k
                    a_sh, u_sh = pltpu.roll(a3, SUBLANES - k, axis=1), pltpu.roll(u3, SUBLANES - k, axis=1)
                u3 = u3 + a3 * jnp.where(ok, u_sh, 0.0)
                a3 = a3 * jnp.where(ok, a_sh, 1.0)
            a_s[d, pl.ds(r0, CH), :] = a3.reshape(CH, GROUP_W)
            u_s[d, pl.ds(r0, CH), :] = u3.reshape(CH, GROUP_W)
        return carry

    lax.fori_loop(0, nc, gate_step, 0)

    n_tiles = seq_len // SUBLANES
    if has_ctx:
        hf0, hb0 = h0_ref[0, 0, 0:1, :], h0_ref[0, 0, 1:2, :]
    else:
        hf0 = hb0 = jnp.zeros((1, GROUP_W), F32)

    def carry_step(i, carry):
        hf, hb = carry
        rf = pl.multiple_of(i * SUBLANES, SUBLANES)
        rb = pl.multiple_of((n_tiles - 1 - i) * SUBLANES, SUBLANES)
        tf = u_s[0, pl.ds(rf, SUBLANES), :] + a_s[0, pl.ds(rf, SUBLANES), :] * hf
        tb = u_s[1, pl.ds(rb, SUBLANES), :] + a_s[1, pl.ds(rb, SUBLANES), :] * hb
        u_s[0, pl.ds(rf, SUBLANES), :] = tf
        u_s[1, pl.ds(rb, SUBLANES), :] = tb
        return tf[SUBLANES - 1:SUBLANES, :], tb[0:1, :]

    hf, hb = lax.fori_loop(0, n_tiles, carry_step, (hf0, hb0), unroll=4)
    if not has_ctx:
        st_ref[0, 0:1, :] = hf
        st_ref[0, 1:2, :] = hb

    def out_step(c, carry):
        r0 = pl.multiple_of(c * CH, CH)
        g = g_ref[pl.ds(r0, CH), :]
        gelu = g * (0.5 * (1.0 + jnp.tanh(math.sqrt(2.0 / math.pi) * (g + 0.044715 * (g * g * g)))))
        y_ref[pl.ds(r0, CH), :] = (u_s[0, pl.ds(r0, CH), :] + u_s[1, pl.ds(r0, CH), :]) * gelu
        return carry

    lax.fori_loop(0, nc, out_step, 0)


def _rglru(has_ctx, nb, seq_len, layer, x, g, cw, cb, wa, ba, wx, bx, lam, h0=None):
    rows = pl.BlockSpec((seq_len, GROUP_W), lambda b: (b, 0))
    full = lambda a: pl.BlockSpec(a.shape, lambda b: (0,) * a.ndim)
    in_specs = [rows, rows] + [full(a) for a in (cw, cb, wa, ba, wx, bx, lam)]
    args = [x, g, cw, cb, wa, ba, wx, bx, lam]
    y_shape = jax.ShapeDtypeStruct((nb * seq_len, GROUP_W), F32)
    if has_ctx:
        in_specs.append(pl.BlockSpec((1, 1, 2, GROUP_W), lambda b: (b, layer, 0, 0)))
        args.append(h0)
        out_specs, out_shape = rows, y_shape
    else:
        out_specs = [rows, pl.BlockSpec((1, 2, GROUP_W), lambda b: (b, 0, 0))]
        out_shape = [y_shape, jax.ShapeDtypeStruct((nb, 2, GROUP_W), F32)]
    return pl.pallas_call(
        functools.partial(_rglru_kernel, has_ctx, seq_len),
        grid=(nb,),
        in_specs=in_specs,
        out_specs=out_specs,
        out_shape=out_shape,
        scratch_shapes=[pltpu.VMEM((2, seq_len, GROUP_W), F32), pltpu.VMEM((2, seq_len, GROUP_W), F32)],
        compiler_params=_cparams(1),
        name="rglru_ctx" if has_ctx else "rglru",
    )(*args)


ROUTE_OFF = N_EGROUPS
SORT_TM = 256
RUN_PAD = 16
SLOTS = SORT_TM + N_EGROUPS * RUN_PAD
SLOTS_PAD = 384
RUN_BITS = (16, 32, 64, 128, 256)
FFN_BLK = 512
GS_COLS = D_MODEL + LANES


def _routing_gate(logits):
    lane = lax.broadcasted_iota(jnp.int32, logits.shape, 1)
    lane_f = lane.astype(F32)
    neg = -jnp.inf
    big = float(LANES)
    gl = jnp.where(lane < N_EGROUPS, logits, neg)
    gmax = jnp.max(gl, axis=-1, keepdims=True)
    g_w = 1.0 / jnp.sum(jnp.exp(gl - gmax), axis=-1, keepdims=True)
    g_sel = jnp.min(jnp.where(gl == gmax, lane_f, big), axis=-1, keepdims=True)
    e_lane = lane - ROUTE_OFF
    in_grp = (e_lane >= 0) & (e_lane < N_EXPERTS) & ((e_lane // N_EPG).astype(F32) == g_sel)
    el = jnp.where(in_grp, logits, neg)
    m1 = jnp.max(el, axis=-1, keepdims=True)
    i1 = jnp.min(jnp.where(el == m1, lane_f, big), axis=-1, keepdims=True)
    el2 = jnp.where(lane_f == i1, neg, el)
    m2 = jnp.max(el2, axis=-1, keepdims=True)
    i2 = jnp.min(jnp.where(el2 == m2, lane_f, big), axis=-1, keepdims=True)
    r = jnp.exp(m2 - m1)
    p1 = 1.0 / (1.0 + r)
    p2 = r / (1.0 + r)
    return jnp.where(lane_f == i1, g_w * p1, jnp.where(lane_f == i2, g_w * p2, 0.0)), g_sel


def _route_kernel(x_ref, m_ref, mix0, mix1, mix2, mix3, wout_ref, g2_ref, wr_ref, br_ref,
                  x1_ref, h2_ref, gate_ref, cnt_ref, slot_ref):
    tm = SORT_TM
    mix = jnp.concatenate([mix0[...], mix1[...], mix2[...], mix3[...]], axis=1).astype(BF16)
    x1 = x_ref[...] + m_ref[0, 2:3] * _dot(mix, wout_ref[...])
    x1_ref[...] = x1
    h2 = _rms(x1) * g2_ref[...] * (1.0 + m_ref[0, 4:5]) + m_ref[0, 3:4]
    h2_hi = h2.astype(BF16)
    h2_ref[...] = h2_hi
    h2_lo = (h2 - h2_hi.astype(F32)).astype(BF16)
    hi = _dot(h2_hi, wr_ref[...])
    logits = hi[:, :LANES] + hi[:, LANES:] + _dot(h2_lo, wr_ref[:, :LANES]) + br_ref[...]
    gate, g_sel = _routing_gate(logits)
    gate_ref[...] = gate

    lane = lax.broadcasted_iota(jnp.int32, (tm, LANES), 1).astype(F32)
    onehot = jnp.where(lane == g_sel, 1.0, 0.0)
    cnt_ref[0] = jnp.sum(onehot, axis=0, keepdims=True)
    onehot_t = onehot.T
    earlier = jnp.where(lax.broadcasted_iota(jnp.int32, (tm, tm), 0) < lax.broadcasted_iota(jnp.int32, (tm, tm), 1),
                        1.0, 0.0).astype(BF16)
    rank_t = _dot(onehot_t.astype(BF16), earlier)
    cnt = jnp.sum(onehot_t, axis=1, keepdims=True)
    padded = jnp.ceil(cnt * (1.0 / RUN_PAD)) * RUN_PAD
    grp = lax.broadcasted_iota(jnp.int32, (LANES, 1), 0)
    start = jnp.zeros((LANES, 1), F32)
    for g in range(N_EGROUPS - 1):
        start = start + jnp.where(grp > g, padded[g:g + 1, :], 0.0)
    slot = jnp.sum(onehot_t * (rank_t + start), axis=0, keepdims=True)
    slot_ref[0] = slot


def _route(x, mod, mod_row, mixes, wout, g2, wr, br):
    t = x.shape[0]
    tm = SORT_TM
    nt = t // tm
    tok = lambda n: pl.BlockSpec((tm, n), lambda i: (i, 0))
    const = lambda a: pl.BlockSpec(a.shape, lambda i: (0,) * a.ndim)
    return pl.pallas_call(
        _route_kernel,
        grid=(nt,),
        in_specs=[tok(D_MODEL), pl.BlockSpec((1, 6, D_MODEL), lambda i: (mod_row(i), 0, 0)),
                  tok(GROUP_W), tok(GROUP_W), tok(GROUP_W), tok(GROUP_W),
                  const(wout), const(g2), const(wr), const(br)],
        out_specs=[tok(D_MODEL), tok(D_MODEL), tok(LANES),
                   pl.BlockSpec((1, 1, LANES), lambda i: (i, 0, 0)), pl.BlockSpec((1, 1, tm), lambda i: (i, 0, 0))],
        out_shape=[jax.ShapeDtypeStruct((t, D_MODEL), F32), jax.ShapeDtypeStruct((t, D_MODEL), BF16),
                   jax.ShapeDtypeStruct((t, LANES), F32),
                   jax.ShapeDtypeStruct((nt, 1, LANES), F32), jax.ShapeDtypeStruct((nt, 1, tm), F32)],
        compiler_params=_cparams(1),
        name="route",
    )(x, mod, *mixes, wout, g2, wr, br)


def _for_each_piece(n_rows, fn):
    off = 0
    for bit in RUN_BITS:
        has = (n_rows & bit) != 0
        pl.when(has)(functools.partial(fn, off, bit))
        off = off + jnp.where(has, bit, 0)


def _regroup_kernel(loc0_ref, len_ref, dst0_ref, tail0_ref, tail_len_ref, slack_ref, h2_ref, gate_ref, slot_ref,
                    gs_hbm, srt_s, zero_s, sem, zsem):
    i = pl.program_id(0)
    n = pl.num_programs(0)
    tm = SORT_TM
    start, wait = (lambda cp: cp.start()), (lambda cp: cp.wait())
    run_start, run_wait = (lambda cp, g: cp.start(priority=g % 2)), (lambda cp, g: cp.wait())

    def zero_pieces(do):
        def tail_copy(g, off, size):
            return pltpu.make_async_copy(zero_s.at[pl.ds(0, size)],
                                         gs_hbm.at[pl.ds(pl.multiple_of(tail0_ref[g] + off, RUN_PAD), size)], zsem)

        def slack_copy(j):
            size = RUN_BITS[-1]
            return pltpu.make_async_copy(zero_s, gs_hbm.at[pl.ds(pl.multiple_of(slack_ref[0] + j * size, size), size)], zsem)

        for g in range(N_EGROUPS):
            _for_each_piece(tail_len_ref[g], lambda off, size, g=g: do(tail_copy(g, off, size)))
        lax.fori_loop(0, slack_ref[1], lambda j, c: (do(slack_copy(j)), c)[1], 0)

    def run_pieces(tile, do):
        buf = tile % 2
        for g in range(N_EGROUPS):
            r = tile * N_EGROUPS + g

            def piece(off, size, r=r, g=g):
                do(pltpu.make_async_copy(
                    srt_s.at[buf, pl.ds(pl.multiple_of(loc0_ref[r] + off, RUN_PAD), size)],
                    gs_hbm.at[pl.ds(pl.multiple_of(dst0_ref[r] + off, RUN_PAD), size)], sem.at[buf]), g)

            _for_each_piece(len_ref[r], piece)

    @pl.when(i == 0)
    def _zeros():
        zero_s[...] = jnp.zeros_like(zero_s)
        zero_pieces(start)

    @pl.when(i >= 2)
    def _reuse():
        run_pieces(i - 2, run_wait)

    perm = lax.broadcasted_iota(jnp.int32, (SLOTS, tm), 0).astype(F32) == slot_ref[0]
    perm = jnp.where(perm, 1.0, 0.0).astype(BF16)
    srt_h = _dot(perm, h2_ref[...])
    gate = gate_ref[...]
    g_hi = gate.astype(BF16)
    g_mid = (gate - g_hi.astype(F32)).astype(BF16)
    g_lo = (gate - g_hi.astype(F32) - g_mid.astype(F32)).astype(BF16)
    parts = _dot(perm, jnp.concatenate([g_hi, g_mid], axis=1))
    srt_g = parts[:, :LANES] + parts[:, LANES:] + _dot(perm, g_lo)
    srt_s[i % 2] = jnp.concatenate([srt_h, srt_g], axis=1)
    run_pieces(i, run_start)

    @pl.when(i == n - 1)
    def _drain():
        pl.when(i >= 1)(lambda: run_pieces(i - 1, run_wait))
        run_pieces(i, run_wait)
        zero_pieces(wait)


def _regroup(h2, gate, slot, loc0, length, dst0, tail0, tail_len, slack, n_rows):
    t = h2.shape[0]
    tm = SORT_TM
    return pl.pallas_call(
        _regroup_kernel,
        grid_spec=pltpu.PrefetchScalarGridSpec(
            num_scalar_prefetch=6,
            grid=(t // tm,),
            in_specs=[pl.BlockSpec((tm, D_MODEL), lambda i, *_: (i, 0)),
                      pl.BlockSpec((tm, LANES), lambda i, *_: (i, 0)),
                      pl.BlockSpec((1, 1, tm), lambda i, *_: (i, 0, 0))],
            out_specs=pl.BlockSpec(memory_space=pl.ANY),
            scratch_shapes=[pltpu.VMEM((2, SLOTS, GS_COLS), F32), pltpu.VMEM((RUN_BITS[-1], GS_COLS), F32),
                            pltpu.SemaphoreType.DMA((2,)), pltpu.SemaphoreType.DMA(())],
        ),
        out_shape=jax.ShapeDtypeStruct((n_rows, GS_COLS), F32),
        compiler_params=_cparams(1),
        name="regroup",
    )(loc0, length, dst0, tail0, tail_len, slack, h2, gate, slot)


def _ffn_kernel(blk_grp_ref, n_valid_ref, gs_ref, wg_ref, wu_ref, wd_ref, ys_ref):
    b = pl.program_id(0)

    @pl.when(b >= n_valid_ref[0])
    def _unused():
        ys_ref[...] = jnp.zeros_like(ys_ref)

    @pl.when(b < n_valid_ref[0])
    def _block():
        grp = blk_grp_ref[b]
        xs = gs_ref[:, :D_MODEL].astype(BF16)
        gates = gs_ref[:, D_MODEL:]
        lane = lax.broadcasted_iota(jnp.int32, gates.shape, 1)
        hid = []
        for j in range(N_EPG):
            gcol = jnp.sum(jnp.where(lane == grp * N_EPG + j + ROUTE_OFF, gates, 0.0), axis=-1, keepdims=True)
            hj = _silu(_dot(xs, wg_ref[j])) * _dot(xs, wu_ref[j])
            hid.append((hj * gcol).astype(BF16))
        ys_ref[...] = _dot(jnp.concatenate(hid, axis=1), wd_ref[...]).astype(BF16)


def _group_ffn(gs, blk_grp, n_valid, wg, wu, wd):
    nblk = gs.shape[0] // FFN_BLK
    live = lambda b, bg, nv: jnp.minimum(b, nv[0] - 1)
    return pl.pallas_call(
        _ffn_kernel,
        grid_spec=pltpu.PrefetchScalarGridSpec(
            num_scalar_prefetch=2,
            grid=(nblk,),
            in_specs=[
                pl.BlockSpec((FFN_BLK, GS_COLS), lambda b, bg, nv: (live(b, bg, nv), 0)),
                pl.BlockSpec((N_EPG, D_MODEL, EXPERT_FF), lambda b, bg, nv: (bg[b], 0, 0)),
                pl.BlockSpec((N_EPG, D_MODEL, EXPERT_FF), lambda b, bg, nv: (bg[b], 0, 0)),
                pl.BlockSpec((N_EPG * EXPERT_FF, D_MODEL), lambda b, bg, nv: (bg[b], 0)),
            ],
            out_specs=pl.BlockSpec((FFN_BLK, D_MODEL), lambda b, bg, nv: (b, 0)),
        ),
        out_shape=jax.ShapeDtypeStruct((nblk * FFN_BLK, D_MODEL), BF16),
        compiler_params=_cparams(1),
        name="group_ffn",
    )(blk_grp, n_valid, gs, wg, wu, wd)


def _combine_kernel(final, loc0_ref, len_ref, dst0_ref, x1_ref, m_ref, slot_ref, gfin_ref, ys_hbm, o_ref, run_s, sem):
    i = pl.program_id(0)
    tm = SORT_TM

    def fetch(tile, do):
        buf = tile % 2
        for g in range(N_EGROUPS):
            r = tile * N_EGROUPS + g

            def piece(off, size, r=r, g=g):
                do(pltpu.make_async_copy(
                    ys_hbm.at[pl.ds(pl.multiple_of(dst0_ref[r] + off, RUN_PAD), size)],
                    run_s.at[buf, pl.ds(pl.multiple_of(loc0_ref[r] + off, RUN_PAD), size)], sem.at[buf]), g)

            _for_each_piece(len_ref[r], piece)

    @pl.when(i == 0)
    def _first():
        fetch(i, lambda cp, g: cp.start(priority=g % 2))

    @pl.when(i + 1 < pl.num_programs(0))
    def _prefetch():
        fetch(i + 1, lambda cp, g: cp.start(priority=g % 2))

    fetch(i, lambda cp, g: cp.wait())

    used = loc0_ref[i * N_EGROUPS + N_EGROUPS - 1] + len_ref[i * N_EGROUPS + N_EGROUPS - 1]
    rows = lax.broadcasted_iota(jnp.int32, (SLOTS_PAD, D_MODEL), 0)
    y_run = jnp.where(rows < used, run_s[i % 2], jnp.zeros((), BF16))
    slot_b = jnp.broadcast_to(slot_ref[0], (LANES, tm)).T
    lane = lax.broadcasted_iota(jnp.int32, (tm, LANES), 1).astype(F32)
    inv = jnp.concatenate([jnp.where(slot_b == lane + float(c), 1.0, 0.0) for c in range(0, SLOTS_PAD, LANES)],
                          axis=1).astype(BF16)
    x2 = x1_ref[...] + m_ref[0, 5:6] * _dot(inv, y_run)
    o_ref[...] = _rms(x2) * gfin_ref[...] if final else x2


def _combine(final, x1, mod, mod_row, slot, gfin, ys, loc0, length, dst0):
    t = x1.shape[0]
    tm = SORT_TM
    return pl.pallas_call(
        functools.partial(_combine_kernel, final),
        grid_spec=pltpu.PrefetchScalarGridSpec(
            num_scalar_prefetch=3,
            grid=(t // tm,),
            in_specs=[
                pl.BlockSpec((tm, D_MODEL), lambda i, *_: (i, 0)),
                pl.BlockSpec((1, 6, D_MODEL), lambda i, *_: (mod_row(i), 0, 0)),
                pl.BlockSpec((1, 1, tm), lambda i, *_: (i, 0, 0)),
                pl.BlockSpec((1, D_MODEL), lambda i, *_: (0, 0)),
                pl.BlockSpec(memory_space=pl.ANY),
            ],
            out_specs=pl.BlockSpec((tm, D_MODEL), lambda i, *_: (i, 0)),
            scratch_shapes=[pltpu.VMEM((2, SLOTS_PAD, D_MODEL), BF16), pltpu.SemaphoreType.DMA((2,))],
        ),
        out_shape=jax.ShapeDtypeStruct((t, D_MODEL), F32),
        compiler_params=_cparams(1),
        name="combine",
    )(loc0, length, dst0, x1, mod, slot, gfin, ys)


def _outproj_moe(final, x, mod, mod_row, mixes, wout, g2, wr, br, wg, wu, wd, gfin):
    t = x.shape[0]
    nt = t // SORT_TM
    x1, h2, gate, cnt, slot = _route(x, mod, mod_row, mixes, wout, g2, wr, br)
    cnt = cnt[:, 0, :N_EGROUPS].astype(jnp.int32)
    length = (cnt + RUN_PAD - 1) // RUN_PAD * RUN_PAD
    loc0 = jnp.cumsum(length, axis=1) - length
    g_rows = jnp.sum(length, axis=0)
    g_blocks = (g_rows + FFN_BLK - 1) // FFN_BLK
    g_base = (jnp.cumsum(g_blocks) - g_blocks) * FFN_BLK
    dst0 = g_base[None, :] + jnp.cumsum(length, axis=0) - length
    n_blocks = (t + nt * N_EGROUPS * (RUN_PAD - 1) + FFN_BLK - 1) // FFN_BLK + N_EGROUPS
    blk_grp = jnp.minimum(jnp.sum(jnp.arange(n_blocks)[:, None] >= jnp.cumsum(g_blocks)[None, :], axis=1),
                          N_EGROUPS - 1).astype(jnp.int32)
    n_valid = jnp.sum(g_blocks).astype(jnp.int32)[None]
    flat = lambda a: a.reshape(-1).astype(jnp.int32)
    slack = jnp.stack([n_valid[0] * FFN_BLK, (n_blocks - n_valid[0]) * (FFN_BLK // RUN_BITS[-1])]).astype(jnp.int32)
    gs = _regroup(h2, gate, slot, flat(loc0), flat(length), flat(dst0), flat(g_base + g_rows),
                  flat(g_blocks * FFN_BLK - g_rows), slack, n_blocks * FFN_BLK)
    ys = _group_ffn(gs, blk_grp, n_valid, wg, wu, wd)
    return _combine(final, x1, mod, mod_row, slot, gfin, ys, flat(loc0), flat(length), flat(dst0))


def _block_diag(w):
    n, k, _ = w.shape
    eye = jnp.eye(n, dtype=w.dtype)
    return (eye[:, None, :, None] * w[:, :, None, :]).reshape(n * k, n * k)


def _pad_lanes(v, n=LANES):
    return jnp.pad(v, ((0, 0), (0, n - v.shape[-1])))


def kernel(x_prompt, x_sample, c, cache_k, cache_v, state_ssd, state_rglru, c_ctx, w_mod, b_mod, norm1_g, norm2_g, w_in, w_out, ssd_conv_w, ssd_conv_b, ssd_dt_bias, ssd_a_log, ssd_d, ssd_norm_g, pool_w, pool_scale, da_lam_q1, da_lam_k1, da_lam_q2, da_lam_k2, da_norm_g, rg_conv_w, rg_conv_b, rg_wa, rg_ba, rg_wx, rg_bx, rg_lambda, moe_w_group, moe_b_group, moe_w_expert, moe_b_expert, moe_w_gate, moe_w_up, moe_w_down, final_norm_g):
    nbp, lp, _ = x_prompt.shape
    nbs, ls, _ = x_sample.shape
    past = cache_k.shape[2]
    tm_in = 1024
    assert nbs + 1 <= MOD_ROWS and lp % CH == 0 and ls % CH == 0
    assert tm_in % lp == 0 and (nbp * lp) % tm_in == 0 and ls % tm_in == 0 and lp % SORT_TM == 0 and ls % SORT_TM == 0
    assert lp % KEY_BLK == 0 and ls % KEY_BLK == 0 and past % KEY_BLK == 0

    cond = jnp.concatenate([c_ctx[None, :], c, jnp.zeros((MOD_ROWS - 1 - nbs, D_MODEL), F32)], axis=0)
    mod = _modulation(cond, w_mod, b_mod).reshape(DEPTH * MOD_ROWS, 6, D_MODEL)

    w_in_r = jnp.concatenate([w_in[:, :, :DT_LO], w_in[:, :, DT_HI:], w_in[:, :, DT_LO:DT_HI],
                              jnp.zeros((DEPTH, D_MODEL, LANES - (DT_HI - DT_LO)), F32)], axis=-1).astype(BF16)
    w_out_b = w_out.astype(BF16)
    w_gate_b = moe_w_gate.astype(BF16)
    w_up_b = moe_w_up.astype(BF16)
    w_down_b = moe_w_down.astype(BF16).reshape(DEPTH, N_EXPERTS * EXPERT_FF, D_MODEL)
    w_route = jnp.concatenate([moe_w_group, moe_w_expert,
                               jnp.zeros((DEPTH, D_MODEL, LANES - N_EGROUPS - N_EXPERTS), F32)], axis=-1)
    w_route_hi = w_route.astype(BF16)
    w_route = jnp.concatenate([w_route_hi, (w_route - w_route_hi.astype(F32)).astype(BF16)], axis=-1)
    b_route = _pad_lanes(jnp.concatenate([moe_b_group, moe_b_expert], axis=-1))
    dtb = _pad_lanes(ssd_dt_bias.reshape(DEPTH, 2 * SSD_HEADS))
    alog = _pad_lanes(ssd_a_log.reshape(DEPTH, 2 * SSD_HEADS))
    d_skip = jnp.repeat(ssd_d, SSD_HEADDIM, axis=-1)
    cos, sin = _rope_tables(ls)
    ck = cache_k.reshape(nbs, DEPTH, past, GROUP_W)
    cv = cache_v.reshape(nbs, DEPTH, past, GROUP_W)
    g_fin = final_norm_g[None, :]

    xp = x_prompt.reshape(nbp * lp, D_MODEL)
    xs = x_sample.reshape(nbs * ls, D_MODEL)
    new_k = jnp.zeros((nbp, DEPTH, lp, GROUP_W), F32)
    new_v = jnp.zeros((nbp, DEPTH, lp, GROUP_W), F32)
    ssd_out, rg_out = [], []
    for l in range(DEPTH):
        row1 = lambda a: a[l][None, :]
        final = l == DEPTH - 1
        lam_init = 0.8 - 0.6 * math.exp(-0.3 * l)
        ssd_w = (ssd_conv_w[l], row1(ssd_conv_b), row1(dtb), row1(alog), row1(d_skip), row1(ssd_norm_g))
        pool_wb = _block_diag(pool_w[l])
        att_w = (row1(da_lam_q1), row1(da_lam_k1), row1(da_lam_q2), row1(da_lam_k2), row1(da_norm_g))
        rg_w = (rg_conv_w[l], row1(rg_conv_b),
                jnp.stack([_block_diag(rg_wa[l, 0]), _block_diag(rg_wa[l, 1])]), rg_ba[l][:, None, :],
                jnp.stack([_block_diag(rg_wx[l, 0]), _block_diag(rg_wx[l, 1])]), rg_bx[l][:, None, :],
                rg_lambda[l][:, None, :])
        moe_w = (w_out_b[l], row1(norm2_g), w_route[l], row1(b_route), w_gate_b[l], w_up_b[l], w_down_b[l], g_fin)
        ctx_row = lambda i, l=l: l * MOD_ROWS
        lat_row = lambda tm: (lambda i, l=l: l * MOD_ROWS + 1 + i // (ls // tm))

        xbc, z, xpool, q, new_k, new_v, xr, gr, dt = _inproj(
            xp, mod, ctx_row, row1(norm1_g), w_in_r[l], tm_in, l, caches=(new_k, new_v))
        ya, st_ssd = _ssd(False, nbp, lp, l, xbc, z, dt, *ssd_w)
        yb = _pool(nbp, lp, xpool, pool_wb, row1(pool_scale))
        yc = _attn(False, nbp, lp, l, lam_init, q, new_k, new_v, *att_w)
        yd, st_rg = _rglru(False, nbp, lp, l, xr, gr, *rg_w)
        xp = _outproj_moe(final, xp, mod, ctx_row, (ya, yb, yc, yd), *moe_w)
        ssd_out.append(st_ssd)
        rg_out.append(st_rg)

        xbc, z, xpool, q, k, v, xr, gr, dt = _inproj(xs, mod, lat_row(tm_in), row1(norm1_g), w_in_r[l], tm_in, l)
        ya = _ssd(True, nbs, ls, l, xbc, z, dt, *ssd_w, h0=state_ssd)
        yb = _pool(nbs, ls, xpool, pool_wb, row1(pool_scale))
        yc = _attn(True, nbs, ls, l, lam_init, q, k, v, *att_w, ck=ck, cv=cv, cos=cos, sin=sin)
        yd = _rglru(True, nbs, ls, l, xr, gr, *rg_w, h0=state_rglru)
        xs = _outproj_moe(final, xs, mod, lat_row(SORT_TM), (ya, yb, yc, yd), *moe_w)

    return (xp.reshape(nbp, lp, D_MODEL), xs.reshape(nbs, ls, D_MODEL),
            new_k.reshape(nbp, DEPTH, lp, DA_HEADS, 2 * DA_QKDIM), new_v.reshape(nbp, DEPTH, lp, DA_HEADS, DA_VDIM),
            jnp.stack(ssd_out, axis=1), jnp.stack(rg_out, axis=1))
```
